```python
import math
import jax, jax.numpy as jnp
from jax import lax
import numpy as np

D_MODEL = 1024
BATCH = 8
SEQ = 4096
DEPTH = 2
DEC_BATCH = 128
DEC_SEQ = 1
PAST_LEN = 16384
PAGE_SIZE = 128

N_A_LAYERS = DEPTH // 2
N_B_LAYERS = DEPTH - N_A_LAYERS
GLA_HEADS = 4
GLA_DK = D_MODEL // 2 // GLA_HEADS
GLA_DV = D_MODEL // GLA_HEADS
GLA_GATE_RANK = 16
GLA_GATE_NORM = 16.0
GLA_CHUNK = 64
MLA_HEADS = 8
MLA_Q_RANK = 3 * D_MODEL // 8
MLA_KV_RANK = D_MODEL // 4
MLA_NOPE = 64
MLA_ROPE = 32
MLA_QK = MLA_NOPE + MLA_ROPE
MLA_V = 64
ROPE_THETA = 10000.0
Q_BLOCK = 128
MOE_GROUPS = 4
MOE_EXP_PER_GROUP = 4
MOE_EXPERTS = MOE_GROUPS * MOE_EXP_PER_GROUP
MOE_TOP_K = 2
MOE_D_FF = D_MODEL // 4
NORM_EPS = 1e-6

kernel_name = 'yoco_gla_mla_hmoe_step'

F32 = jnp.float32


def rms_norm(x, g):
    xf = x.astype(F32)
    y = xf * lax.rsqrt(jnp.mean(xf * xf, axis=-1, keepdims=True) + NORM_EPS)
    return (y * g.astype(F32)).astype(x.dtype)


def rope_tables(pos):
    inv = ROPE_THETA ** (-jnp.arange(0, MLA_ROPE, 2, dtype=F32) / MLA_ROPE)
    ang = pos.astype(F32)[:, None] * inv[None, :]
    ang = jnp.concatenate([ang, ang], axis=-1)
    return jnp.cos(ang), jnp.sin(ang)


def apply_rope(x, cos, sin):
    half = x.shape[-1] // 2
    xf = x.astype(F32)
    rot = jnp.concatenate([-xf[..., half:], xf[..., :half]], axis=-1)
    return (xf * cos + rot * sin).astype(x.dtype)


def gla_recurrence(q, k, v, gk, s0):
    bsz, L, H, dk = q.shape
    dv = v.shape[-1]
    C = math.gcd(L, GLA_CHUNK)
    n = L // C

    def chunks(t):
        return t.astype(F32).reshape(bsz, n, C, H, t.shape[-1]).transpose(1, 0, 3, 2, 4)

    causal = jnp.tril(jnp.ones((C, C), dtype=bool))[:, :, None]

    def step(S, inp):
        qc, kc, vc, gc = inp
        b = jnp.cumsum(gc, axis=2)
        diff = b[:, :, :, None, :] - b[:, :, None, :, :]
        decay = jnp.exp(jnp.where(causal, diff, -jnp.inf))
        attn = jnp.einsum('bhtd,bhsd,bhtsd->bhts', qc, kc, decay)
        o = (jnp.einsum('bhts,bhsv->bhtv', attn, vc)
             + jnp.einsum('bhtd,bhdv->bhtv', qc * jnp.exp(b), S))
        b_last = b[:, :, -1:, :]
        S_new = (S * jnp.exp(b_last[:, :, 0, :])[..., None]
                 + jnp.einsum('bhsd,bhsv->bhdv', kc * jnp.exp(b_last - b), vc))
        return S_new, o

    s_fin, o = lax.scan(step, s0.astype(F32), (chunks(q), chunks(k), chunks(v), chunks(gk)))
    o = o.transpose(1, 0, 3, 2, 4).reshape(bsz, L, H, dv)
    return o, s_fin


def gla_mix(h, w_in, w_gate_up, b_gate, out_norm, w_out, s0):
    bsz, L, _ = h.shape
    hk = GLA_HEADS * GLA_DK
    hv = GLA_HEADS * GLA_DV
    proj = h @ w_in
    q, k, v, r, g_low = jnp.split(proj, [hk, 2 * hk, 2 * hk + hv, 2 * hk + 2 * hv], axis=-1)
    gk = jax.nn.log_sigmoid((g_low @ w_gate_up + b_gate).astype(F32)) / GLA_GATE_NORM

    def heads(t, d):
        return t.reshape(bsz, L, GLA_HEADS, d)

    o, s_fin = gla_recurrence(heads(q, GLA_DK) * (GLA_DK ** -0.5), heads(k, GLA_DK),
                              heads(v, GLA_DV), heads(gk, GLA_DK), s0)
    o = rms_norm(o, out_norm) * jax.nn.silu(heads(r, GLA_DV).astype(F32))
    return o.reshape(bsz, L, hv).astype(h.dtype) @ w_out, s_fin.astype(s0.dtype)


def mla_queries(h, w_dq, q_lat_norm, w_uq, q_norm, cos, sin):
    bsz, L, _ = h.shape
    cq = rms_norm(h @ w_dq, q_lat_norm)
    q = (cq @ w_uq).reshape(bsz, L, MLA_HEADS, MLA_QK)
    q = jnp.concatenate([q[..., :MLA_NOPE], apply_rope(q[..., MLA_NOPE:], cos[:, None, :], sin[:, None, :])], axis=-1)
    return rms_norm(q, q_norm)


def mla_shared_kv(s, kv_in_norm, w_dkv, kv_lat_norm, cos, sin):
    kv = rms_norm(s, kv_in_norm) @ w_dkv
    ckv = rms_norm(kv[..., :MLA_KV_RANK], kv_lat_norm)
    kr = apply_rope(kv[..., MLA_KV_RANK:], cos, sin)
    return ckv, kr


def mla_keys(ckv, kr, w_uk, k_norm):
    kn = jnp.einsum('...pr,rhd->...phd', ckv, w_uk)
    krb = jnp.broadcast_to(kr[..., None, :], kn.shape[:-1] + (MLA_ROPE,))
    return rms_norm(jnp.concatenate([kn, krb], axis=-1), k_norm)


def mla_attend_prompt(q, ckv, kr, w_uk, w_uv, k_norm):
    bsz, S, H, _ = q.shape
    k = mla_keys(ckv, kr, w_uk, k_norm)
    v = jnp.einsum('bsr,rhd->bshd', ckv, w_uv)
    nb = S // Q_BLOCK
    qb = q.reshape(bsz, nb, Q_BLOCK, H, MLA_QK).transpose(1, 0, 2, 3, 4)
    key_pos = jnp.arange(S)
    scale = MLA_QK ** -0.5

    def block(args):
        qi, i = args
        s = jnp.einsum('bqhd,bkhd->bhqk', qi, k).astype(F32) * scale
        qpos = i * Q_BLOCK + jnp.arange(Q_BLOCK)
        s = jnp.where(key_pos[None, :] <= qpos[:, None], s, -jnp.inf)
        p = jax.nn.softmax(s, axis=-1)
        return jnp.einsum('bhqk,bkhd->bqhd', p.astype(v.dtype), v)

    o = lax.map(block, (qb, jnp.arange(nb)))
    return o.transpose(1, 0, 2, 3, 4).reshape(bsz, S, H, MLA_V)


def mla_attend_sample(q, ckv_new, kr_new, cache_ckv, cache_krope, page_table, w_uk, w_uv, k_norm):
    db, L, H, _ = q.shape
    qf = q.astype(F32)
    scale = MLA_QK ** -0.5

    def page_step(carry, page_ids):
        m, l, acc = carry
        c = cache_ckv[page_ids]
        kr = cache_krope[page_ids]
        k = mla_keys(c, kr, w_uk, k_norm)
        s = jnp.einsum('bqhd,bphd->bqhp', qf, k.astype(F32)) * scale
        m_new = jnp.maximum(m, jnp.max(s, axis=-1))
        corr = jnp.exp(m - m_new)
        p = jnp.exp(s - m_new[..., None])
        l = l * corr + jnp.sum(p, axis=-1)
        acc = acc * corr[..., None] + jnp.einsum('bqhp,bpr->bqhr', p, c.astype(F32))
        return (m_new, l, acc), None

    init = (jnp.full((db, L, H), -jnp.inf, F32), jnp.zeros((db, L, H), F32),
            jnp.zeros((db, L, H, MLA_KV_RANK), F32))
    (m, l, acc), _ = lax.scan(page_step, init, page_table.T)
    k_new = mla_keys(ckv_new, kr_new, w_uk, k_norm)
    s = jnp.einsum('bqhd,bkhd->bqhk', qf, k_new.astype(F32)) * scale
    causal = jnp.tril(jnp.ones((L, L), dtype=bool))[None, :, None, :]
    s = jnp.where(causal, s, -jnp.inf)
    m_new = jnp.maximum(m, jnp.max(s, axis=-1))
    corr = jnp.exp(m - m_new)
    p = jnp.exp(s - m_new[..., None])
    l = l * corr + jnp.sum(p, axis=-1)
    acc = acc * corr[..., None] + jnp.einsum('bqhk,bkr->bqhr', p, ckv_new.astype(F32))
    o_lat = acc / l[..., None]
    return jnp.einsum('bqhr,rhd->bqhd', o_lat, w_uv.astype(F32)).astype(q.dtype)


def hier_moe(h, w_group, b_group, w_router, b_router, w_gate, w_up, w_down):
    bsz, L, D = h.shape
    t = h.reshape(bsz * L, D)
    p_group = jax.nn.softmax((t @ w_group + b_group).astype(F32), axis=-1)
    g_val, g_idx = lax.top_k(p_group, 1)
    g_onehot = jax.nn.one_hot(g_idx[:, 0], MOE_GROUPS, dtype=F32)
    logits_e = (t @ w_router + b_router).astype(F32).reshape(-1, MOE_GROUPS, MOE_EXP_PER_GROUP)
    p_e = jax.nn.softmax(jnp.einsum('tg,tge->te', g_onehot, logits_e), axis=-1)
    e_val, e_idx = lax.top_k(p_e, MOE_TOP_K)
    w_sel = g_val * e_val / jnp.sum(e_val, axis=-1, keepdims=True)
    expert_id = g_idx * MOE_EXP_PER_GROUP + e_idx
    gates = jnp.einsum('tk,tke->te', w_sel, jax.nn.one_hot(expert_id, MOE_EXPERTS, dtype=F32))
    hid = jax.nn.silu(jnp.einsum('td,edf->tef', t, w_gate)) * jnp.einsum('td,edf->tef', t, w_up)
    y = jnp.einsum('tef,efd->td', hid * gates[..., None].astype(hid.dtype), w_down)
    return y.reshape(bsz, L, D)


def trunk(x, c, pos, gla_state_in, paged, w):
    bsz, L, _ = x.shape
    cos, sin = rope_tables(pos)
    ckv = kr = None
    new_gla = []
    for l in range(DEPTH):
        mod = (jax.nn.silu(c) @ w['ada_w'][l] + w['ada_b'][l])[:, None, :]
        sh_a, sc_a, g_a, sh_f, sc_f, g_f = jnp.split(mod, 6, axis=-1)
        hm = rms_norm(x, w['mix_norm'][l]) * (1.0 + sc_a) + sh_a
        if l < N_A_LAYERS:
            a = l
            m, st = gla_mix(hm, w['gla_w_in'][a], w['gla_w_gate_up'][a], w['gla_b_gate'][a],
                            w['gla_out_norm'][a], w['gla_w_out'][a], gla_state_in[a])
            new_gla.append(st)
        else:
            b = l - N_A_LAYERS
            q = mla_queries(hm, w['mla_w_dq'][b], w['mla_q_lat_norm'][b], w['mla_w_uq'][b],
                            w['mla_q_norm'][b], cos, sin)
            if paged is None:
                o = mla_attend_prompt(q, ckv, kr, w['kv_w_uk'], w['kv_w_uv'], w['kv_k_norm'])
            else:
                o = mla_attend_sample(q, ckv, kr, paged[0], paged[1], paged[2],
                                      w['kv_w_uk'], w['kv_w_uv'], w['kv_k_norm'])
            m = o.reshape(bsz, L, MLA_HEADS * MLA_V) @ w['mla_w_out'][b]
        x = x + g_a * m
        hf = rms_norm(x, w['ffn_norm'][l]) * (1.0 + sc_f) + sh_f
        x = x + g_f * hier_moe(hf, w['moe_w_group'][l], w['moe_b_group'][l], w['moe_w_router'][l],
                               w['moe_b_router'][l], w['moe_w_gate'][l], w['moe_w_up'][l], w['moe_w_down'][l])
        if l == N_A_LAYERS - 1:
            ckv, kr = mla_shared_kv(x, w['kv_in_norm'], w['kv_w_dkv'], w['kv_lat_norm'], cos, sin)
    return x, jnp.stack(new_gla, axis=0), ckv, kr


def setup_inputs(seed: int = 0) -> dict:
    key = jax.random.key(seed)
    ks = iter(jax.random.split(key, 64))

    def nrm(shape, scale):
        return jax.random.normal(next(ks), shape, F32) * scale

    def gain(shape):
        return 1.0 + nrm(shape, 0.02)

    n_pages = PAST_LEN // PAGE_SIZE
    n_used = DEC_BATCH * n_pages
    n_pool = n_used + max(1, n_used // 4)
    page_table = jax.random.permutation(next(ks), n_pool)[:n_used].reshape(DEC_BATCH, n_pages).astype(jnp.int32)
    D = D_MODEL
    gla_in = 2 * GLA_HEADS * GLA_DK + 2 * GLA_HEADS * GLA_DV + GLA_GATE_RANK
    return {
        'x_prompt': nrm((BATCH, SEQ, D), 1.0),
        'x_sample': nrm((DEC_BATCH, DEC_SEQ, D), 1.0),
        'c_prompt': nrm((BATCH, D), 1.0),
        'c_sample': nrm((DEC_BATCH, D), 1.0),
        'state_gla': nrm((N_A_LAYERS, DEC_BATCH, GLA_HEADS, GLA_DK, GLA_DV), 0.5),
        'cache_ckv': nrm((n_pool, PAGE_SIZE, MLA_KV_RANK), 1.0),
        'cache_krope': nrm((n_pool, PAGE_SIZE, MLA_ROPE), 1.0),
        'page_table': page_table,
        'ada_w': nrm((DEPTH, D, 6 * D), 0.5 * D ** -0.5),
        'ada_b': nrm((DEPTH, 6 * D), 0.02),
        'mix_norm': gain((DEPTH, D)),
        'ffn_norm': gain((DEPTH, D)),
        'gla_w_in': nrm((N_A_LAYERS, D, gla_in), D ** -0.5),
        'gla_w_gate_up': nrm((N_A_LAYERS, GLA_GATE_RANK, GLA_HEADS * GLA_DK), GLA_GATE_RANK ** -0.5),
        'gla_b_gate': nrm((N_A_LAYERS, GLA_HEADS * GLA_DK), 0.1),
        'gla_out_norm': gain((N_A_LAYERS, GLA_DV)),
        'gla_w_out': nrm((N_A_LAYERS, GLA_HEADS * GLA_DV, D), (GLA_HEADS * GLA_DV) ** -0.5),
        'mla_w_dq': nrm((N_B_LAYERS, D, MLA_Q_RANK), D ** -0.5),
        'mla_q_lat_norm': gain((N_B_LAYERS, MLA_Q_RANK)),
        'mla_w_uq': nrm((N_B_LAYERS, MLA_Q_RANK, MLA_HEADS * MLA_QK), MLA_Q_RANK ** -0.5),
        'mla_q_norm': gain((N_B_LAYERS, MLA_QK)),
        'mla_w_out': nrm((N_B_LAYERS, MLA_HEADS * MLA_V, D), (MLA_HEADS * MLA_V) ** -0.5),
        'kv_in_norm': gain((D,)),
        'kv_w_dkv': nrm((D, MLA_KV_RANK + MLA_ROPE), D ** -0.5),
        'kv_lat_norm': gain((MLA_KV_RANK,)),
        'kv_w_uk': nrm((MLA_KV_RANK, MLA_HEADS, MLA_NOPE), MLA_KV_RANK ** -0.5),
        'kv_w_uv': nrm((MLA_KV_RANK, MLA_HEADS, MLA_V), MLA_KV_RANK ** -0.5),
        'kv_k_norm': gain((MLA_QK,)),
        'moe_w_group': nrm((DEPTH, D, MOE_GROUPS), D ** -0.5),
        'moe_b_group': nrm((DEPTH, MOE_GROUPS), 0.01),
        'moe_w_router': nrm((DEPTH, D, MOE_EXPERTS), D ** -0.5),
        'moe_b_router': nrm((DEPTH, MOE_EXPERTS), 0.01),
        'moe_w_gate': nrm((DEPTH, MOE_EXPERTS, D, MOE_D_FF), D ** -0.5),
        'moe_w_up': nrm((DEPTH, MOE_EXPERTS, D, MOE_D_FF), D ** -0.5),
        'moe_w_down': nrm((DEPTH, MOE_EXPERTS, MOE_D_FF, D), MOE_D_FF ** -0.5),
    }


def reference(x_prompt, x_sample, c_prompt, c_sample, state_gla, cache_ckv, cache_krope, page_table,
              ada_w, ada_b, mix_norm, ffn_norm,
              gla_w_in, gla_w_gate_up, gla_b_gate, gla_out_norm, gla_w_out,
              mla_w_dq, mla_q_lat_norm, mla_w_uq, mla_q_norm, mla_w_out,
              kv_in_norm, kv_w_dkv, kv_lat_norm, kv_w_uk, kv_w_uv, kv_k_norm,
              moe_w_group, moe_b_group, moe_w_router, moe_b_router, moe_w_gate, moe_w_up, moe_w_down):
    w = dict(ada_w=ada_w, ada_b=ada_b, mix_norm=mix_norm, ffn_norm=ffn_norm,
             gla_w_in=gla_w_in, gla_w_gate_up=gla_w_gate_up, gla_b_gate=gla_b_gate,
             gla_out_norm=gla_out_norm, gla_w_out=gla_w_out,
             mla_w_dq=mla_w_dq, mla_q_lat_norm=mla_q_lat_norm, mla_w_uq=mla_w_uq,
             mla_q_norm=mla_q_norm, mla_w_out=mla_w_out,
             kv_in_norm=kv_in_norm, kv_w_dkv=kv_w_dkv, kv_lat_norm=kv_lat_norm,
             kv_w_uk=kv_w_uk, kv_w_uv=kv_w_uv, kv_k_norm=kv_k_norm,
             moe_w_group=moe_w_group, moe_b_group=moe_b_group, moe_w_router=moe_w_router,
             moe_b_router=moe_b_router, moe_w_gate=moe_w_gate, moe_w_up=moe_w_up, moe_w_down=moe_w_down)
    gla_zero = jnp.zeros((N_A_LAYERS, x_prompt.shape[0], GLA_HEADS, GLA_DK, GLA_DV), state_gla.dtype)
    y_prompt, state_gla_prompt, ckv_prompt, krope_prompt = trunk(
        x_prompt, c_prompt, jnp.arange(SEQ), gla_zero, None, w)
    y_sample, state_gla_sample, ckv_sample, krope_sample = trunk(
        x_sample, c_sample, PAST_LEN + jnp.arange(DEC_SEQ), state_gla,
        (cache_ckv, cache_krope, page_table), w)
    return (y_prompt, y_sample, state_gla_prompt, state_gla_sample, ckv_prompt, krope_prompt, ckv_sample, krope_sample)
```

```python
import functools

import jax
import jax.numpy as jnp
from jax import lax
from jax.experimental import pallas as pl
from jax.experimental.pallas import tpu as pltpu

F32 = jnp.float32
BF16 = jnp.bfloat16

NORM_EPS = 1e-6
GLA_HEADS = 4
GLA_GATE_NORM = 16.0
GLA_CHUNK = 64
MLA_HEADS = 8
MLA_NOPE = 64
MLA_ROPE = 32
MLA_QK = MLA_NOPE + MLA_ROPE
MLA_V = 64
ROPE_THETA = 10000.0
MOE_GROUPS = 4
MOE_EXP_PER_GROUP = 4
MOE_EXPERTS = MOE_GROUPS * MOE_EXP_PER_GROUP
LANES = 128
NEG_BIG = -1e30
VMEM_LIMIT = 52 * 1024 * 1024


def _params(*sem):
    return pltpu.CompilerParams(dimension_semantics=sem, vmem_limit_bytes=VMEM_LIMIT)


def _dot(a, b):
    return jnp.dot(a, b, preferred_element_type=F32)


def _dot_nt(a, b):
    return lax.dot_general(a, b, (((1,), (1,)), ((), ())), preferred_element_type=F32)


def _dot_tn(a, b):
    return lax.dot_general(a, b, (((0,), (0,)), ((), ())), preferred_element_type=F32)


def _rms(x, g):
    return x * lax.rsqrt(jnp.mean(x * x, axis=-1, keepdims=True) + NORM_EPS) * g


def _silu(x):
    return x / (1.0 + jnp.exp(-x))


def _split_bf16(x):
    hi = x.astype(BF16)
    lo = (x - hi.astype(F32)).astype(BF16)
    return hi, lo


def _ada_kernel(c_ref, w_ref, b_ref, o_ref):
    a = _silu(c_ref[...]).astype(BF16)
    o_ref[0] = _dot(a, w_ref[0].astype(BF16)) + b_ref[0]


def _ada_mod(c, ada_w, ada_b):
    depth, d, n = ada_w.shape
    r = c.shape[0]
    tn = 1536
    return pl.pallas_call(
        _ada_kernel,
        grid=(depth, n // tn),
        in_specs=[
            pl.BlockSpec((r, d), lambda l, j: (0, 0)),
            pl.BlockSpec((1, d, tn), lambda l, j: (l, 0, j)),
            pl.BlockSpec((1, 1, tn), lambda l, j: (l, 0, j)),
        ],
        out_specs=pl.BlockSpec((1, r, tn), lambda l, j: (l, 0, j)),
        out_shape=jax.ShapeDtypeStruct((depth, r, n), F32),
        compiler_params=_params("arbitrary", "arbitrary"),
        name="ada_mod",
    )(c, ada_w, ada_b.reshape(depth, 1, n))


def _mod_spec(mod, tm, d, chunk):
    if mod.shape[1] == 1:
        return pl.BlockSpec((1, 1, d), lambda b, t: (b, 0, chunk))
    return pl.BlockSpec((1, tm, d), lambda b, t: (b, t, chunk))


def _const_spec(arr):
    nd = arr.ndim
    return pl.BlockSpec(arr.shape, lambda b, t: (0,) * nd)


def _gla_proj_kernel(x_ref, sh_ref, sc_ref, g_ref, wq_ref, wk_ref, wv_ref, wr_ref, wg_ref, wgu_ref, bg_ref,
                     q_ref, k_ref, v_ref, r_ref, gk_ref, *, q_scale):
    hm = _rms(x_ref[0], g_ref[...]) * (1.0 + sc_ref[0]) + sh_ref[0]
    hb = hm.astype(BF16)
    q_ref[0] = _dot(hb, wq_ref[...]) * q_scale
    k_ref[0] = _dot(hb, wk_ref[...])
    v_ref[0] = _dot(hb, wv_ref[...]).astype(BF16)
    r_ref[0] = _dot(hb, wr_ref[...])
    g_low = _dot(hb, wg_ref[...])
    z = _dot(g_low.astype(BF16), wgu_ref[...]) + bg_ref[...]
    log_sig = jnp.minimum(z, 0.0) - jnp.log(1.0 + jnp.exp(-jnp.abs(z)))
    gk_ref[0] = log_sig * (1.0 / GLA_GATE_NORM)


def _gla_proj(x, mod, norm_g, w_in, w_gate_up, b_gate, tm):
    bsz, L, d = x.shape
    hk = w_gate_up.shape[1]
    hv = (w_in.shape[1] - 2 * hk - w_gate_up.shape[0]) // 2
    wq = w_in[:, :hk].astype(BF16)
    wk = w_in[:, hk:2 * hk].astype(BF16)
    wv = w_in[:, 2 * hk:2 * hk + hv].astype(BF16)
    wr = w_in[:, 2 * hk + hv:2 * hk + 2 * hv].astype(BF16)
    wg = w_in[:, 2 * hk + 2 * hv:].astype(BF16)
    consts = [norm_g.reshape(1, d), wq, wk, wv, wr, wg, w_gate_up.astype(BF16), b_gate.reshape(1, hk)]
    row = lambda n: pl.BlockSpec((1, tm, n), lambda b, t: (b, t, 0))
    return pl.pallas_call(
        functools.partial(_gla_proj_kernel, q_scale=float((hk // GLA_HEADS) ** -0.5)),
        grid=(bsz, L // tm),
        in_specs=[row(d), _mod_spec(mod, tm, d, 0), _mod_spec(mod, tm, d, 1)] + [_const_spec(a) for a in consts],
        out_specs=[row(hk), row(hk), row(hv), row(hv), row(hk)],
        out_shape=[
            jax.ShapeDtypeStruct((bsz, L, hk), F32),
            jax.ShapeDtypeStruct((bsz, L, hk), F32),
            jax.ShapeDtypeStruct((bsz, L, hv), BF16),
            jax.ShapeDtypeStruct((bsz, L, hv), F32),
            jax.ShapeDtypeStruct((bsz, L, hk), F32),
        ],
        compiler_params=_params("arbitrary", "arbitrary"),
        name="gla_proj",
    )(x, mod, mod, *consts)


def _gla_kernel(q_ref, k_ref, v_ref, g_ref, s0_ref, o_ref, sf_ref, st_scr, *, chunk, n_chunks):
    t = pl.program_id(2)

    @pl.when(t == 0)
    def _():
        st_scr[...] = s0_ref[0, 0].T

    row = lax.broadcasted_iota(jnp.int32, (chunk, chunk), 0)
    col = lax.broadcasted_iota(jnp.int32, (chunk, chunk), 1)
    causal = row >= col
    tri = jnp.where(causal, 1.0, 0.0).astype(BF16)
    for c in range(n_chunks):
        sl = slice(c * chunk, (c + 1) * chunk)
        g_hi, g_lo = _split_bf16(g_ref[0, sl, :])
        b = _dot(tri, g_hi) + _dot(tri, g_lo)
        qt = (q_ref[0, sl, :] * jnp.exp(b)).astype(BF16)
        kt = (k_ref[0, sl, :] * jnp.exp(-b)).astype(BF16)
        v = v_ref[0, sl, :]
        attn = jnp.where(causal, _dot_nt(qt, kt), 0.0).astype(BF16)
        st = st_scr[...]
        o_ref[0, sl, :] = _dot(attn, v) + _dot_nt(qt, st.astype(BF16))
        st_scr[...] = (st + _dot_tn(v, kt)) * jnp.exp(b[chunk - 1:chunk, :])

    @pl.when(t == pl.num_programs(2) - 1)
    def _():
        sf_ref[0, 0] = st_scr[...].T


def _gla_seq(q, k, v, gk, s0, tl):
    bsz, L, hk = q.shape
    hv = v.shape[-1]
    dk, dv = hk // GLA_HEADS, hv // GLA_HEADS
    import math
    chunk = math.gcd(tl, GLA_CHUNK)
    kspec = pl.BlockSpec((1, tl, dk), lambda b, h, t: (b, t, h))
    vspec = pl.BlockSpec((1, tl, dv), lambda b, h, t: (b, t, h))
    sspec = pl.BlockSpec((1, 1, dk, dv), lambda b, h, t: (b, h, 0, 0))
    return pl.pallas_call(
        functools.partial(_gla_kernel, chunk=chunk, n_chunks=tl // chunk),
        grid=(bsz, GLA_HEADS, L // tl),
        in_specs=[kspec, kspec, vspec, kspec, sspec],
        out_specs=[vspec, sspec],
        out_shape=[jax.ShapeDtypeStruct((bsz, L, hv), F32),
                   jax.ShapeDtypeStruct((bsz, GLA_HEADS, dk, dv), F32)],
        scratch_shapes=[pltpu.VMEM((dv, dk), F32)],
        compiler_params=_params("arbitrary", "arbitrary", "arbitrary"),
        name="gla_seq",
    )(q, k, v, gk, s0)


def _gla_step_kernel(q_ref, k_ref, v_ref, g_ref, s_ref, o_ref, sn_ref, *, nb, dk, dv):
    eye = (lax.broadcasted_iota(jnp.int32, (dk, dk), 0) == lax.broadcasted_iota(jnp.int32, (dk, dk), 1))

    def column(row_vec):
        return jnp.sum(jnp.where(eye, row_vec, 0.0), axis=1, keepdims=True)

    for j in range(nb):
        for h in range(GLA_HEADS):
            ks = slice(h * dk, (h + 1) * dk)
            vs = slice(h * dv, (h + 1) * dv)
            decay = column(jnp.exp(g_ref[j:j + 1, ks]))
            kc = column(k_ref[j:j + 1, ks])
            qc = column(q_ref[j:j + 1, ks])
            s_new = s_ref[j, h] * decay + kc * v_ref[j:j + 1, vs].astype(F32)
            sn_ref[j, h] = s_new
            o_ref[j:j + 1, vs] = jnp.sum(qc * s_new, axis=0, keepdims=True)


def _gla_step(q, k, v, gk, s0):
    n, hk = q.shape
    hv = v.shape[-1]
    dk, dv = hk // GLA_HEADS, hv // GLA_HEADS
    nb = 8
    row = lambda w: pl.BlockSpec((nb, w), lambda i: (i, 0))
    sspec = pl.BlockSpec((nb, GLA_HEADS, dk, dv), lambda i: (i, 0, 0, 0))
    return pl.pallas_call(
        functools.partial(_gla_step_kernel, nb=nb, dk=dk, dv=dv),
        grid=(n // nb,),
        in_specs=[row(hk), row(hk), row(hv), row(hk), sspec],
        out_specs=[row(hv), sspec],
        out_shape=[jax.ShapeDtypeStruct((n, hv), F32),
                   jax.ShapeDtypeStruct((n, GLA_HEADS, dk, dv), F32)],
        compiler_params=_params("arbitrary"),
        name="gla_step",
    )(q, k, v, gk, s0)


def _residual_ffn_norm(x, m, ga, fnorm, scf, shf, x1_ref, hf_ref):
    x1 = x + ga * m
    x1_ref[0] = x1
    hf_ref[0] = (_rms(x1, fnorm) * (1.0 + scf) + shf).astype(BF16)


def _gla_out_kernel(o_ref, r_ref, x_ref, ga_ref, shf_ref, scf_ref, onorm_ref, wout_ref, fnorm_ref,
                    x1_ref, hf_ref, *, dv):
    parts = []
    for h in range(GLA_HEADS):
        vs = slice(h * dv, (h + 1) * dv)
        y = _rms(o_ref[0, :, vs], onorm_ref[...]) * _silu(r_ref[0, :, vs])
        parts.append(y.astype(BF16))
    m = _dot(jnp.concatenate(parts, axis=-1), wout_ref[...])
    _residual_ffn_norm(x_ref[0], m, ga_ref[0], fnorm_ref[...], scf_ref[0], shf_ref[0], x1_ref, hf_ref)


def _mla_out_kernel(o_ref, x_ref, ga_ref, shf_ref, scf_ref, wout_ref, fnorm_ref, x1_ref, hf_ref):
    m = _dot(o_ref[0], wout_ref[...])
    _residual_ffn_norm(x_ref[0], m, ga_ref[0], fnorm_ref[...], scf_ref[0], shf_ref[0], x1_ref, hf_ref)


def _mixer_out(kernel_fn, acts, x, mod, consts, tm, name):
    bsz, L, d = x.shape
    row = lambda a: pl.BlockSpec((1, tm, a.shape[-1]), lambda b, t: (b, t, 0))
    return pl.pallas_call(
        kernel_fn,
        grid=(bsz, L // tm),
        in_specs=[row(a) for a in acts] + [row(x), _mod_spec(mod, tm, d, 2), _mod_spec(mod, tm, d, 3),
                                           _mod_spec(mod, tm, d, 4)] + [_const_spec(a) for a in consts],
        out_specs=[row(x), row(x)],
        out_shape=[jax.ShapeDtypeStruct((bsz, L, d), F32), jax.ShapeDtypeStruct((bsz, L, d), BF16)],
        compiler_params=_params("arbitrary", "arbitrary"),
        name=name,
    )(*acts, x, mod, mod, mod, *consts)


def _moe_kernel(hf_ref, x1_ref, gf_ref, wr_ref, br_ref, wg_ref, wu_ref, wd_ref, y_ref, acc_scr, gate_scr):
    e = pl.program_id(2)
    hf = hf_ref[0]
    lane = lax.broadcasted_iota(jnp.int32, gate_scr.shape, 1)

    @pl.when(e == 0)
    def _():
        logits = _dot(hf, wr_ref[...]) + br_ref[...]
        big = jnp.int32(1 << 20)
        is_grp = lane < MOE_GROUPS
        lg = jnp.where(is_grp, logits, -jnp.inf)
        mg = jnp.max(lg, axis=-1, keepdims=True)
        g_val = 1.0 / jnp.sum(jnp.exp(lg - mg), axis=-1, keepdims=True)
        g_idx = jnp.min(jnp.where(is_grp & (lg == mg), lane, big), axis=-1, keepdims=True)
        lo = MOE_GROUPS + g_idx * MOE_EXP_PER_GROUP
        in_grp = (lane >= lo) & (lane < lo + MOE_EXP_PER_GROUP)
        le = jnp.where(in_grp, logits, -jnp.inf)
        me = jnp.max(le, axis=-1, keepdims=True)
        pe = jnp.exp(le - me)
        p = pe / jnp.sum(pe, axis=-1, keepdims=True)
        p1 = jnp.max(p, axis=-1, keepdims=True)
        i1 = jnp.min(jnp.where(in_grp & (p == p1), lane, big), axis=-1, keepdims=True)
        rest = jnp.where(in_grp & (lane != i1), p, -1.0)
        p2 = jnp.max(rest, axis=-1, keepdims=True)
        i2 = jnp.min(jnp.where(rest == p2, lane, big), axis=-1, keepdims=True)
        norm = g_val / (p1 + p2)
        gate_scr[...] = jnp.where(lane == i1, p1 * norm, 0.0) + jnp.where(lane == i2, p2 * norm, 0.0)
        acc_scr[...] = jnp.zeros_like(acc_scr)

    gate = jnp.sum(jnp.where(lane == e + MOE_GROUPS, gate_scr[...], 0.0), axis=-1, keepdims=True)
    hid = _silu(_dot(hf, wg_ref[0])) * _dot(hf, wu_ref[0]) * gate
    acc_scr[...] += _dot(hid.astype(BF16), wd_ref[0])

    @pl.when(e == pl.num_programs(2) - 1)
    def _():
        y_ref[0] = x1_ref[0] + gf_ref[0] * acc_scr[...]


def _moe(hf, x1, mod, w_route, b_route, w_gate, w_up, w_down, tm):
    bsz, L, d = x1.shape
    n_exp, _, d_ff = w_gate.shape
    row = pl.BlockSpec((1, tm, d), lambda b, t, e: (b, t, 0))
    if mod.shape[1] == 1:
        gspec = pl.BlockSpec((1, 1, d), lambda b, t, e: (b, 0, 5))
    else:
        gspec = pl.BlockSpec((1, tm, d), lambda b, t, e: (b, t, 5))
    return pl.pallas_call(
        _moe_kernel,
        grid=(bsz, L // tm, n_exp),
        in_specs=[row, row, gspec,
                  pl.BlockSpec(w_route.shape, lambda b, t, e: (0, 0)),
                  pl.BlockSpec(b_route.shape, lambda b, t, e: (0, 0)),
                  pl.BlockSpec((1, d, d_ff), lambda b, t, e: (e, 0, 0)),
                  pl.BlockSpec((1, d, d_ff), lambda b, t, e: (e, 0, 0)),
                  pl.BlockSpec((1, d_ff, d), lambda b, t, e: (e, 0, 0))],
        out_specs=row,
        out_shape=jax.ShapeDtypeStruct((bsz, L, d), F32),
        scratch_shapes=[pltpu.VMEM((tm, d), F32), pltpu.VMEM((tm, LANES), F32)],
        compiler_params=_params("arbitrary", "arbitrary", "arbitrary"),
        name="moe",
    )(hf, x1, mod, w_route, b_route, w_gate, w_up, w_down)


def _kv_kernel(x_ref, innorm_ref, wc_ref, wr_ref, wrot_ref, latnorm_ref, cos_ref, sin_ref, ckv_ref, kr_ref):
    hn = _rms(x_ref[0], innorm_ref[...]).astype(BF16)
    ckv_ref[0] = _rms(_dot(hn, wc_ref[...]), latnorm_ref[...])
    kr_ref[0] = _dot(hn, wr_ref[...]) * cos_ref[...] + _dot(hn, wrot_ref[...]) * sin_ref[...]


def _rope_spec(table, tm):
    if table.shape[0] == 1:
        return pl.BlockSpec((1, LANES), lambda b, t: (0, 0))
    return pl.BlockSpec((tm, LANES), lambda b, t: (t, 0))


def _shared_kv(x, consts, cos, sin, tm):
    bsz, L, d = x.shape
    rank = consts[1].shape[1]
    row = lambda n: pl.BlockSpec((1, tm, n), lambda b, t: (b, t, 0))
    return pl.pallas_call(
        _kv_kernel,
        grid=(bsz, L // tm),
        in_specs=[row(d)] + [_const_spec(a) for a in consts] + [_rope_spec(cos, tm), _rope_spec(sin, tm)],
        out_specs=[row(rank), row(LANES)],
        out_shape=[jax.ShapeDtypeStruct((bsz, L, rank), F32), jax.ShapeDtypeStruct((bsz, L, LANES), F32)],
        compiler_params=_params("arbitrary", "arbitrary"),
        name="shared_kv",
    )(x, *consts, cos, sin)


def _mla_q_kernel(x_ref, sh_ref, sc_ref, mnorm_ref, wdq_ref, qlat_ref, wuq_ref, wuqrot_ref, qnorm_ref,
                  cos_ref, sin_ref, q_ref, *, scale):
    hm = _rms(x_ref[0], mnorm_ref[...]) * (1.0 + sc_ref[0]) + sh_ref[0]
    cq = _rms(_dot(hm.astype(BF16), wdq_ref[...]), qlat_ref[...]).astype(BF16)
    qp = _dot(cq, wuq_ref[...])
    qr = _dot(cq, wuqrot_ref[...])
    for h in range(MLA_HEADS):
        hs = slice(h * LANES, (h + 1) * LANES)
        a = qp[:, hs] * cos_ref[...] + qr[:, hs] * sin_ref[...]
        inv = lax.rsqrt(jnp.sum(a * a, axis=-1, keepdims=True) * (1.0 / MLA_QK) + NORM_EPS)
        q_ref[0, :, hs] = (a * inv * (qnorm_ref[...] * scale)).astype(q_ref.dtype)


def _mla_q(x, mod, consts, cos, sin, tm, out_dtype):
    bsz, L, d = x.shape
    n = MLA_HEADS * LANES
    row = lambda w: pl.BlockSpec((1, tm, w), lambda b, t: (b, t, 0))
    return pl.pallas_call(
        functools.partial(_mla_q_kernel, scale=float(MLA_QK ** -0.5)),
        grid=(bsz, L // tm),
        in_specs=[row(d), _mod_spec(mod, tm, d, 0), _mod_spec(mod, tm, d, 1)] + [_const_spec(a) for a in consts]
        + [_rope_spec(cos, tm), _rope_spec(sin, tm)],
        out_specs=row(n),
        out_shape=jax.ShapeDtypeStruct((bsz, L, n), out_dtype),
        compiler_params=_params("arbitrary", "arbitrary"),
        name="mla_q",
    )(x, mod, mod, *consts, cos, sin)


def _kv_up_kernel(ckv_ref, kr_ref, wuk_ref, wuv_ref, knorm_ref, k_ref, v_ref):
    c = ckv_ref[0].astype(BF16)
    kn = _dot(c, wuk_ref[...])
    v_ref[0] = _dot(c, wuv_ref[...]).astype(v_ref.dtype)
    for h in range(MLA_HEADS):
        hs = slice(h * LANES, (h + 1) * LANES)
        a = kn[:, hs] + kr_ref[0]
        inv = lax.rsqrt(jnp.sum(a * a, axis=-1, keepdims=True) * (1.0 / MLA_QK) + NORM_EPS)
        k_ref[0, :, hs] = (a * inv * knorm_ref[...]).astype(k_ref.dtype)


def _kv_up(ckv, kr_slot, consts, tm, out_dtype):
    bsz, L, rank = ckv.shape
    n = MLA_HEADS * LANES
    row = lambda w: pl.BlockSpec((1, tm, w), lambda b, t: (b, t, 0))
    return pl.pallas_call(
        _kv_up_kernel,
        grid=(bsz, L // tm),
        in_specs=[row(rank), row(LANES)] + [_const_spec(a) for a in consts],
        out_specs=[row(n), row(n)],
        out_shape=[jax.ShapeDtypeStruct((bsz, L, n), out_dtype), jax.ShapeDtypeStruct((bsz, L, n), out_dtype)],
        compiler_params=_params("arbitrary", "arbitrary"),
        name="kv_up",
    )(ckv, kr_slot, *consts)


def _flash_kernel(q_ref, k_ref, v_ref, o_ref, *, tq):
    qi = pl.program_id(2)
    q = q_ref[0]

    def update(carry, j, masked):
        m, l, acc = carry
        off = pl.multiple_of(j * tq, tq)
        s = _dot_nt(q, k_ref[0, pl.ds(off, tq), :])
        if masked:
            row = lax.broadcasted_iota(jnp.int32, (tq, tq), 0)
            col = lax.broadcasted_iota(jnp.int32, (tq, tq), 1)
            s = jnp.where(col <= row, s, -jnp.inf)
        m_new = jnp.maximum(m, jnp.max(s, axis=-1, keepdims=True))
        p = jnp.exp(s - m_new)
        corr = jnp.exp(m - m_new)
        l = l * corr + jnp.sum(p, axis=-1, keepdims=True)
        acc = acc * corr + _dot(p.astype(BF16), v_ref[0, pl.ds(off, tq), :])
        return m_new, l, acc

    init = (jnp.full((tq, 1), NEG_BIG, F32), jnp.zeros((tq, 1), F32), jnp.zeros((tq, LANES), F32))
    carry = lax.fori_loop(0, qi, lambda j, c: update(c, j, False), init)
    m, l, acc = update(carry, qi, True)
    o_ref[0] = (acc / l).astype(o_ref.dtype)


def _flash(q, k, v, tq):
    bsz, S, n = q.shape
    qspec = pl.BlockSpec((1, tq, LANES), lambda b, h, i: (b, i, h))
    kspec = pl.BlockSpec((1, S, LANES), lambda b, h, i: (b, 0, h))
    return pl.pallas_call(
        functools.partial(_flash_kernel, tq=tq),
        grid=(bsz, MLA_HEADS, S // tq),
        in_specs=[qspec, kspec, kspec],
        out_specs=qspec,
        out_shape=jax.ShapeDtypeStruct((bsz, S, n), BF16),
        compiler_params=_params("arbitrary", "arbitrary", "arbitrary"),
        name="flash",
    )(q, k, v)


def _paged_kernel(pt_ref, *refs, n_pages):
    del pt_ref
    c_refs = refs[:n_pages]
    kr_refs = refs[n_pages:2 * n_pages]
    (q_ref, knew_ref, cnew_ref, knorm_ref, wukt_ref, wukt_slot_ref, wuv_slot_ref,
     o_ref, m_scr, l_scr, acc_scr, qabs_scr) = refs[2 * n_pages:]
    j = pl.program_id(1)
    nh = MLA_HEADS
    head = lax.broadcasted_iota(jnp.int32, (nh, 1), 0)
    qg = q_ref[0] * knorm_ref[...]

    @pl.when(j == 0)
    def _():
        m_scr[...] = jnp.full_like(m_scr, NEG_BIG)
        l_scr[...] = jnp.zeros_like(l_scr)
        acc_scr[...] = jnp.zeros_like(acc_scr)
        qb = qg.astype(BF16)
        qabs = jnp.zeros(qabs_scr.shape, F32)
        for h in range(nh):
            qabs = qabs + jnp.where(head == h, _dot(qb, wukt_slot_ref[h]), 0.0)
        qabs_scr[...] = qabs

    qabs = qabs_scr[...].astype(BF16)
    q_rope = qg[:, MLA_NOPE:MLA_QK].astype(BF16)
    ones = jnp.ones((nh, MLA_ROPE), BF16)
    m, l, acc = m_scr[...], l_scr[...], acc_scr[...]
    for p in range(n_pages):
        c = c_refs[p][0].astype(BF16)
        kr = kr_refs[p][0]
        knt = _dot_nt(wukt_ref[...], c)
        sq = knt * knt
        ss = jnp.sum(sq.reshape(nh, MLA_NOPE, sq.shape[-1]), axis=1)
        kr2_hi, kr2_lo = _split_bf16(kr * kr)
        ss = ss + _dot_nt(ones, kr2_hi) + _dot_nt(ones, kr2_lo)
        num = _dot_nt(qabs, c) + _dot_nt(q_rope, kr.astype(BF16))
        s = num * lax.rsqrt(ss * (1.0 / MLA_QK) + NORM_EPS)
        m_new = jnp.maximum(m, jnp.max(s, axis=-1, keepdims=True))
        pr = jnp.exp(s - m_new)
        corr = jnp.exp(m - m_new)
        l = l * corr + jnp.sum(pr, axis=-1, keepdims=True)
        acc = acc * corr + _dot(pr.astype(BF16), c)
        m = m_new
    m_scr[...], l_scr[...], acc_scr[...] = m, l, acc

    @pl.when(j == pl.num_programs(1) - 1)
    def _():
        s_new = jnp.sum(q_ref[0] * knew_ref[0], axis=-1, keepdims=True)
        m_new = jnp.maximum(m, s_new)
        corr = jnp.exp(m - m_new)
        p_new = jnp.exp(s_new - m_new)
        l_fin = l * corr + p_new
        o_lat = ((acc * corr + p_new * cnew_ref[0]) / l_fin).astype(BF16)
        out = jnp.zeros((nh, LANES), F32)
        for h in range(nh):
            out = out + jnp.where(head == h, _dot(o_lat, wuv_slot_ref[h]), 0.0)
        o_ref[0] = out


def _paged_attention(q, k_new, c_new, cache_ckv, cache_krope, page_table, consts, n_pages):
    nb, nh, _ = q.shape
    _, page, rank = cache_ckv.shape
    rope = cache_krope.shape[-1]
    steps = page_table.shape[1] // n_pages

    def page_spec(width, p):
        return pl.BlockSpec((1, page, width), lambda b, j, pt: (pt[b, j * n_pages + p], 0, 0))

    per_b = lambda a: pl.BlockSpec((1,) + a.shape[1:], lambda b, j, pt: (b,) + (0,) * (a.ndim - 1))
    const = lambda a: pl.BlockSpec(a.shape, lambda b, j, pt: (0,) * a.ndim)
    grid_spec = pltpu.PrefetchScalarGridSpec(
        num_scalar_prefetch=1,
        grid=(nb, steps),
        in_specs=[page_spec(rank, p) for p in range(n_pages)] + [page_spec(rope, p) for p in range(n_pages)]
        + [per_b(q), per_b(k_new), per_b(c_new)] + [const(a) for a in consts],
        out_specs=pl.BlockSpec((1, nh, LANES), lambda b, j, pt: (b, 0, 0)),
        scratch_shapes=[pltpu.VMEM((nh, 1), F32), pltpu.VMEM((nh, 1), F32), pltpu.VMEM((nh, rank), F32),
                        pltpu.VMEM((nh, rank), F32)],
    )
    return pl.pallas_call(
        functools.partial(_paged_kernel, n_pages=n_pages),
        grid_spec=grid_spec,
        out_shape=jax.ShapeDtypeStruct((nb, nh, LANES), F32),
        compiler_params=_params("arbitrary", "arbitrary"),
        name="paged_attn",
    )(page_table, *([cache_ckv] * n_pages), *([cache_krope] * n_pages), q, k_new, c_new, *consts)


def _rotate_half_cols(w):
    half = w.shape[-1] // 2
    return jnp.concatenate([-w[..., half:], w[..., :half]], axis=-1)


def _slot(nope, rope):
    ref = nope if nope is not None else rope
    lead = ref.shape[:-1]
    z = lambda n: jnp.zeros(lead + (n,), ref.dtype)
    a = nope if nope is not None else z(MLA_NOPE)
    b = rope if rope is not None else z(MLA_ROPE)
    return jnp.concatenate([a, b, z(LANES - MLA_QK)], axis=-1)


def _rope_tables(pos):
    inv = ROPE_THETA ** (-jnp.arange(0, MLA_ROPE, 2, dtype=F32) / MLA_ROPE)
    ang = pos.astype(F32)[:, None] * inv[None, :]
    ang = jnp.concatenate([ang, ang], axis=-1)
    n = pos.shape[0]
    cos = _slot(jnp.ones((n, MLA_NOPE), F32), jnp.cos(ang))
    sin = _slot(jnp.zeros((n, MLA_NOPE), F32), jnp.sin(ang))
    return cos, sin


def _prepare(w):
    d = w['mix_norm'].shape[-1]
    p = {}
    p['route_w'] = [jnp.pad(jnp.concatenate([w['moe_w_group'][l], w['moe_w_router'][l]], axis=-1),
                            ((0, 0), (0, LANES - MOE_GROUPS - MOE_EXPERTS))).astype(BF16) for l in range(2)]
    p['route_b'] = [jnp.pad(jnp.concatenate([w['moe_b_group'][l], w['moe_b_router'][l]], axis=-1),
                            (0, LANES - MOE_GROUPS - MOE_EXPERTS)).reshape(1, LANES) for l in range(2)]
    p['moe_gate'] = w['moe_w_gate'].astype(BF16)
    p['moe_up'] = w['moe_w_up'].astype(BF16)
    p['moe_down'] = w['moe_w_down'].astype(BF16)
    p['gla_out'] = [w['gla_out_norm'][0].reshape(1, -1), w['gla_w_out'][0].astype(BF16),
                    w['ffn_norm'][0].reshape(1, d)]
    rank = w['kv_lat_norm'].shape[0]
    w_r = w['kv_w_dkv'][:, rank:]
    p['kv'] = [w['kv_in_norm'].reshape(1, d), w['kv_w_dkv'][:, :rank].astype(BF16),
               _slot(None, w_r).astype(BF16), _slot(None, _rotate_half_cols(w_r)).astype(BF16),
               w['kv_lat_norm'].reshape(1, rank)]
    q_rank = w['mla_w_dq'].shape[-1]
    wuq = w['mla_w_uq'][0].reshape(q_rank, MLA_HEADS, MLA_QK)
    wuq_n, wuq_r = wuq[..., :MLA_NOPE], wuq[..., MLA_NOPE:]
    p['mla_q'] = [w['mix_norm'][1].reshape(1, d), w['mla_w_dq'][0].astype(BF16),
                  w['mla_q_lat_norm'][0].reshape(1, q_rank),
                  _slot(wuq_n, wuq_r).reshape(q_rank, -1).astype(BF16),
                  _slot(None, _rotate_half_cols(wuq_r)).reshape(q_rank, -1).astype(BF16),
                  jnp.pad(w['mla_q_norm'][0], (0, LANES - MLA_QK)).reshape(1, LANES)]
    knorm = jnp.pad(w['kv_k_norm'], (0, LANES - MLA_QK)).reshape(1, LANES)
    wuk, wuv = w['kv_w_uk'], w['kv_w_uv']
    p['kv_up'] = [_slot(wuk, None).reshape(rank, -1).astype(BF16),
                  jnp.pad(wuv, ((0, 0), (0, 0), (0, LANES - MLA_V))).reshape(rank, -1).astype(BF16), knorm]
    wukt = wuk.transpose(1, 2, 0)
    p['paged'] = [knorm, wukt.reshape(MLA_HEADS * MLA_NOPE, rank).astype(BF16),
                  jnp.pad(wukt, ((0, 0), (0, LANES - MLA_NOPE), (0, 0))).astype(BF16),
                  jnp.pad(wuv.transpose(1, 0, 2), ((0, 0), (0, 0), (0, LANES - MLA_V))).astype(BF16)]
    wo = w['mla_w_out'][0].reshape(MLA_HEADS, MLA_V, d)
    p['mla_out'] = [jnp.pad(wo, ((0, 0), (0, LANES - MLA_V), (0, 0))).reshape(MLA_HEADS * LANES, d).astype(BF16),
                    w['ffn_norm'][1].reshape(1, d)]
    return p


def _moe_layer(l, hf, x1, mod, p, tm):
    return _moe(hf, x1, mod, p['route_w'][l], p['route_b'][l], p['moe_gate'][l], p['moe_up'][l],
                p['moe_down'][l], tm)


def _trunk(x, mods, pos, gla_s0, paged, w, p, tm):
    cos, sin = _rope_tables(pos)
    q, k, v, r, gk = _gla_proj(x, mods[0], w['mix_norm'][0], w['gla_w_in'][0], w['gla_w_gate_up'][0],
                               w['gla_b_gate'][0], tm)
    if paged is None:
        o, s_fin = _gla_seq(q, k, v, gk, gla_s0, min(256, x.shape[1]))
    else:
        o, s_fin = _gla_step(q[0], k[0], v[0], gk[0], gla_s0)
        o = o[None]
    x1, hf = _mixer_out(functools.partial(_gla_out_kernel, dv=v.shape[-1] // GLA_HEADS), [o, r], x, mods[0],
                        p['gla_out'], tm, "gla_out")
    x2 = _moe_layer(0, hf, x1, mods[0], p, tm)
    ckv, kr_slot = _shared_kv(x2, p['kv'], cos, sin, tm)
    if paged is None:
        qh = _mla_q(x2, mods[1], p['mla_q'], cos, sin, tm, BF16)
        kh, vh = _kv_up(ckv, kr_slot, p['kv_up'], tm, BF16)
        o = _flash(qh, kh, vh, min(512, x.shape[1]))
    else:
        n = x.shape[1]
        qh = _mla_q(x2, mods[1], p['mla_q'], cos, sin, tm, F32)
        kh, _ = _kv_up(ckv, kr_slot, p['kv_up'], tm, F32)
        o = _paged_attention(qh.reshape(n, MLA_HEADS, LANES), kh.reshape(n, MLA_HEADS, LANES),
                             ckv.reshape(n, 1, -1), paged[0], paged[1], paged[2], p['paged'], 8)
        o = o.reshape(1, n, MLA_HEADS * LANES).astype(BF16)
    x3, hf = _mixer_out(_mla_out_kernel, [o], x2, mods[1], p['mla_out'], tm, "mla_out")
    y = _moe_layer(1, hf, x3, mods[1], p, tm)
    return y, s_fin, ckv, kr_slot[..., MLA_NOPE:MLA_QK]


def _forward(x_prompt, x_sample, c_prompt, c_sample, state_gla, cache_ckv, cache_krope, page_table, w, past_len):
    bsz, seq, d = x_prompt.shape
    nd, dl, _ = x_sample.shape
    assert dl == 1, "decode path handles one new token per sequence"
    p = _prepare(w)
    mod = _ada_mod(jnp.concatenate([c_prompt, c_sample], axis=0), w['ada_w'], w['ada_b'])
    mods_p = [mod[l, :bsz].reshape(bsz, 1, -1) for l in range(2)]
    mods_s = [mod[l, bsz:].reshape(1, nd, -1) for l in range(2)]
    s0_p = jnp.zeros((bsz,) + state_gla.shape[2:], state_gla.dtype)
    y_p, s_p, ckv_p, kr_p = _trunk(x_prompt, mods_p, jnp.arange(seq), s0_p, None, w, p, min(512, seq))
    y_s, s_s, ckv_s, kr_s = _trunk(x_sample.reshape(1, nd, d), mods_s, jnp.full((1,), past_len), state_gla[0],
                                   (cache_ckv, cache_krope, page_table), w, p, nd)
    return (y_p, y_s.reshape(nd, 1, d), s_p[None], s_s[None], ckv_p, kr_p,
            ckv_s.reshape(nd, 1, -1), kr_s.reshape(nd, 1, -1))


def kernel(x_prompt, x_sample, c_prompt, c_sample, state_gla, cache_ckv, cache_krope, page_table, ada_w, ada_b, mix_norm, ffn_norm, gla_w_in, gla_w_gate_up, gla_b_gate, gla_out_norm, gla_w_out, mla_w_dq, mla_q_lat_norm, mla_w_uq, mla_q_norm, mla_w_out, kv_in_norm, kv_w_dkv, kv_lat_norm, kv_w_uk, kv_w_uv, kv_k_norm, moe_w_group, moe_b_group, moe_w_router, moe_b_router, moe_w_gate, moe_w_up, moe_w_down):
    w = dict(ada_w=ada_w, ada_b=ada_b, mix_norm=mix_norm, ffn_norm=ffn_norm,
             gla_w_in=gla_w_in, gla_w_gate_up=gla_w_gate_up, gla_b_gate=gla_b_gate,
             gla_out_norm=gla_out_norm, gla_w_out=gla_w_out,
             mla_w_dq=mla_w_dq, mla_q_lat_norm=mla_q_lat_norm, mla_w_uq=mla_w_uq,
             mla_q_norm=mla_q_norm, mla_w_out=mla_w_out,
             kv_in_norm=kv_in_norm, kv_w_dkv=kv_w_dkv, kv_lat_norm=kv_lat_norm,
             kv_w_uk=kv_w_uk, kv_w_uv=kv_w_uv, kv_k_norm=kv_k_norm,
             moe_w_group=moe_w_group, moe_b_group=moe_b_group, moe_w_router=moe_w_router,
             moe_b_router=moe_b_router, moe_w_gate=moe_w_gate, moe_w_up=moe_w_up, moe_w_down=moe_w_down)
    past_len = page_table.shape[1] * cache_ckv.shape[1]
    return _forward(x_prompt, x_sample, c_prompt, c_sample, state_gla, cache_ckv, cache_krope, page_table, w,
                    past_len)
```

```python
import functools
import math

import jax
import jax.numpy as jnp
from jax import lax
from jax.experimental import pallas as pl
from jax.experimental.pallas import tpu as pltpu

F32 = jnp.float32
BF16 = jnp.bfloat16

NORM_EPS = 1e-6
GLA_HEADS = 4
GLA_GATE_NORM = 16.0
GLA_CHUNK = 64
MLA_HEADS = 8
MLA_NOPE = 64
MLA_ROPE = 32
MLA_QK = MLA_NOPE + MLA_ROPE
MLA_V = 64
ROPE_THETA = 10000.0
MOE_GROUPS = 4
MOE_EXP_PER_GROUP = 4
MOE_EXPERTS = MOE_GROUPS * MOE_EXP_PER_GROUP
LANES = 128
NEG_BIG = -1e30
FLASH_T = 256
FLASH_KEYS = 1024
VMEM_LIMIT = 52 * 1024 * 1024


def _params(*sem):
    return pltpu.CompilerParams(dimension_semantics=sem, vmem_limit_bytes=VMEM_LIMIT)


def _dot(a, b):
    return jnp.dot(a, b, preferred_element_type=F32)


def _dot_nt(a, b):
    return lax.dot_general(a, b, (((1,), (1,)), ((), ())), preferred_element_type=F32)


def _dot_tn(a, b):
    return lax.dot_general(a, b, (((0,), (0,)), ((), ())), preferred_element_type=F32)


def _rms(x, g):
    return x * lax.rsqrt(jnp.mean(x * x, axis=-1, keepdims=True) + NORM_EPS) * g


def _silu(x):
    return x / (1.0 + jnp.exp(-x))


def _split_bf16(x):
    hi = x.astype(BF16)
    lo = (x - hi.astype(F32)).astype(BF16)
    return hi, lo


def _ada_kernel(c_ref, w_ref, b_ref, o_ref):
    a = _silu(c_ref[...]).astype(BF16)
    o_ref[0] = _dot(a, w_ref[0].astype(BF16)) + b_ref[0]


def _ada_mod(c, ada_w, ada_b):
    depth, d, n = ada_w.shape
    r = c.shape[0]
    tn = 1536
    return pl.pallas_call(
        _ada_kernel,
        grid=(depth, n // tn),
        in_specs=[
            pl.BlockSpec((r, d), lambda l, j: (0, 0)),
            pl.BlockSpec((1, d, tn), lambda l, j: (l, 0, j)),
            pl.BlockSpec((1, 1, tn), lambda l, j: (l, 0, j)),
        ],
        out_specs=pl.BlockSpec((1, r, tn), lambda l, j: (l, 0, j)),
        out_shape=jax.ShapeDtypeStruct((depth, r, n), F32),
        compiler_params=_params("arbitrary", "arbitrary"),
        name="ada_mod",
    )(c, ada_w, ada_b.reshape(depth, 1, n))


def _mod_spec(mod, tm, d, chunk):
    if mod.shape[1] == 1:
        return pl.BlockSpec((1, 1, d), lambda b, t: (b, 0, chunk))
    return pl.BlockSpec((1, tm, d), lambda b, t: (b, t, chunk))


def _const_spec(arr):
    nd = arr.ndim
    return pl.BlockSpec(arr.shape, lambda b, t: (0,) * nd)


def _gla_proj_kernel(x_ref, sh_ref, sc_ref, g_ref, wq_ref, wk_ref, wv_ref, wr_ref, wg_ref, wgu_ref, bg_ref,
                     q_ref, k_ref, v_ref, r_ref, gk_ref, *, q_scale):
    hm = _rms(x_ref[0], g_ref[...]) * (1.0 + sc_ref[0]) + sh_ref[0]
    hb = hm.astype(BF16)
    q_ref[0] = _dot(hb, wq_ref[...]) * q_scale
    k_ref[0] = _dot(hb, wk_ref[...])
    v_ref[0] = _dot(hb, wv_ref[...]).astype(BF16)
    r_ref[0] = _dot(hb, wr_ref[...])
    g_low = _dot(hb, wg_ref[...])
    z = _dot(g_low.astype(BF16), wgu_ref[...]) + bg_ref[...]
    log_sig = jnp.minimum(z, 0.0) - jnp.log(1.0 + jnp.exp(-jnp.abs(z)))
    gk_ref[0] = log_sig * (1.0 / GLA_GATE_NORM)


def _gla_proj(x, mod, norm_g, w_in, w_gate_up, b_gate, tm):
    bsz, L, d = x.shape
    hk = w_gate_up.shape[1]
    hv = (w_in.shape[1] - 2 * hk - w_gate_up.shape[0]) // 2
    wq = w_in[:, :hk].astype(BF16)
    wk = w_in[:, hk:2 * hk].astype(BF16)
    wv = w_in[:, 2 * hk:2 * hk + hv].astype(BF16)
    wr = w_in[:, 2 * hk + hv:2 * hk + 2 * hv].astype(BF16)
    wg = w_in[:, 2 * hk + 2 * hv:].astype(BF16)
    consts = [norm_g.reshape(1, d), wq, wk, wv, wr, wg, w_gate_up.astype(BF16), b_gate.reshape(1, hk)]
    row = lambda n: pl.BlockSpec((1, tm, n), lambda b, t: (b, t, 0))
    return pl.pallas_call(
        functools.partial(_gla_proj_kernel, q_scale=float((hk // GLA_HEADS) ** -0.5)),
        grid=(bsz, L // tm),
        in_specs=[row(d), _mod_spec(mod, tm, d, 0), _mod_spec(mod, tm, d, 1)] + [_const_spec(a) for a in consts],
        out_specs=[row(hk), row(hk), row(hv), row(hv), row(hk)],
        out_shape=[
            jax.ShapeDtypeStruct((bsz, L, hk), F32),
            jax.ShapeDtypeStruct((bsz, L, hk), F32),
            jax.ShapeDtypeStruct((bsz, L, hv), BF16),
            jax.ShapeDtypeStruct((bsz, L, hv), F32),
            jax.ShapeDtypeStruct((bsz, L, hk), F32),
        ],
        compiler_params=_params("arbitrary", "arbitrary"),
        name="gla_proj",
    )(x, mod, mod, *consts)


def _gla_kernel(q_ref, k_ref, v_ref, g_ref, s0_ref, o_ref, sf_ref, st_scr, *, chunk, n_chunks):
    t = pl.program_id(2)

    @pl.when(t == 0)
    def _():
        st_scr[...] = s0_ref[0, 0].T

    row = lax.broadcasted_iota(jnp.int32, (chunk, chunk), 0)
    col = lax.broadcasted_iota(jnp.int32, (chunk, chunk), 1)
    causal = row >= col
    tri = jnp.where(causal, 1.0, 0.0).astype(BF16)
    for c in range(n_chunks):
        sl = slice(c * chunk, (c + 1) * chunk)
        g_hi, g_lo = _split_bf16(g_ref[0, sl, :])
        b = _dot(tri, g_hi) + _dot(tri, g_lo)
        qt = (q_ref[0, sl, :] * jnp.exp(b)).astype(BF16)
        kt = (k_ref[0, sl, :] * jnp.exp(-b)).astype(BF16)
        v = v_ref[0, sl, :]
        attn = jnp.where(causal, _dot_nt(qt, kt), 0.0).astype(BF16)
        st = st_scr[...]
        o_ref[0, sl, :] = _dot(attn, v) + _dot_nt(qt, st.astype(BF16))
        st_scr[...] = (st + _dot_tn(v, kt)) * jnp.exp(b[chunk - 1:chunk, :])

    @pl.when(t == pl.num_programs(2) - 1)
    def _():
        sf_ref[0, 0] = st_scr[...].T


def _gla_seq(q, k, v, gk, s0, tl):
    bsz, L, hk = q.shape
    hv = v.shape[-1]
    dk, dv = hk // GLA_HEADS, hv // GLA_HEADS
    chunk = math.gcd(tl, GLA_CHUNK)
    kspec = pl.BlockSpec((1, tl, dk), lambda b, h, t: (b, t, h))
    vspec = pl.BlockSpec((1, tl, dv), lambda b, h, t: (b, t, h))
    sspec = pl.BlockSpec((1, 1, dk, dv), lambda b, h, t: (b, h, 0, 0))
    return pl.pallas_call(
        functools.partial(_gla_kernel, chunk=chunk, n_chunks=tl // chunk),
        grid=(bsz, GLA_HEADS, L // tl),
        in_specs=[kspec, kspec, vspec, kspec, sspec],
        out_specs=[vspec, sspec],
        out_shape=[jax.ShapeDtypeStruct((bsz, L, hv), F32),
                   jax.ShapeDtypeStruct((bsz, GLA_HEADS, dk, dv), F32)],
        scratch_shapes=[pltpu.VMEM((dv, dk), F32)],
        compiler_params=_params("arbitrary", "arbitrary", "arbitrary"),
        name="gla_seq",
    )(q, k, v, gk, s0)


def _gla_step_kernel(q_ref, k_ref, v_ref, g_ref, s_ref, o_ref, sn_ref, *, nb, dk, dv):
    eye = (lax.broadcasted_iota(jnp.int32, (dk, dk), 0) == lax.broadcasted_iota(jnp.int32, (dk, dk), 1))

    def column(row_vec):
        return jnp.sum(jnp.where(eye, row_vec, 0.0), axis=1, keepdims=True)

    for j in range(nb):
        for h in range(GLA_HEADS):
            ks = slice(h * dk, (h + 1) * dk)
            vs = slice(h * dv, (h + 1) * dv)
            decay = column(jnp.exp(g_ref[j:j + 1, ks]))
            kc = column(k_ref[j:j + 1, ks])
            qc = column(q_ref[j:j + 1, ks])
            s_new = s_ref[j, h] * decay + kc * v_ref[j:j + 1, vs].astype(F32)
            sn_ref[j, h] = s_new
            o_ref[j:j + 1, vs] = jnp.sum(qc * s_new, axis=0, keepdims=True)


def _gla_step(q, k, v, gk, s0):
    n, hk = q.shape
    hv = v.shape[-1]
    dk, dv = hk // GLA_HEADS, hv // GLA_HEADS
    nb = 8
    row = lambda w: pl.BlockSpec((nb, w), lambda i: (i, 0))
    sspec = pl.BlockSpec((nb, GLA_HEADS, dk, dv), lambda i: (i, 0, 0, 0))
    return pl.pallas_call(
        functools.partial(_gla_step_kernel, nb=nb, dk=dk, dv=dv),
        grid=(n // nb,),
        in_specs=[row(hk), row(hk), row(hv), row(hk), sspec],
        out_specs=[row(hv), sspec],
        out_shape=[jax.ShapeDtypeStruct((n, hv), F32),
                   jax.ShapeDtypeStruct((n, GLA_HEADS, dk, dv), F32)],
        compiler_params=_params("arbitrary"),
        name="gla_step",
    )(q, k, v, gk, s0)


def _residual_ffn_norm(x, m, ga, fnorm, scf, shf, x1_ref, hf_ref):
    x1 = x + ga * m
    x1_ref[0] = x1
    hf_ref[0] = (_rms(x1, fnorm) * (1.0 + scf) + shf).astype(BF16)


def _gla_out_kernel(o_ref, r_ref, x_ref, ga_ref, shf_ref, scf_ref, onorm_ref, wout_ref, fnorm_ref,
                    x1_ref, hf_ref, *, dv):
    parts = []
    for h in range(GLA_HEADS):
        vs = slice(h * dv, (h + 1) * dv)
        y = _rms(o_ref[0, :, vs], onorm_ref[...]) * _silu(r_ref[0, :, vs])
        parts.append(y.astype(BF16))
    m = _dot(jnp.concatenate(parts, axis=-1), wout_ref[...])
    _residual_ffn_norm(x_ref[0], m, ga_ref[0], fnorm_ref[...], scf_ref[0], shf_ref[0], x1_ref, hf_ref)


def _mla_out_kernel(o_ref, x_ref, ga_ref, shf_ref, scf_ref, wout_ref, fnorm_ref, x1_ref, hf_ref):
    m = _dot(o_ref[0], wout_ref[...])
    _residual_ffn_norm(x_ref[0], m, ga_ref[0], fnorm_ref[...], scf_ref[0], shf_ref[0], x1_ref, hf_ref)


def _mixer_out(kernel_fn, acts, x, mod, consts, tm, name):
    bsz, L, d = x.shape
    row = lambda a: pl.BlockSpec((1, tm, a.shape[-1]), lambda b, t: (b, t, 0))
    return pl.pallas_call(
        kernel_fn,
        grid=(bsz, L // tm),
        in_specs=[row(a) for a in acts] + [row(x), _mod_spec(mod, tm, d, 2), _mod_spec(mod, tm, d, 3),
                                           _mod_spec(mod, tm, d, 4)] + [_const_spec(a) for a in consts],
        out_specs=[row(x), row(x)],
        out_shape=[jax.ShapeDtypeStruct((bsz, L, d), F32), jax.ShapeDtypeStruct((bsz, L, d), BF16)],
        compiler_params=_params("arbitrary", "arbitrary"),
        name=name,
    )(*acts, x, mod, mod, mod, *consts)


def _moe_kernel(hf_ref, x1_ref, gf_ref, wr_ref, br_ref, wg_ref, wu_ref, wd_ref, y_ref, acc_scr, gate_scr):
    e = pl.program_id(2)
    hf = hf_ref[0]
    lane = lax.broadcasted_iota(jnp.int32, gate_scr.shape, 1)

    @pl.when(e == 0)
    def _():
        logits = _dot(hf, wr_ref[...]) + br_ref[...]
        big = jnp.int32(1 << 20)
        is_grp = lane < MOE_GROUPS
        lg = jnp.where(is_grp, logits, -jnp.inf)
        mg = jnp.max(lg, axis=-1, keepdims=True)
        g_val = 1.0 / jnp.sum(jnp.exp(lg - mg), axis=-1, keepdims=True)
        g_idx = jnp.min(jnp.where(is_grp & (lg == mg), lane, big), axis=-1, keepdims=True)
        lo = MOE_GROUPS + g_idx * MOE_EXP_PER_GROUP
        in_grp = (lane >= lo) & (lane < lo + MOE_EXP_PER_GROUP)
        le = jnp.where(in_grp, logits, -jnp.inf)
        me = jnp.max(le, axis=-1, keepdims=True)
        pe = jnp.exp(le - me)
        p = pe / jnp.sum(pe, axis=-1, keepdims=True)
        p1 = jnp.max(p, axis=-1, keepdims=True)
        i1 = jnp.min(jnp.where(in_grp & (p == p1), lane, big), axis=-1, keepdims=True)
        rest = jnp.where(in_grp & (lane != i1), p, -1.0)
        p2 = jnp.max(rest, axis=-1, keepdims=True)
        i2 = jnp.min(jnp.where(rest == p2, lane, big), axis=-1, keepdims=True)
        norm = g_val / (p1 + p2)
        gate_scr[...] = jnp.where(lane == i1, p1 * norm, 0.0) + jnp.where(lane == i2, p2 * norm, 0.0)
        acc_scr[...] = jnp.zeros_like(acc_scr)

    gate = jnp.sum(jnp.where(lane == e + MOE_GROUPS, gate_scr[...], 0.0), axis=-1, keepdims=True)
    hid = _silu(_dot(hf, wg_ref[0])) * _dot(hf, wu_ref[0]) * gate
    acc_scr[...] += _dot(hid.astype(BF16), wd_ref[0])

    @pl.when(e == pl.num_programs(2) - 1)
    def _():
        y_ref[0] = x1_ref[0] + gf_ref[0] * acc_scr[...]


def _moe(hf, x1, mod, w_route, b_route, w_gate, w_up, w_down, tm):
    bsz, L, d = x1.shape
    n_exp, _, d_ff = w_gate.shape
    row = pl.BlockSpec((1, tm, d), lambda b, t, e: (b, t, 0))
    if mod.shape[1] == 1:
        gspec = pl.BlockSpec((1, 1, d), lambda b, t, e: (b, 0, 5))
    else:
        gspec = pl.BlockSpec((1, tm, d), lambda b, t, e: (b, t, 5))
    return pl.pallas_call(
        _moe_kernel,
        grid=(bsz, L // tm, n_exp),
        in_specs=[row, row, gspec,
                  pl.BlockSpec(w_route.shape, lambda b, t, e: (0, 0)),
                  pl.BlockSpec(b_route.shape, lambda b, t, e: (0, 0)),
                  pl.BlockSpec((1, d, d_ff), lambda b, t, e: (e, 0, 0)),
                  pl.BlockSpec((1, d, d_ff), lambda b, t, e: (e, 0, 0)),
                  pl.BlockSpec((1, d_ff, d), lambda b, t, e: (e, 0, 0))],
        out_specs=row,
        out_shape=jax.ShapeDtypeStruct((bsz, L, d), F32),
        scratch_shapes=[pltpu.VMEM((tm, d), F32), pltpu.VMEM((tm, LANES), F32)],
        compiler_params=_params("arbitrary", "arbitrary", "arbitrary"),
        name="moe",
    )(hf, x1, mod, w_route, b_route, w_gate, w_up, w_down)


def _kv_kernel(x_ref, innorm_ref, wc_ref, wr_ref, wrot_ref, latnorm_ref, cos_ref, sin_ref, ckv_ref, kr_ref):
    hn = _rms(x_ref[0], innorm_ref[...]).astype(BF16)
    ckv_ref[0] = _rms(_dot(hn, wc_ref[...]), latnorm_ref[...])
    kr_ref[0] = _dot(hn, wr_ref[...]) * cos_ref[...] + _dot(hn, wrot_ref[...]) * sin_ref[...]


def _rope_spec(table, tm):
    if table.shape[0] == 1:
        return pl.BlockSpec((1, LANES), lambda b, t: (0, 0))
    return pl.BlockSpec((tm, LANES), lambda b, t: (t, 0))


def _shared_kv(x, consts, cos, sin, tm):
    bsz, L, d = x.shape
    rank = consts[1].shape[1]
    row = lambda n: pl.BlockSpec((1, tm, n), lambda b, t: (b, t, 0))
    return pl.pallas_call(
        _kv_kernel,
        grid=(bsz, L // tm),
        in_specs=[row(d)] + [_const_spec(a) for a in consts] + [_rope_spec(cos, tm), _rope_spec(sin, tm)],
        out_specs=[row(rank), row(LANES)],
        out_shape=[jax.ShapeDtypeStruct((bsz, L, rank), F32), jax.ShapeDtypeStruct((bsz, L, LANES), F32)],
        compiler_params=_params("arbitrary", "arbitrary"),
        name="shared_kv",
    )(x, *consts, cos, sin)


def _mla_q_kernel(x_ref, sh_ref, sc_ref, mnorm_ref, wdq_ref, qlat_ref, wuq_ref, wuqrot_ref, qnorm_ref,
                  cos_ref, sin_ref, q_ref, *, scale):
    hm = _rms(x_ref[0], mnorm_ref[...]) * (1.0 + sc_ref[0]) + sh_ref[0]
    cq = _rms(_dot(hm.astype(BF16), wdq_ref[...]), qlat_ref[...]).astype(BF16)
    qp = _dot(cq, wuq_ref[...])
    qr = _dot(cq, wuqrot_ref[...])
    for h in range(MLA_HEADS):
        hs = slice(h * LANES, (h + 1) * LANES)
        a = qp[:, hs] * cos_ref[...] + qr[:, hs] * sin_ref[...]
        inv = lax.rsqrt(jnp.sum(a * a, axis=-1, keepdims=True) * (1.0 / MLA_QK) + NORM_EPS)
        q_ref[0, :, hs] = (a * inv * (qnorm_ref[...] * scale)).astype(q_ref.dtype)


def _mla_q(x, mod, consts, cos, sin, tm, out_dtype):
    bsz, L, d = x.shape
    n = MLA_HEADS * LANES
    row = lambda w: pl.BlockSpec((1, tm, w), lambda b, t: (b, t, 0))
    return pl.pallas_call(
        functools.partial(_mla_q_kernel, scale=float(MLA_QK ** -0.5)),
        grid=(bsz, L // tm),
        in_specs=[row(d), _mod_spec(mod, tm, d, 0), _mod_spec(mod, tm, d, 1)] + [_const_spec(a) for a in consts]
        + [_rope_spec(cos, tm), _rope_spec(sin, tm)],
        out_specs=row(n),
        out_shape=jax.ShapeDtypeStruct((bsz, L, n), out_dtype),
        compiler_params=_params("arbitrary", "arbitrary"),
        name="mla_q",
    )(x, mod, mod, *consts, cos, sin)


def _kv_up_kernel(ckv_ref, kr_ref, wuk_ref, knorm_ref, *rest, vt_tile):
    if vt_tile:
        wuvt_ref, k_ref, vt_ref = rest
    else:
        (k_ref,) = rest
    c = ckv_ref[0].astype(BF16)
    kn = _dot(c, wuk_ref[...])
    for h in range(MLA_HEADS):
        hs = slice(h * LANES, (h + 1) * LANES)
        a = kn[:, hs] + kr_ref[0]
        inv = lax.rsqrt(jnp.sum(a * a, axis=-1, keepdims=True) * (1.0 / MLA_QK) + NORM_EPS)
        k_ref[0, :, hs] = (a * inv * knorm_ref[...]).astype(k_ref.dtype)
    if vt_tile:
        for s in range(c.shape[0] // vt_tile):
            vt = _dot_nt(wuvt_ref[...], c[s * vt_tile:(s + 1) * vt_tile])
            vt_ref[0, s] = vt.astype(vt_ref.dtype)


def _kv_up(ckv, kr_slot, consts, tm, out_dtype, vt_tile=0):
    bsz, L, rank = ckv.shape
    n = MLA_HEADS * LANES
    row = lambda w: pl.BlockSpec((1, tm, w), lambda b, t: (b, t, 0))
    out_specs = [row(n)]
    out_shape = [jax.ShapeDtypeStruct((bsz, L, n), out_dtype)]
    if vt_tile:
        nv = consts[-1].shape[0]
        out_specs.append(pl.BlockSpec((1, tm // vt_tile, nv, vt_tile), lambda b, t: (b, t, 0, 0)))
        out_shape.append(jax.ShapeDtypeStruct((bsz, L // vt_tile, nv, vt_tile), out_dtype))
    return pl.pallas_call(
        functools.partial(_kv_up_kernel, vt_tile=vt_tile),
        grid=(bsz, L // tm),
        in_specs=[row(rank), row(LANES)] + [_const_spec(a) for a in consts],
        out_specs=out_specs,
        out_shape=out_shape,
        compiler_params=_params("arbitrary", "arbitrary"),
        name="kv_up",
    )(ckv, kr_slot, *consts)


def _flash_kernel(q_ref, k_ref, vt_ref, o_ref, *, tq, tk):
    qi = pl.program_id(2)
    sub = tk // tq
    n_full = (qi * tq) // tk

    def update(carry, c, masked):
        new = []
        off = pl.multiple_of(c * tk, tk)
        scores = [_dot_nt(k_ref[0, pl.ds(off, tk), hh * LANES:(hh + 1) * LANES],
                          q_ref[0, :, hh * LANES:(hh + 1) * LANES]) for hh in range(len(carry))]
        for hh, (m, l, acc) in enumerate(carry):
            s = scores[hh]
            if masked:
                key = lax.broadcasted_iota(jnp.int32, (tk, tq), 0) + c * tk
                query = lax.broadcasted_iota(jnp.int32, (tk, tq), 1) + qi * tq
                s = jnp.where(key <= query, s, -jnp.inf)
            m_new = jnp.maximum(m, jnp.max(s, axis=0, keepdims=True))
            p = jnp.exp(s - m_new)
            corr = jnp.exp(m - m_new)
            l = l * corr + jnp.sum(p, axis=0, keepdims=True)
            pb = p.astype(BF16)
            acc = acc * corr
            for t in range(sub):
                vt = vt_ref[0, c * sub + t, hh * MLA_V:(hh + 1) * MLA_V, :]
                acc = acc + _dot(vt, pb[t * tq:(t + 1) * tq])
            new.append((m_new, l, acc))
        return tuple(new)

    one = (jnp.full((1, tq), NEG_BIG, F32), jnp.zeros((1, tq), F32), jnp.zeros((MLA_V, tq), F32))
    carry = lax.fori_loop(0, n_full, lambda c, cr: update(cr, c, False), (one, one))
    carry = update(carry, n_full, True)
    out_t = jnp.concatenate([acc / l for (_, l, acc) in carry], axis=0)
    o_ref[0] = out_t.T.astype(o_ref.dtype)


def _flash(q, k, vt, tq):
    bsz, S, _ = q.shape
    pairs = MLA_HEADS // 2
    qspec = pl.BlockSpec((1, tq, 2 * LANES), lambda b, h, i: (b, i, h))
    kspec = pl.BlockSpec((1, S, 2 * LANES), lambda b, h, i: (b, 0, h))
    vspec = pl.BlockSpec((1, S // tq, 2 * MLA_V, tq), lambda b, h, i: (b, 0, h, 0))
    return pl.pallas_call(
        functools.partial(_flash_kernel, tq=tq, tk=min(FLASH_KEYS, S)),
        grid=(bsz, pairs, S // tq),
        in_specs=[qspec, kspec, vspec],
        out_specs=pl.BlockSpec((1, tq, 2 * MLA_V), lambda b, h, i: (b, i, h)),
        out_shape=jax.ShapeDtypeStruct((bsz, S, MLA_HEADS * MLA_V), BF16),
        compiler_params=_params("arbitrary", "arbitrary", "arbitrary"),
        name="flash",
    )(q, k, vt)


def _paged_kernel(pt_ref, *refs, n_pages):
    del pt_ref
    c_refs = refs[:n_pages]
    kr_refs = refs[n_pages:2 * n_pages]
    (q_ref, knew_ref, cnew_ref, knorm_ref, wukt_ref, wukt_slot_ref, wuv_slot_ref,
     o_ref, m_scr, l_scr, acc_scr, lhs_scr, c_scr) = refs[2 * n_pages:]
    j = pl.program_id(1)
    nh = MLA_HEADS
    n_kn = nh * MLA_NOPE
    page = c_refs[0].shape[1]
    tile = 2 * page
    head = lax.broadcasted_iota(jnp.int32, (nh, 1), 0)
    qg = q_ref[0] * knorm_ref[...]

    @pl.when(j == 0)
    def _():
        m_scr[...] = jnp.full_like(m_scr, NEG_BIG)
        l_scr[...] = jnp.zeros_like(l_scr)
        acc_scr[...] = jnp.zeros_like(acc_scr)
        qb = qg.astype(BF16)
        qabs = jnp.zeros((nh, lhs_scr.shape[1]), F32)
        for h in range(nh):
            qabs = qabs + jnp.where(head == h, _dot(qb, wukt_slot_ref[h]), 0.0)
        lhs_scr[:n_kn, :] = wukt_ref[...]
        lhs_scr[n_kn:, :] = jnp.concatenate([qabs, jnp.zeros_like(qabs)], axis=0).astype(BF16)

    q_rope = qg[:, MLA_NOPE:MLA_QK].astype(BF16)
    scores = []
    for i in range(n_pages // 2):
        c2 = jnp.concatenate([c_refs[2 * i][0].astype(BF16), c_refs[2 * i + 1][0].astype(BF16)], axis=0)
        c_scr[i * tile:(i + 1) * tile, :] = c2
        both = _dot_nt(lhs_scr[...], c2)
        knt = both[:n_kn]
        ss = jnp.sum((knt * knt).reshape(nh, MLA_NOPE, tile), axis=1)
        kr2 = jnp.concatenate([kr_refs[2 * i][0], kr_refs[2 * i + 1][0]], axis=1)
        ss = ss + jnp.sum(kr2 * kr2, axis=0, keepdims=True)
        num = both[n_kn:n_kn + nh] + _dot(q_rope, kr2.astype(BF16))
        scores.append(num * lax.rsqrt(ss * (1.0 / MLA_QK) + NORM_EPS))
    s = jnp.concatenate(scores, axis=1)
    m = m_scr[...]
    m_new = jnp.maximum(m, jnp.max(s, axis=-1, keepdims=True))
    pr = jnp.exp(s - m_new)
    corr = jnp.exp(m - m_new)
    l = l_scr[...] * corr + jnp.sum(pr, axis=-1, keepdims=True)
    pb = pr.astype(BF16)
    pv = _dot(pb[:, :tile], c_scr[:tile, :])
    for i in range(1, n_pages // 2):
        pv = pv + _dot(pb[:, i * tile:(i + 1) * tile], c_scr[i * tile:(i + 1) * tile, :])
    acc = acc_scr[...] * corr + pv
    m_scr[...] = m_new
    l_scr[...] = l
    acc_scr[...] = acc

    @pl.when(j == pl.num_programs(1) - 1)
    def _():
        s_new = jnp.sum(q_ref[0] * knew_ref[0], axis=-1, keepdims=True)
        m_fin = jnp.maximum(m_new, s_new)
        corr_fin = jnp.exp(m_new - m_fin)
        p_new = jnp.exp(s_new - m_fin)
        l_fin = l * corr_fin + p_new
        o_lat = ((acc * corr_fin + p_new * cnew_ref[0]) / l_fin).astype(BF16)
        out = jnp.zeros((nh, LANES), F32)
        for h in range(nh):
            out = out + jnp.where(head == h, _dot(o_lat, wuv_slot_ref[h]), 0.0)
        o_ref[0] = out


def _paged_attention(q, k_new, c_new, cache_ckv, cache_krope_t, page_table, consts, n_pages):
    nb, nh, _ = q.shape
    _, page, rank = cache_ckv.shape
    rope = cache_krope_t.shape[1]
    steps = page_table.shape[1] // n_pages

    def page_spec(shape, p):
        return pl.BlockSpec((1,) + shape, lambda b, j, pt: (pt[b, j * n_pages + p], 0, 0))

    per_b = lambda a: pl.BlockSpec((1,) + a.shape[1:], lambda b, j, pt: (b,) + (0,) * (a.ndim - 1))
    const = lambda a: pl.BlockSpec(a.shape, lambda b, j, pt: (0,) * a.ndim)
    grid_spec = pltpu.PrefetchScalarGridSpec(
        num_scalar_prefetch=1,
        grid=(nb, steps),
        in_specs=[page_spec((page, rank), p) for p in range(n_pages)]
        + [page_spec((rope, page), p) for p in range(n_pages)]
        + [per_b(q), per_b(k_new), per_b(c_new)] + [const(a) for a in consts],
        out_specs=pl.BlockSpec((1, nh, LANES), lambda b, j, pt: (b, 0, 0)),
        scratch_shapes=[pltpu.VMEM((nh, 1), F32), pltpu.VMEM((nh, 1), F32), pltpu.VMEM((nh, rank), F32),
                        pltpu.VMEM((nh * MLA_NOPE + 2 * nh, rank), BF16),
                        pltpu.VMEM((n_pages * page, rank), BF16)],
    )
    return pl.pallas_call(
        functools.partial(_paged_kernel, n_pages=n_pages),
        grid_spec=grid_spec,
        out_shape=jax.ShapeDtypeStruct((nb, nh, LANES), F32),
        compiler_params=_params("arbitrary", "arbitrary"),
        name="paged_attn",
    )(page_table, *([cache_ckv] * n_pages), *([cache_krope_t] * n_pages), q, k_new, c_new, *consts)


def _rotate_half_cols(w):
    half = w.shape[-1] // 2
    return jnp.concatenate([-w[..., half:], w[..., :half]], axis=-1)


def _slot(nope, rope):
    ref = nope if nope is not None else rope
    lead = ref.shape[:-1]
    z = lambda n: jnp.zeros(lead + (n,), ref.dtype)
    a = nope if nope is not None else z(MLA_NOPE)
    b = rope if rope is not None else z(MLA_ROPE)
    return jnp.concatenate([a, b, z(LANES - MLA_QK)], axis=-1)


def _rope_tables(pos):
    inv = ROPE_THETA ** (-jnp.arange(0, MLA_ROPE, 2, dtype=F32) / MLA_ROPE)
    ang = pos.astype(F32)[:, None] * inv[None, :]
    ang = jnp.concatenate([ang, ang], axis=-1)
    n = pos.shape[0]
    cos = _slot(jnp.ones((n, MLA_NOPE), F32), jnp.cos(ang))
    sin = _slot(jnp.zeros((n, MLA_NOPE), F32), jnp.sin(ang))
    return cos, sin


def _prepare(w):
    d = w['mix_norm'].shape[-1]
    p = {}
    p['route_w'] = [jnp.pad(jnp.concatenate([w['moe_w_group'][l], w['moe_w_router'][l]], axis=-1),
                            ((0, 0), (0, LANES - MOE_GROUPS - MOE_EXPERTS))).astype(BF16) for l in range(2)]
    p['route_b'] = [jnp.pad(jnp.concatenate([w['moe_b_group'][l], w['moe_b_router'][l]], axis=-1),
                            (0, LANES - MOE_GROUPS - MOE_EXPERTS)).reshape(1, LANES) for l in range(2)]
    p['moe_gate'] = w['moe_w_gate'].astype(BF16)
    p['moe_up'] = w['moe_w_up'].astype(BF16)
    p['moe_down'] = w['moe_w_down'].astype(BF16)
    p['gla_out'] = [w['gla_out_norm'][0].reshape(1, -1), w['gla_w_out'][0].astype(BF16),
                    w['ffn_norm'][0].reshape(1, d)]
    rank = w['kv_lat_norm'].shape[0]
    w_r = w['kv_w_dkv'][:, rank:]
    p['kv'] = [w['kv_in_norm'].reshape(1, d), w['kv_w_dkv'][:, :rank].astype(BF16),
               _slot(None, w_r).astype(BF16), _slot(None, _rotate_half_cols(w_r)).astype(BF16),
               w['kv_lat_norm'].reshape(1, rank)]
    q_rank = w['mla_w_dq'].shape[-1]
    wuq = w['mla_w_uq'][0].reshape(q_rank, MLA_HEADS, MLA_QK)
    wuq_n, wuq_r = wuq[..., :MLA_NOPE], wuq[..., MLA_NOPE:]
    p['mla_q'] = [w['mix_norm'][1].reshape(1, d), w['mla_w_dq'][0].astype(BF16),
                  w['mla_q_lat_norm'][0].reshape(1, q_rank),
                  _slot(wuq_n, wuq_r).reshape(q_rank, -1).astype(BF16),
                  _slot(None, _rotate_half_cols(wuq_r)).reshape(q_rank, -1).astype(BF16),
                  jnp.pad(w['mla_q_norm'][0], (0, LANES - MLA_QK)).reshape(1, LANES)]
    knorm = jnp.pad(w['kv_k_norm'], (0, LANES - MLA_QK)).reshape(1, LANES)
    wuk, wuv = w['kv_w_uk'], w['kv_w_uv']
    p['kv_up'] = [_slot(wuk, None).reshape(rank, -1).astype(BF16), knorm]
    p['wuv_t'] = wuv.reshape(rank, -1).T.astype(BF16)
    wukt = wuk.transpose(1, 2, 0)
    p['paged'] = [knorm, wukt.reshape(MLA_HEADS * MLA_NOPE, rank).astype(BF16),
                  jnp.pad(wukt, ((0, 0), (0, LANES - MLA_NOPE), (0, 0))).astype(BF16),
                  jnp.pad(wuv.transpose(1, 0, 2), ((0, 0), (0, 0), (0, LANES - MLA_V))).astype(BF16)]
    wo = w['mla_w_out'][0].reshape(MLA_HEADS, MLA_V, d)
    p['mla_out_slot'] = [jnp.pad(wo, ((0, 0), (0, LANES - MLA_V), (0, 0))).reshape(MLA_HEADS * LANES, d)
                         .astype(BF16), w['ffn_norm'][1].reshape(1, d)]
    p['mla_out'] = [w['mla_w_out'][0].astype(BF16), w['ffn_norm'][1].reshape(1, d)]
    return p


def _moe_layer(l, hf, x1, mod, p, tm):
    return _moe(hf, x1, mod, p['route_w'][l], p['route_b'][l], p['moe_gate'][l], p['moe_up'][l],
                p['moe_down'][l], tm)


def _trunk(x, mods, pos, gla_s0, paged, w, p, tm):
    cos, sin = _rope_tables(pos)
    q, k, v, r, gk = _gla_proj(x, mods[0], w['mix_norm'][0], w['gla_w_in'][0], w['gla_w_gate_up'][0],
                               w['gla_b_gate'][0], tm)
    if paged is None:
        o, s_fin = _gla_seq(q, k, v, gk, gla_s0, min(256, x.shape[1]))
    else:
        o, s_fin = _gla_step(q[0], k[0], v[0], gk[0], gla_s0)
        o = o[None]
    x1, hf = _mixer_out(functools.partial(_gla_out_kernel, dv=v.shape[-1] // GLA_HEADS), [o, r], x, mods[0],
                        p['gla_out'], tm, "gla_out")
    x2 = _moe_layer(0, hf, x1, mods[0], p, tm)
    ckv, kr_slot = _shared_kv(x2, p['kv'], cos, sin, tm)
    if paged is None:
        qh = _mla_q(x2, mods[1], p['mla_q'], cos, sin, tm, BF16)
        tq = min(FLASH_T, x.shape[1])
        kh, vt = _kv_up(ckv, kr_slot, p['kv_up'] + [p['wuv_t']], tm, BF16, vt_tile=tq)
        o = _flash(qh, kh, vt, tq)
        out_consts = p['mla_out']
    else:
        n = x.shape[1]
        qh = _mla_q(x2, mods[1], p['mla_q'], cos, sin, tm, F32)
        (kh,) = _kv_up(ckv, kr_slot, p['kv_up'], tm, F32)
        o = _paged_attention(qh.reshape(n, MLA_HEADS, LANES), kh.reshape(n, MLA_HEADS, LANES),
                             ckv.reshape(n, 1, -1), paged[0], jnp.swapaxes(paged[1], 1, 2), paged[2], p['paged'],
                             min(32, paged[2].shape[1]))
        o = o.reshape(1, n, MLA_HEADS * LANES).astype(BF16)
        out_consts = p['mla_out_slot']
    x3, hf = _mixer_out(_mla_out_kernel, [o], x2, mods[1], out_consts, tm, "mla_out")
    y = _moe_layer(1, hf, x3, mods[1], p, tm)
    return y, s_fin, ckv, kr_slot[..., MLA_NOPE:MLA_QK]


def _forward(x_prompt, x_sample, c_prompt, c_sample, state_gla, cache_ckv, cache_krope, page_table, w, past_len):
    bsz, seq, d = x_prompt.shape
    nd, dl, _ = x_sample.shape
    assert dl == 1, "decode path handles one new token per sequence"
    p = _prepare(w)
    mod = _ada_mod(jnp.concatenate([c_prompt, c_sample], axis=0), w['ada_w'], w['ada_b'])
    mods_p = [mod[l, :bsz].reshape(bsz, 1, -1) for l in range(2)]
    mods_s = [mod[l, bsz:].reshape(1, nd, -1) for l in range(2)]
    s0_p = jnp.zeros((bsz,) + state_gla.shape[2:], state_gla.dtype)
    y_p, s_p, ckv_p, kr_p = _trunk(x_prompt, mods_p, jnp.arange(seq), s0_p, None, w, p, min(512, seq))
    y_s, s_s, ckv_s, kr_s = _trunk(x_sample.reshape(1, nd, d), mods_s, jnp.full((1,), past_len), state_gla[0],
                                   (cache_ckv, cache_krope, page_table), w, p, nd)
    return (y_p, y_s.reshape(nd, 1, d), s_p[None], s_s[None], ckv_p, kr_p,
            ckv_s.reshape(nd, 1, -1), kr_s.reshape(nd, 1, -1))


def kernel(x_prompt, x_sample, c_prompt, c_sample, state_gla, cache_ckv, cache_krope, page_table, ada_w, ada_b, mix_norm, ffn_norm, gla_w_in, gla_w_gate_up, gla_b_gate, gla_out_norm, gla_w_out, mla_w_dq, mla_q_lat_norm, mla_w_uq, mla_q_norm, mla_w_out, kv_in_norm, kv_w_dkv, kv_lat_norm, kv_w_uk, kv_w_uv, kv_k_norm, moe_w_group, moe_b_group, moe_w_router, moe_b_router, moe_w_gate, moe_w_up, moe_w_down):
    w = dict(ada_w=ada_w, ada_b=ada_b, mix_norm=mix_norm, ffn_norm=ffn_norm,
             gla_w_in=gla_w_in, gla_w_gate_up=gla_w_gate_up, gla_b_gate=gla_b_gate,
             gla_out_norm=gla_out_norm, gla_w_out=gla_w_out,
             mla_w_dq=mla_w_dq, mla_q_lat_norm=mla_q_lat_norm, mla_w_uq=mla_w_uq,
             mla_q_norm=mla_q_norm, mla_w_out=mla_w_out,
             kv_in_norm=kv_in_norm, kv_w_dkv=kv_w_dkv, kv_lat_norm=kv_lat_norm,
             kv_w_uk=kv_w_uk, kv_w_uv=kv_w_uv, kv_k_norm=kv_k_norm,
             moe_w_group=moe_w_group, moe_b_group=moe_b_group, moe_w_router=moe_w_router,
             moe_b_router=moe_b_router, moe_w_gate=moe_w_gate, moe_w_up=moe_w_up, moe_w_down=moe_w_down)
    past_len = page_table.shape[1] * cache_ckv.shape[1]
    return _forward(x_prompt, x_sample, c_prompt, c_sample, state_gla, cache_ckv, cache_krope, page_table, w,
                    past_len)
```

```python
import functools
import math

import jax
import jax.numpy as jnp
from jax import lax
from jax.experimental import pallas as pl
from jax.experimental.pallas import tpu as pltpu

F32 = jnp.float32
BF16 = jnp.bfloat16

NORM_EPS = 1e-6
GLA_HEADS = 4
GLA_GATE_NORM = 16.0
GLA_CHUNK = 64
MLA_HEADS = 8
MLA_NOPE = 64
MLA_ROPE = 32
MLA_QK = MLA_NOPE + MLA_ROPE
MLA_V = 64
ROPE_THETA = 10000.0
MOE_GROUPS = 4
MOE_EXP_PER_GROUP = 4
MOE_EXPERTS = MOE_GROUPS * MOE_EXP_PER_GROUP
LANES = 128
NEG_BIG = -1e30
FLASH_T = 256
FLASH_KEYS = 1024
MOE_ROWS = 1024
VMEM_LIMIT = 52 * 1024 * 1024


def _params(*sem):
    return pltpu.CompilerParams(dimension_semantics=sem, vmem_limit_bytes=VMEM_LIMIT)


def _dot(a, b):
    return jnp.dot(a, b, preferred_element_type=F32)


def _dot_nt(a, b):
    return lax.dot_general(a, b, (((1,), (1,)), ((), ())), preferred_element_type=F32)


def _dot_tn(a, b):
    return lax.dot_general(a, b, (((0,), (0,)), ((), ())), preferred_element_type=F32)


def _rms(x, g):
    return x * lax.rsqrt(jnp.mean(x * x, axis=-1, keepdims=True) + NORM_EPS) * g


def _silu(x):
    return x / (1.0 + jnp.exp(-x))


def _split_bf16(x):
    hi = x.astype(BF16)
    lo = (x - hi.astype(F32)).astype(BF16)
    return hi, lo


def _ada_kernel(c_ref, w_ref, b_ref, o_ref):
    a = _silu(c_ref[...]).astype(BF16)
    o_ref[0] = _dot(a, w_ref[0].astype(BF16)) + b_ref[0]


def _ada_mod(c, ada_w, ada_b):
    depth, d, n = ada_w.shape
    r = c.shape[0]
    tn = 1536
    return pl.pallas_call(
        _ada_kernel,
        grid=(depth, n // tn),
        in_specs=[
            pl.BlockSpec((r, d), lambda l, j: (0, 0)),
            pl.BlockSpec((1, d, tn), lambda l, j: (l, 0, j)),
            pl.BlockSpec((1, 1, tn), lambda l, j: (l, 0, j)),
        ],
        out_specs=pl.BlockSpec((1, r, tn), lambda l, j: (l, 0, j)),
        out_shape=jax.ShapeDtypeStruct((depth, r, n), F32),
        compiler_params=_params("arbitrary", "arbitrary"),
        name="ada_mod",
    )(c, ada_w, ada_b.reshape(depth, 1, n))


def _mod_spec(mod, tm, d, chunk):
    if mod.shape[1] == 1:
        return pl.BlockSpec((1, 1, d), lambda b, t: (b, 0, chunk))
    return pl.BlockSpec((1, tm, d), lambda b, t: (b, t, chunk))


def _const_spec(arr):
    nd = arr.ndim
    return pl.BlockSpec(arr.shape, lambda b, t: (0,) * nd)


def _gla_proj_kernel(x_ref, sh_ref, sc_ref, g_ref, wq_ref, wk_ref, wv_ref, wr_ref, wg_ref, wgu_ref, bg_ref,
                     q_ref, k_ref, v_ref, r_ref, gk_ref, *, q_scale):
    hm = _rms(x_ref[0], g_ref[...]) * (1.0 + sc_ref[0]) + sh_ref[0]
    hb = hm.astype(BF16)
    q_ref[0] = _dot(hb, wq_ref[...]) * q_scale
    k_ref[0] = _dot(hb, wk_ref[...])
    v_ref[0] = _dot(hb, wv_ref[...]).astype(BF16)
    r_ref[0] = _dot(hb, wr_ref[...])
    g_low = _dot(hb, wg_ref[...])
    z = _dot(g_low.astype(BF16), wgu_ref[...]) + bg_ref[...]
    log_sig = jnp.minimum(z, 0.0) - jnp.log(1.0 + jnp.exp(-jnp.abs(z)))
    gk_ref[0] = log_sig * (1.0 / GLA_GATE_NORM)


def _gla_proj_consts(norm_g, w_in, w_gate_up, b_gate):
    hk = w_gate_up.shape[1]
    hv = (w_in.shape[1] - 2 * hk - w_gate_up.shape[0]) // 2
    cuts = [0, hk, 2 * hk, 2 * hk + hv, 2 * hk + 2 * hv, w_in.shape[1]]
    pieces = [w_in[:, a:b].astype(BF16) for a, b in zip(cuts[:-1], cuts[1:])]
    return [norm_g.reshape(1, -1)] + pieces + [w_gate_up.astype(BF16), b_gate.reshape(1, hk)]


def _gla_proj(x, mod, norm_g, w_in, w_gate_up, b_gate, tm):
    bsz, L, d = x.shape
    consts = _gla_proj_consts(norm_g, w_in, w_gate_up, b_gate)
    hk, hv = consts[1].shape[1], consts[3].shape[1]
    row = lambda n: pl.BlockSpec((1, tm, n), lambda b, t: (b, t, 0))
    return pl.pallas_call(
        functools.partial(_gla_proj_kernel, q_scale=float((hk // GLA_HEADS) ** -0.5)),
        grid=(bsz, L // tm),
        in_specs=[row(d), _mod_spec(mod, tm, d, 0), _mod_spec(mod, tm, d, 1)] + [_const_spec(a) for a in consts],
        out_specs=[row(hk), row(hk), row(hv), row(hv), row(hk)],
        out_shape=[
            jax.ShapeDtypeStruct((bsz, L, hk), F32),
            jax.ShapeDtypeStruct((bsz, L, hk), F32),
            jax.ShapeDtypeStruct((bsz, L, hv), BF16),
            jax.ShapeDtypeStruct((bsz, L, hv), F32),
            jax.ShapeDtypeStruct((bsz, L, hk), F32),
        ],
        compiler_params=_params("arbitrary", "arbitrary"),
        name="gla_proj",
    )(x, mod, mod, *consts)


def _gla_layer_kernel(x_ref, sh_ref, sc_ref, ga_ref, shf_ref, scf_ref, s0_ref, mnorm_ref, wq_ref, wk_ref, wv_ref,
                      wr_ref, wg_ref, wgu_ref, bg_ref, onorm_ref, wout_ref, fnorm_ref,
                      x1_ref, hf_ref, sf_ref, st_scr, *, chunk, q_scale):
    t = pl.program_id(1)
    n_heads, dv, dk = st_scr.shape

    @pl.when(t == 0)
    def _():
        for h in range(n_heads):
            st_scr[h] = s0_ref[0, h].T

    x = x_ref[0]
    tm = x.shape[0]
    hb = (_rms(x, mnorm_ref[...]) * (1.0 + sc_ref[0]) + sh_ref[0]).astype(BF16)
    q = _dot(hb, wq_ref[...]) * q_scale
    k = _dot(hb, wk_ref[...])
    v = _dot(hb, wv_ref[...]).astype(BF16)
    r = _dot(hb, wr_ref[...])
    z = _dot(_dot(hb, wg_ref[...]).astype(BF16), wgu_ref[...]) + bg_ref[...]
    gk = (jnp.minimum(z, 0.0) - jnp.log(1.0 + jnp.exp(-jnp.abs(z)))) * (1.0 / GLA_GATE_NORM)
    tri = jnp.where(lax.broadcasted_iota(jnp.int32, (tm, tm), 0) >= lax.broadcasted_iota(jnp.int32, (tm, tm), 1),
                    1.0, 0.0).astype(BF16)
    g_hi, g_lo = _split_bf16(gk)
    cum = _dot(tri, g_hi) + _dot(tri, g_lo)
    causal = (lax.broadcasted_iota(jnp.int32, (chunk, chunk), 0)
              >= lax.broadcasted_iota(jnp.int32, (chunk, chunk), 1))
    parts = []
    for h in range(n_heads):
        ks = slice(h * dk, (h + 1) * dk)
        vs = slice(h * dv, (h + 1) * dv)
        st = st_scr[h]
        outs = []
        for c in range(tm // chunk):
            rows = slice(c * chunk, (c + 1) * chunk)
            b = cum[rows, ks]
            if c:
                b = b - cum[c * chunk - 1:c * chunk, ks]
            qt = (q[rows, ks] * jnp.exp(b)).astype(BF16)
            kt = (k[rows, ks] * jnp.exp(-b)).astype(BF16)
            vc = v[rows, vs]
            attn = jnp.where(causal, _dot_nt(qt, kt), 0.0).astype(BF16)
            outs.append(_dot(attn, vc) + _dot_nt(qt, st.astype(BF16)))
            st = (st + _dot_tn(vc, kt)) * jnp.exp(b[chunk - 1:chunk, :])
        st_scr[h] = st
        o_h = jnp.concatenate(outs, axis=0)
        parts.append((_rms(o_h, onorm_ref[...]) * _silu(r[:, vs])).astype(BF16))
    m = _dot(jnp.concatenate(parts, axis=-1), wout_ref[...])
    _residual_ffn_norm(x, m, ga_ref[0], fnorm_ref[...], scf_ref[0], shf_ref[0], x1_ref, hf_ref)

    @pl.when(t == pl.num_programs(1) - 1)
    def _():
        for h in range(n_heads):
            sf_ref[0, h] = st_scr[h].T


def _gla_layer(x, mod, s0, proj_consts, out_consts, tm):
    bsz, L, d = x.shape
    _, n_heads, dk, dv = s0.shape
    consts = proj_consts + out_consts
    row = lambda a: pl.BlockSpec((1, tm, a.shape[-1]), lambda b, t: (b, t, 0))
    sspec = pl.BlockSpec((1, n_heads, dk, dv), lambda b, t: (b, 0, 0, 0))
    return pl.pallas_call(
        functools.partial(_gla_layer_kernel, chunk=math.gcd(tm, GLA_CHUNK), q_scale=float(dk ** -0.5)),
        grid=(bsz, L // tm),
        in_specs=[row(x)] + [_mod_spec(mod, tm, d, c) for c in range(5)] + [sspec]
        + [_const_spec(a) for a in consts],
        out_specs=[row(x), row(x), sspec],
        out_shape=[jax.ShapeDtypeStruct((bsz, L, d), F32), jax.ShapeDtypeStruct((bsz, L, d), BF16),
                   jax.ShapeDtypeStruct(s0.shape, F32)],
        scratch_shapes=[pltpu.VMEM((n_heads, dv, dk), F32)],
        compiler_params=_params("arbitrary", "arbitrary"),
        name="gla_layer",
    )(x, mod, mod, mod, mod, mod, s0, *consts)


def _gla_step_kernel(q_ref, k_ref, v_ref, g_ref, s_ref, o_ref, sn_ref, *, nb, dk, dv):
    eye = (lax.broadcasted_iota(jnp.int32, (dk, dk), 0) == lax.broadcasted_iota(jnp.int32, (dk, dk), 1))

    def column(row_vec):
        return jnp.sum(jnp.where(eye, row_vec, 0.0), axis=1, keepdims=True)

    for j in range(nb):
        for h in range(GLA_HEADS):
            ks = slice(h * dk, (h + 1) * dk)
            vs = slice(h * dv, (h + 1) * dv)
            decay = column(jnp.exp(g_ref[j:j + 1, ks]))
            kc = column(k_ref[j:j + 1, ks])
            qc = column(q_ref[j:j + 1, ks])
            s_new = s_ref[j, h] * decay + kc * v_ref[j:j + 1, vs].astype(F32)
            sn_ref[j, h] = s_new
            o_ref[j:j + 1, vs] = jnp.sum(qc * s_new, axis=0, keepdims=True)


def _gla_step(q, k, v, gk, s0):
    n, hk = q.shape
    hv = v.shape[-1]
    dk, dv = hk // GLA_HEADS, hv // GLA_HEADS
    nb = 8
    row = lambda w: pl.BlockSpec((nb, w), lambda i: (i, 0))
    sspec = pl.BlockSpec((nb, GLA_HEADS, dk, dv), lambda i: (i, 0, 0, 0))
    return pl.pallas_call(
        functools.partial(_gla_step_kernel, nb=nb, dk=dk, dv=dv),
        grid=(n // nb,),
        in_specs=[row(hk), row(hk), row(hv), row(hk), sspec],
        out_specs=[row(hv), sspec],
        out_shape=[jax.ShapeDtypeStruct((n, hv), F32),
                   jax.ShapeDtypeStruct((n, GLA_HEADS, dk, dv), F32)],
        compiler_params=_params("arbitrary"),
        name="gla_step",
    )(q, k, v, gk, s0)


def _residual_ffn_norm(x, m, ga, fnorm, scf, shf, x1_ref, hf_ref):
    x1 = x + ga * m
    x1_ref[0] = x1
    hf_ref[0] = (_rms(x1, fnorm) * (1.0 + scf) + shf).astype(BF16)


def _gla_out_kernel(o_ref, r_ref, x_ref, ga_ref, shf_ref, scf_ref, onorm_ref, wout_ref, fnorm_ref,
                    x1_ref, hf_ref, *, dv):
    parts = []
    for h in range(GLA_HEADS):
        vs = slice(h * dv, (h + 1) * dv)
        y = _rms(o_ref[0, :, vs], onorm_ref[...]) * _silu(r_ref[0, :, vs])
        parts.append(y.astype(BF16))
    m = _dot(jnp.concatenate(parts, axis=-1), wout_ref[...])
    _residual_ffn_norm(x_ref[0], m, ga_ref[0], fnorm_ref[...], scf_ref[0], shf_ref[0], x1_ref, hf_ref)


def _mla_out_kernel(o_ref, x_ref, ga_ref, shf_ref, scf_ref, wout_ref, fnorm_ref, x1_ref, hf_ref):
    m = _dot(o_ref[0], wout_ref[...])
    _residual_ffn_norm(x_ref[0], m, ga_ref[0], fnorm_ref[...], scf_ref[0], shf_ref[0], x1_ref, hf_ref)


def _mixer_out(kernel_fn, acts, x, mod, consts, tm, name):
    bsz, L, d = x.shape
    row = lambda a: pl.BlockSpec((1, tm, a.shape[-1]), lambda b, t: (b, t, 0))
    return pl.pallas_call(
        kernel_fn,
        grid=(bsz, L // tm),
        in_specs=[row(a) for a in acts] + [row(x), _mod_spec(mod, tm, d, 2), _mod_spec(mod, tm, d, 3),
                                           _mod_spec(mod, tm, d, 4)] + [_const_spec(a) for a in consts],
        out_specs=[row(x), row(x)],
        out_shape=[jax.ShapeDtypeStruct((bsz, L, d), F32), jax.ShapeDtypeStruct((bsz, L, d), BF16)],
        compiler_params=_params("arbitrary", "arbitrary"),
        name=name,
    )(*acts, x, mod, mod, mod, *consts)


def _moe_kernel(hf_ref, x1_ref, gf_ref, wr_ref, br_ref, wg_ref, wu_ref, wd_ref, y_ref, acc_scr, gate_scr, grp_scr):
    g = pl.program_id(2)
    hf = hf_ref[0]

    @pl.when(g == 0)
    def _():
        logits = _dot(hf, wr_ref[...]) + br_ref[...]
        lane = lax.broadcasted_iota(jnp.int32, logits.shape, 1)
        big = jnp.int32(1 << 20)
        is_grp = lane < MOE_GROUPS
        lg = jnp.where(is_grp, logits, -jnp.inf)
        mg = jnp.max(lg, axis=-1, keepdims=True)
        g_val = 1.0 / jnp.sum(jnp.exp(lg - mg), axis=-1, keepdims=True)
        g_idx = jnp.min(jnp.where(is_grp & (lg == mg), lane, big), axis=-1, keepdims=True)
        lo = MOE_GROUPS + g_idx * MOE_EXP_PER_GROUP
        in_grp = (lane >= lo) & (lane < lo + MOE_EXP_PER_GROUP)
        le = jnp.where(in_grp, logits, -jnp.inf)
        me = jnp.max(le, axis=-1, keepdims=True)
        pe = jnp.exp(le - me)
        p = pe / jnp.sum(pe, axis=-1, keepdims=True)
        p1 = jnp.max(p, axis=-1, keepdims=True)
        i1 = jnp.min(jnp.where(in_grp & (p == p1), lane, big), axis=-1, keepdims=True)
        rest = jnp.where(in_grp & (lane != i1), p, -1.0)
        p2 = jnp.max(rest, axis=-1, keepdims=True)
        i2 = jnp.min(jnp.where(rest == p2, lane, big), axis=-1, keepdims=True)
        norm = g_val / (p1 + p2)
        w1, w2 = p1 * norm, p2 * norm
        for i in range(MOE_EXP_PER_GROUP):
            col = jnp.where(i1 == lo + i, w1, 0.0) + jnp.where(i2 == lo + i, w2, 0.0)
            gate_scr[i] = jnp.broadcast_to(col, gate_scr.shape[1:])
        grp_scr[...] = jnp.broadcast_to(g_idx, grp_scr.shape)
        acc_scr[...] = jnp.zeros_like(acc_scr)

    mine = grp_scr[...] == g
    total = None
    for i in range(MOE_EXP_PER_GROUP):
        gate = jnp.where(mine, gate_scr[i], 0.0)
        hid = _silu(_dot(hf, wg_ref[i])) * _dot(hf, wu_ref[i])
        hid = hid * jnp.concatenate([gate] * (hid.shape[1] // LANES), axis=1)
        part = _dot(hid.astype(BF16), wd_ref[i])
        total = part if total is None else total + part
    acc_scr[...] += total

    @pl.when(g == pl.num_programs(2) - 1)
    def _():
        y_ref[0] = x1_ref[0] + gf_ref[0] * acc_scr[...]


def _moe(hf, x1, mod, w_route, b_route, w_gate, w_up, w_down, tm):
    bsz, L, d = x1.shape
    n_exp, _, d_ff = w_gate.shape
    row = pl.BlockSpec((1, tm, d), lambda b, t, e: (b, t, 0))
    if mod.shape[1] == 1:
        gspec = pl.BlockSpec((1, 1, d), lambda b, t, e: (b, 0, 5))
    else:
        gspec = pl.BlockSpec((1, tm, d), lambda b, t, e: (b, t, 5))
    return pl.pallas_call(
        _moe_kernel,
        grid=(bsz, L // tm, n_exp // MOE_EXP_PER_GROUP),
        in_specs=[row, row, gspec,
                  pl.BlockSpec(w_route.shape, lambda b, t, e: (0, 0)),
                  pl.BlockSpec(b_route.shape, lambda b, t, e: (0, 0)),
                  pl.BlockSpec((MOE_EXP_PER_GROUP, d, d_ff), lambda b, t, e: (e, 0, 0)),
                  pl.BlockSpec((MOE_EXP_PER_GROUP, d, d_ff), lambda b, t, e: (e, 0, 0)),
                  pl.BlockSpec((MOE_EXP_PER_GROUP, d_ff, d), lambda b, t, e: (e, 0, 0))],
        out_specs=row,
        out_shape=jax.ShapeDtypeStruct((bsz, L, d), F32),
        scratch_shapes=[pltpu.VMEM((tm, d), F32), pltpu.VMEM((MOE_EXP_PER_GROUP, tm, LANES), F32),
                        pltpu.VMEM((tm, LANES), jnp.int32)],
        compiler_params=_params("arbitrary", "arbitrary", "arbitrary"),
        name="moe",
    )(hf, x1, mod, w_route, b_route, w_gate, w_up, w_down)


def _kv_kernel(x_ref, innorm_ref, wc_ref, wr_ref, wrot_ref, latnorm_ref, cos_ref, sin_ref, ckv_ref, kr_ref):
    hn = _rms(x_ref[0], innorm_ref[...]).astype(BF16)
    ckv_ref[0] = _rms(_dot(hn, wc_ref[...]), latnorm_ref[...])
    kr_ref[0] = _dot(hn, wr_ref[...]) * cos_ref[...] + _dot(hn, wrot_ref[...]) * sin_ref[...]


def _rope_spec(table, tm):
    if table.shape[0] == 1:
        return pl.BlockSpec((1, LANES), lambda b, t: (0, 0))
    return pl.BlockSpec((tm, LANES), lambda b, t: (t, 0))


def _shared_kv(x, consts, cos, sin, tm):
    bsz, L, d = x.shape
    rank = consts[1].shape[1]
    row = lambda n: pl.BlockSpec((1, tm, n), lambda b, t: (b, t, 0))
    return pl.pallas_call(
        _kv_kernel,
        grid=(bsz, L // tm),
        in_specs=[row(d)] + [_const_spec(a) for a in consts] + [_rope_spec(cos, tm), _rope_spec(sin, tm)],
        out_specs=[row(rank), row(LANES)],
        out_shape=[jax.ShapeDtypeStruct((bsz, L, rank), F32), jax.ShapeDtypeStruct((bsz, L, LANES), F32)],
        compiler_params=_params("arbitrary", "arbitrary"),
        name="shared_kv",
    )(x, *consts, cos, sin)


def _mla_q_kernel(x_ref, sh_ref, sc_ref, mnorm_ref, wdq_ref, qlat_ref, wuq_ref, wuqrot_ref, qnorm_ref,
                  cos_ref, sin_ref, q_ref, *, scale):
    hm = _rms(x_ref[0], mnorm_ref[...]) * (1.0 + sc_ref[0]) + sh_ref[0]
    cq = _rms(_dot(hm.astype(BF16), wdq_ref[...]), qlat_ref[...]).astype(BF16)
    qp = _dot(cq, wuq_ref[...])
    qr = _dot(cq, wuqrot_ref[...])
    for h in range(MLA_HEADS):
        hs = slice(h * LANES, (h + 1) * LANES)
        a = qp[:, hs] * cos_ref[...] + qr[:, hs] * sin_ref[...]
        inv = lax.rsqrt(jnp.sum(a * a, axis=-1, keepdims=True) * (1.0 / MLA_QK) + NORM_EPS)
        q_ref[0, :, hs] = (a * inv * (qnorm_ref[...] * scale)).astype(q_ref.dtype)


def _mla_q(x, mod, consts, cos, sin, tm, out_dtype):
    bsz, L, d = x.shape
    n = MLA_HEADS * LANES
    row = lambda w: pl.BlockSpec((1, tm, w), lambda b, t: (b, t, 0))
    return pl.pallas_call(
        functools.partial(_mla_q_kernel, scale=float(MLA_QK ** -0.5)),
        grid=(bsz, L // tm),
        in_specs=[row(d), _mod_spec(mod, tm, d, 0), _mod_spec(mod, tm, d, 1)] + [_const_spec(a) for a in consts]
        + [_rope_spec(cos, tm), _rope_spec(sin, tm)],
        out_specs=row(n),
        out_shape=jax.ShapeDtypeStruct((bsz, L, n), out_dtype),
        compiler_params=_params("arbitrary", "arbitrary"),
        name="mla_q",
    )(x, mod, mod, *consts, cos, sin)


def _kv_up_kernel(ckv_ref, kr_ref, wuk_ref, knorm_ref, *rest, vt_tile):
    if vt_tile:
        wuvt_ref, k_ref, vt_ref = rest
    else:
        (k_ref,) = rest
    c = ckv_ref[0].astype(BF16)
    kn = _dot(c, wuk_ref[...])
    for h in range(MLA_HEADS):
        hs = slice(h * LANES, (h + 1) * LANES)
        a = kn[:, hs] + kr_ref[0]
        inv = lax.rsqrt(jnp.sum(a * a, axis=-1, keepdims=True) * (1.0 / MLA_QK) + NORM_EPS)
        k_ref[0, :, hs] = (a * inv * knorm_ref[...]).astype(k_ref.dtype)
    if vt_tile:
        for s in range(c.shape[0] // vt_tile):
            vt = _dot_nt(wuvt_ref[...], c[s * vt_tile:(s + 1) * vt_tile])
            vt_ref[0, s] = vt.astype(vt_ref.dtype)


def _kv_up(ckv, kr_slot, consts, tm, out_dtype, vt_tile=0):
    bsz, L, rank = ckv.shape
    n = MLA_HEADS * LANES
    row = lambda w: pl.BlockSpec((1, tm, w), lambda b, t: (b, t, 0))
    out_specs = [row(n)]
    out_shape = [jax.ShapeDtypeStruct((bsz, L, n), out_dtype)]
    if vt_tile:
        nv = consts[-1].shape[0]
        out_specs.append(pl.BlockSpec((1, tm // vt_tile, nv, vt_tile), lambda b, t: (b, t, 0, 0)))
        out_shape.append(jax.ShapeDtypeStruct((bsz, L // vt_tile, nv, vt_tile), out_dtype))
    return pl.pallas_call(
        functools.partial(_kv_up_kernel, vt_tile=vt_tile),
        grid=(bsz, L // tm),
        in_specs=[row(rank), row(LANES)] + [_const_spec(a) for a in consts],
        out_specs=out_specs,
        out_shape=out_shape,
        compiler_params=_params("arbitrary", "arbitrary"),
        name="kv_up",
    )(ckv, kr_slot, *consts)


def _flash_kernel(q_ref, k_ref, vt_ref, o_ref, *, tq, tk):
    qi = pl.program_id(2)
    sub = tk // tq
    n_full = (qi * tq) // tk

    def update(carry, c, masked):
        new = []
        off = pl.multiple_of(c * tk, tk)
        scores = [_dot_nt(k_ref[0, pl.ds(off, tk), hh * LANES:(hh + 1) * LANES],
                          q_ref[0, :, hh * LANES:(hh + 1) * LANES]) for hh in range(len(carry))]
        for hh, (m, l, acc) in enumerate(carry):
            s = scores[hh]
            if masked:
                key = lax.broadcasted_iota(jnp.int32, (tk, tq), 0) + c * tk
                query = lax.broadcasted_iota(jnp.int32, (tk, tq), 1) + qi * tq
                s = jnp.where(key <= query, s, -jnp.inf)
            m_new = jnp.maximum(m, jnp.max(s, axis=0, keepdims=True))
            p = jnp.exp(s - m_new)
            corr = jnp.exp(m - m_new)
            l = l * corr + jnp.sum(p, axis=0, keepdims=True)
            pb = p.astype(BF16)
            acc = acc * corr
            for t in range(sub):
                vt = vt_ref[0, c * sub + t, hh * MLA_V:(hh + 1) * MLA_V, :]
                acc = acc + _dot(vt, pb[t * tq:(t + 1) * tq])
            new.append((m_new, l, acc))
        return tuple(new)

    one = (jnp.full((1, tq), NEG_BIG, F32), jnp.zeros((1, tq), F32), jnp.zeros((MLA_V, tq), F32))
    carry = lax.fori_loop(0, n_full, lambda c, cr: update(cr, c, False), (one, one))
    carry = update(carry, n_full, True)
    out_t = jnp.concatenate([acc / l for (_, l, acc) in carry], axis=0)
    o_ref[0] = out_t.T.astype(o_ref.dtype)


def _flash(q, k, vt, tq):
    bsz, S, _ = q.shape
    pairs = MLA_HEADS // 2
    qspec = pl.BlockSpec((1, tq, 2 * LANES), lambda b, h, i: (b, i, h))
    kspec = pl.BlockSpec((1, S, 2 * LANES), lambda b, h, i: (b, 0, h))
    vspec = pl.BlockSpec((1, S // tq, 2 * MLA_V, tq), lambda b, h, i: (b, 0, h, 0))
    return pl.pallas_call(
        functools.partial(_flash_kernel, tq=tq, tk=min(FLASH_KEYS, S)),
        grid=(bsz, pairs, S // tq),
        in_specs=[qspec, kspec, vspec],
        out_specs=pl.BlockSpec((1, tq, 2 * MLA_V), lambda b, h, i: (b, i, h)),
        out_shape=jax.ShapeDtypeStruct((bsz, S, MLA_HEADS * MLA_V), BF16),
        compiler_params=_params("arbitrary", "arbitrary", "arbitrary"),
        name="flash",
    )(q, k, vt)


def _paged_kernel(pt_ref, ckv_hbm, krt_hbm, q_ref, knew_ref, cnew_ref, knorm_ref, wukt_ref, wukt_slot_ref,
                  wuv_slot_ref, o_ref, m_scr, l_scr, acc_scr, lhs_scr, c_scr, cbuf, krbuf, sems, *, n_pages):
    b, j = pl.program_id(0), pl.program_id(1)
    steps = pl.num_programs(1)
    step = b * steps + j
    last_step = pl.num_programs(0) * steps - 1
    slot = step % 2
    nh = MLA_HEADS
    n_kn = nh * MLA_NOPE
    page = ckv_hbm.shape[1]
    tile = 2 * page
    head = lax.broadcasted_iota(jnp.int32, (nh, 1), 0)

    def page_copies(pid, p, to_slot):
        return (pltpu.make_async_copy(ckv_hbm.at[pid], cbuf.at[to_slot, pl.ds(p * page, page), :],
                                      sems.at[to_slot, 0]),
                pltpu.make_async_copy(krt_hbm.at[pid], krbuf.at[to_slot, :, pl.ds(p * page, page)],
                                      sems.at[to_slot, 1]))

    def start_pages(for_step, to_slot, positions):
        fb, fj = for_step // steps, for_step % steps
        for p in positions:
            for cp in page_copies(pt_ref[fb, fj * n_pages + p], p, to_slot):
                cp.start()

    def wait_pages(in_slot):
        for p in range(n_pages):
            for cp in page_copies(0, p, in_slot):
                cp.wait()

    @pl.when(step == 0)
    def _():
        start_pages(step, slot, range(n_pages))

    qg = q_ref[0] * knorm_ref[...]

    @pl.when(j == 0)
    def _():
        m_scr[...] = jnp.full_like(m_scr, NEG_BIG)
        l_scr[...] = jnp.zeros_like(l_scr)
        acc_scr[...] = jnp.zeros_like(acc_scr)
        qb = qg.astype(BF16)
        qabs = jnp.zeros((nh, lhs_scr.shape[1]), F32)
        for h in range(nh):
            qabs = qabs + jnp.where(head == h, _dot(qb, wukt_slot_ref[h]), 0.0)
        lhs_scr[:n_kn, :] = wukt_ref[...]
        lhs_scr[n_kn:, :] = jnp.concatenate([qabs, jnp.zeros_like(qabs)], axis=0).astype(BF16)

    wait_pages(slot)
    nxt = jnp.minimum(step + 1, last_step)
    q_rope = qg[:, MLA_NOPE:MLA_QK].astype(BF16)
    scores = []
    for i in range(n_pages // 2):
        start_pages(nxt, 1 - slot, (2 * i, 2 * i + 1))
        c2 = cbuf[slot, i * tile:(i + 1) * tile, :].astype(BF16)
        c_scr[i * tile:(i + 1) * tile, :] = c2
        both = _dot_nt(lhs_scr[...], c2)
        knt = both[:n_kn]
        ss = jnp.sum((knt * knt).reshape(nh, MLA_NOPE, tile), axis=1)
        kr2 = krbuf[slot, :, i * tile:(i + 1) * tile]
        ss = ss + jnp.sum(kr2 * kr2, axis=0, keepdims=True)
        num = both[n_kn:n_kn + nh] + _dot(q_rope, kr2.astype(BF16))
        scores.append(num * lax.rsqrt(ss * (1.0 / MLA_QK) + NORM_EPS))
    s = jnp.concatenate(scores, axis=1)
    m = m_scr[...]
    m_new = jnp.maximum(m, jnp.max(s, axis=-1, keepdims=True))
    pr = jnp.exp(s - m_new)
    corr = jnp.exp(m - m_new)
    l = l_scr[...] * corr + jnp.sum(pr, axis=-1, keepdims=True)
    pb = pr.astype(BF16)
    pv = [None, None]
    for i in range(n_pages // 2):
        part = _dot(pb[:, i * tile:(i + 1) * tile], c_scr[i * tile:(i + 1) * tile, :])
        pv[i % 2] = part if pv[i % 2] is None else pv[i % 2] + part
    acc = acc_scr[...] * corr + (pv[0] + pv[1])
    m_scr[...] = m_new
    l_scr[...] = l
    acc_scr[...] = acc

    @pl.when(j == steps - 1)
    def _():
        s_new = jnp.sum(q_ref[0] * knew_ref[0], axis=-1, keepdims=True)
        m_fin = jnp.maximum(m_new, s_new)
        corr_fin = jnp.exp(m_new - m_fin)
        p_new = jnp.exp(s_new - m_fin)
        l_fin = l * corr_fin + p_new
        o_lat = ((acc * corr_fin + p_new * cnew_ref[0]) / l_fin).astype(BF16)
        out = jnp.zeros((nh, LANES), F32)
        for h in range(nh):
            out = out + jnp.where(head == h, _dot(o_lat, wuv_slot_ref[h]), 0.0)
        o_ref[0] = out

    @pl.when(step == last_step)
    def _():
        wait_pages(1 - slot)


def _paged_attention(q, k_new, c_new, cache_ckv, cache_krope_t, page_table, consts, n_pages):
    nb, nh, _ = q.shape
    _, page, rank = cache_ckv.shape
    rope = cache_krope_t.shape[1]
    steps = page_table.shape[1] // n_pages
    assert n_pages % 2 == 0 and steps * n_pages == page_table.shape[1]
    per_b = lambda a: pl.BlockSpec((1,) + a.shape[1:], lambda b, j, pt: (b,) + (0,) * (a.ndim - 1))
    const = lambda a: pl.BlockSpec(a.shape, lambda b, j, pt: (0,) * a.ndim)
    hbm = pl.BlockSpec(memory_space=pl.ANY)
    grid_spec = pltpu.PrefetchScalarGridSpec(
        num_scalar_prefetch=1,
        grid=(nb, steps),
        in_specs=[hbm, hbm, per_b(q), per_b(k_new), per_b(c_new)] + [const(a) for a in consts],
        out_specs=pl.BlockSpec((1, nh, LANES), lambda b, j, pt: (b, 0, 0)),
        scratch_shapes=[pltpu.VMEM((nh, 1), F32), pltpu.VMEM((nh, 1), F32), pltpu.VMEM((nh, rank), F32),
                        pltpu.VMEM((nh * MLA_NOPE + 2 * nh, rank), BF16),
                        pltpu.VMEM((n_pages * page, rank), BF16),
                        pltpu.VMEM((2, n_pages * page, rank), F32),
                        pltpu.VMEM((2, rope, n_pages * page), F32),
                        pltpu.SemaphoreType.DMA((2, 2))],
    )
    return pl.pallas_call(
        functools.partial(_paged_kernel, n_pages=n_pages),
        grid_spec=grid_spec,
        out_shape=jax.ShapeDtypeStruct((nb, nh, LANES), F32),
        compiler_params=_params("arbitrary", "arbitrary"),
        name="paged_attn",
    )(page_table, cache_ckv, cache_krope_t, q, k_new, c_new, *consts)


def _rotate_half_cols(w):
    half = w.shape[-1] // 2
    return jnp.concatenate([-w[..., half:], w[..., :half]], axis=-1)


def _slot(nope, rope):
    ref = nope if nope is not None else rope
    lead = ref.shape[:-1]
    z = lambda n: jnp.zeros(lead + (n,), ref.dtype)
    a = nope if nope is not None else z(MLA_NOPE)
    b = rope if rope is not None else z(MLA_ROPE)
    return jnp.concatenate([a, b, z(LANES - MLA_QK)], axis=-1)


def _rope_tables(pos):
    inv = ROPE_THETA ** (-jnp.arange(0, MLA_ROPE, 2, dtype=F32) / MLA_ROPE)
    ang = pos.astype(F32)[:, None] * inv[None, :]
    ang = jnp.concatenate([ang, ang], axis=-1)
    n = pos.shape[0]
    cos = _slot(jnp.ones((n, MLA_NOPE), F32), jnp.cos(ang))
    sin = _slot(jnp.zeros((n, MLA_NOPE), F32), jnp.sin(ang))
    return cos, sin


def _prepare(w):
    d = w['mix_norm'].shape[-1]
    p = {}
    p['route_w'] = [jnp.pad(jnp.concatenate([w['moe_w_group'][l], w['moe_w_router'][l]], axis=-1),
                            ((0, 0), (0, LANES - MOE_GROUPS - MOE_EXPERTS))).astype(BF16) for l in range(2)]
    p['route_b'] = [jnp.pad(jnp.concatenate([w['moe_b_group'][l], w['moe_b_router'][l]], axis=-1),
                            (0, LANES - MOE_GROUPS - MOE_EXPERTS)).reshape(1, LANES) for l in range(2)]
    p['moe_gate'] = w['moe_w_gate'].astype(BF16)
    p['moe_up'] = w['moe_w_up'].astype(BF16)
    p['moe_down'] = w['moe_w_down'].astype(BF16)
    p['gla_out'] = [w['gla_out_norm'][0].reshape(1, -1), w['gla_w_out'][0].astype(BF16),
                    w['ffn_norm'][0].reshape(1, d)]
    rank = w['kv_lat_norm'].shape[0]
    w_r = w['kv_w_dkv'][:, rank:]
    p['kv'] = [w['kv_in_norm'].reshape(1, d), w['kv_w_dkv'][:, :rank].astype(BF16),
               _slot(None, w_r).astype(BF16), _slot(None, _rotate_half_cols(w_r)).astype(BF16),
               w['kv_lat_norm'].reshape(1, rank)]
    q_rank = w['mla_w_dq'].shape[-1]
    wuq = w['mla_w_uq'][0].reshape(q_rank, MLA_HEADS, MLA_QK)
    wuq_n, wuq_r = wuq[..., :MLA_NOPE], wuq[..., MLA_NOPE:]
    p['mla_q'] = [w['mix_norm'][1].reshape(1, d), w['mla_w_dq'][0].astype(BF16),
                  w['mla_q_lat_norm'][0].reshape(1, q_rank),
                  _slot(wuq_n, wuq_r).reshape(q_rank, -1).astype(BF16),
                  _slot(None, _rotate_half_cols(wuq_r)).reshape(q_rank, -1).astype(BF16),
                  jnp.pad(w['mla_q_norm'][0], (0, LANES - MLA_QK)).reshape(1, LANES)]
    knorm = jnp.pad(w['kv_k_norm'], (0, LANES - MLA_QK)).reshape(1, LANES)
    wuk, wuv = w['kv_w_uk'], w['kv_w_uv']
    p['kv_up'] = [_slot(wuk, None).reshape(rank, -1).astype(BF16), knorm]
    p['wuv_t'] = wuv.reshape(rank, -1).T.astype(BF16)
    wukt = wuk.transpose(1, 2, 0)
    p['paged'] = [knorm, wukt.reshape(MLA_HEADS * MLA_NOPE, rank).astype(BF16),
                  jnp.pad(wukt, ((0, 0), (0, LANES - MLA_NOPE), (0, 0))).astype(BF16),
                  jnp.pad(wuv.transpose(1, 0, 2), ((0, 0), (0, 0), (0, LANES - MLA_V))).astype(BF16)]
    wo = w['mla_w_out'][0].reshape(MLA_HEADS, MLA_V, d)
    p['mla_out_slot'] = [jnp.pad(wo, ((0, 0), (0, LANES - MLA_V), (0, 0))).reshape(MLA_HEADS * LANES, d)
                         .astype(BF16), w['ffn_norm'][1].reshape(1, d)]
    p['mla_out'] = [w['mla_w_out'][0].astype(BF16), w['ffn_norm'][1].reshape(1, d)]
    return p


def _moe_layer(l, hf, x1, mod, p):
    return _moe(hf, x1, mod, p['route_w'][l], p['route_b'][l], p['moe_gate'][l], p['moe_up'][l],
                p['moe_down'][l], min(MOE_ROWS, x1.shape[1]))


def _trunk(x, mods, pos, gla_s0, paged, w, p, tm):
    cos, sin = _rope_tables(pos)
    gla_w = (w['mix_norm'][0], w['gla_w_in'][0], w['gla_w_gate_up'][0], w['gla_b_gate'][0])
    if paged is None:
        x1, hf, s_fin = _gla_layer(x, mods[0], gla_s0, _gla_proj_consts(*gla_w), p['gla_out'], tm)
    else:
        q, k, v, r, gk = _gla_proj(x, mods[0], *gla_w, tm)
        o, s_fin = _gla_step(q[0], k[0], v[0], gk[0], gla_s0)
        x1, hf = _mixer_out(functools.partial(_gla_out_kernel, dv=v.shape[-1] // GLA_HEADS), [o[None], r], x,
                            mods[0], p['gla_out'], tm, "gla_out")
    x2 = _moe_layer(0, hf, x1, mods[0], p)
    ckv, kr_slot = _shared_kv(x2, p['kv'], cos, sin, tm)
    if paged is None:
        qh = _mla_q(x2, mods[1], p['mla_q'], cos, sin, tm, BF16)
        tq = min(FLASH_T, x.shape[1])
        kh, vt = _kv_up(ckv, kr_slot, p['kv_up'] + [p['wuv_t']], tm, BF16, vt_tile=tq)
        o = _flash(qh, kh, vt, tq)
        out_consts = p['mla_out']
    else:
        n = x.shape[1]
        qh = _mla_q(x2, mods[1], p['mla_q'], cos, sin, tm, F32)
        (kh,) = _kv_up(ckv, kr_slot, p['kv_up'], tm, F32)
        o = _paged_attention(qh.reshape(n, MLA_HEADS, LANES), kh.reshape(n, MLA_HEADS, LANES),
                             ckv.reshape(n, 1, -1), paged[0], jnp.swapaxes(paged[1], 1, 2), paged[2], p['paged'],
                             min(32, paged[2].shape[1]))
        o = o.reshape(1, n, MLA_HEADS * LANES).astype(BF16)
        out_consts = p['mla_out_slot']
    x3, hf = _mixer_out(_mla_out_kernel, [o], x2, mods[1], out_consts, tm, "mla_out")
    y = _moe_layer(1, hf, x3, mods[1], p)
    return y, s_fin, ckv, kr_slot[..., MLA_NOPE:MLA_QK]


def _forward(x_prompt, x_sample, c_prompt, c_sample, state_gla, cache_ckv, cache_krope, page_table, w, past_len):
    bsz, seq, d = x_prompt.shape
    nd, dl, _ = x_sample.shape
    assert dl == 1, "decode path handles one new token per sequence"
    p = _prepare(w)
    mod = _ada_mod(jnp.concatenate([c_prompt, c_sample], axis=0), w['ada_w'], w['ada_b'])
    mods_p = [mod[l, :bsz].reshape(bsz, 1, -1) for l in range(2)]
    mods_s = [mod[l, bsz:].reshape(1, nd, -1) for l in range(2)]
    s0_p = jnp.zeros((bsz,) + state_gla.shape[2:], state_gla.dtype)
    y_p, s_p, ckv_p, kr_p = _trunk(x_prompt, mods_p, jnp.arange(seq), s0_p, None, w, p, min(512, seq))
    y_s, s_s, ckv_s, kr_s = _trunk(x_sample.reshape(1, nd, d), mods_s, jnp.full((1,), past_len), state_gla[0],
                                   (cache_ckv, cache_krope, page_table), w, p, nd)
    return (y_p, y_s.reshape(nd, 1, d), s_p[None], s_s[None], ckv_p, kr_p,
            ckv_s.reshape(nd, 1, -1), kr_s.reshape(nd, 1, -1))


def kernel(x_prompt, x_sample, c_prompt, c_sample, state_gla, cache_ckv, cache_krope, page_table, ada_w, ada_b, mix_norm, ffn_norm, gla_w_in, gla_w_gate_up, gla_b_gate, gla_out_norm, gla_w_out, mla_w_dq, mla_q_lat_norm, mla_w_uq, mla_q_norm, mla_w_out, kv_in_norm, kv_w_dkv, kv_lat_norm, kv_w_uk, kv_w_uv, kv_k_norm, moe_w_group, moe_b_group, moe_w_router, moe_b_router, moe_w_gate, moe_w_up, moe_w_down):
    w = dict(ada_w=ada_w, ada_b=ada_b, mix_norm=mix_norm, ffn_norm=ffn_norm,
             gla_w_in=gla_w_in, gla_w_gate_up=gla_w_gate_up, gla_b_gate=gla_b_gate,
             gla_out_norm=gla_out_norm, gla_w_out=gla_w_out,
             mla_w_dq=mla_w_dq, mla_q_lat_norm=mla_q_lat_norm, mla_w_uq=mla_w_uq,
             mla_q_norm=mla_q_norm, mla_w_out=mla_w_out,
             kv_in_norm=kv_in_norm, kv_w_dkv=kv_w_dkv, kv_lat_norm=kv_lat_norm,
             kv_w_uk=kv_w_uk, kv_w_uv=kv_w_uv, kv_k_norm=kv_k_norm,
             moe_w_group=moe_w_group, moe_b_group=moe_b_group, moe_w_router=moe_w_router,
             moe_b_router=moe_b_router, moe_w_gate=moe_w_gate, moe_w_up=moe_w_up, moe_w_down=moe_w_down)
    past_len = page_table.shape[1] * cache_ckv.shape[1]
    return _forward(x_prompt, x_sample, c_prompt, c_sample, state_gla, cache_ckv, cache_krope, page_table, w,
                    past_len)
```

```python
import functools
import math

import jax
import jax.numpy as jnp
from jax import lax
from jax.experimental import pallas as pl
from jax.experimental.pallas import tpu as pltpu

F32 = jnp.float32
BF16 = jnp.bfloat16

NORM_EPS = 1e-6
GLA_HEADS = 4
GLA_GATE_NORM = 16.0
GLA_CHUNK = 64
GLA_SAFE_EXPONENT = 80.0
MLA_HEADS = 8
MLA_NOPE = 64
MLA_ROPE = 32
MLA_QK = MLA_NOPE + MLA_ROPE
MLA_V = 64
ROPE_THETA = 10000.0
MOE_GROUPS = 4
MOE_EXP_PER_GROUP = 4
MOE_EXPERTS = MOE_GROUPS * MOE_EXP_PER_GROUP
LANES = 128
SUBLANES = 8
NEG_BIG = -1e30
FLASH_T = 256
FLASH_KEYS = 1024
MOE_ROWS = 1024
FLASH_HEADS = 4
VMEM_LIMIT = 52 * 1024 * 1024


def _params(*sem):
    return pltpu.CompilerParams(dimension_semantics=sem, vmem_limit_bytes=VMEM_LIMIT)


def _dot(a, b):
    return jnp.dot(a, b, preferred_element_type=F32)


def _dot_nt(a, b):
    return lax.dot_general(a, b, (((1,), (1,)), ((), ())), preferred_element_type=F32)


def _dot_tn(a, b):
    return lax.dot_general(a, b, (((0,), (0,)), ((), ())), preferred_element_type=F32)


def _rms(x, g):
    return x * lax.rsqrt(jnp.mean(x * x, axis=-1, keepdims=True) + NORM_EPS) * g


def _silu(x):
    return x / (1.0 + jnp.exp(-x))


def _split_bf16(x):
    hi = x.astype(BF16)
    lo = (x - hi.astype(F32)).astype(BF16)
    return hi, lo


def _ada_kernel(c_ref, w_ref, b_ref, o_ref):
    a = _silu(c_ref[...]).astype(BF16)
    o_ref[0] = _dot(a, w_ref[0].astype(BF16)) + b_ref[0]


def _ada_mod(c, ada_w, ada_b):
    depth, d, n = ada_w.shape
    r = c.shape[0]
    tn = 1536
    return pl.pallas_call(
        _ada_kernel,
        grid=(depth, n // tn),
        in_specs=[
            pl.BlockSpec((r, d), lambda l, j: (0, 0)),
            pl.BlockSpec((1, d, tn), lambda l, j: (l, 0, j)),
            pl.BlockSpec((1, 1, tn), lambda l, j: (l, 0, j)),
        ],
        out_specs=pl.BlockSpec((1, r, tn), lambda l, j: (l, 0, j)),
        out_shape=jax.ShapeDtypeStruct((depth, r, n), F32),
        compiler_params=_params("arbitrary", "arbitrary"),
        name="ada_mod",
    )(c, ada_w, ada_b.reshape(depth, 1, n))


def _mod_spec(mod, tm, d, chunk):
    if mod.shape[1] == 1:
        return pl.BlockSpec((1, 1, d), lambda b, t: (b, 0, chunk))
    return pl.BlockSpec((1, tm, d), lambda b, t: (b, t, chunk))


def _const_spec(arr):
    nd = arr.ndim
    return pl.BlockSpec(arr.shape, lambda b, t: (0,) * nd)


def _gla_proj_kernel(x_ref, sh_ref, sc_ref, g_ref, wq_ref, wk_ref, wv_ref, wr_ref, wg_ref, wgu_ref, bg_ref,
                     q_ref, k_ref, v_ref, r_ref, gk_ref, *, q_scale):
    hm = _rms(x_ref[0], g_ref[...]) * (1.0 + sc_ref[0]) + sh_ref[0]
    hb = hm.astype(BF16)
    q_ref[0] = _dot(hb, wq_ref[...]) * q_scale
    k_ref[0] = _dot(hb, wk_ref[...])
    v_ref[0] = _dot(hb, wv_ref[...]).astype(BF16)
    r_ref[0] = _dot(hb, wr_ref[...])
    g_low = _dot(hb, wg_ref[...])
    z = _dot(g_low.astype(BF16), wgu_ref[...]) + bg_ref[...]
    log_sig = jnp.minimum(z, 0.0) - jnp.log(1.0 + jnp.exp(-jnp.abs(z)))
    gk_ref[0] = log_sig * (1.0 / GLA_GATE_NORM)


def _gla_proj_consts(norm_g, w_in, w_gate_up, b_gate):
    hk = w_gate_up.shape[1]
    hv = (w_in.shape[1] - 2 * hk - w_gate_up.shape[0]) // 2
    cuts = [0, hk, 2 * hk, 2 * hk + hv, 2 * hk + 2 * hv, w_in.shape[1]]
    pieces = [w_in[:, a:b].astype(BF16) for a, b in zip(cuts[:-1], cuts[1:])]
    return [norm_g.reshape(1, -1)] + pieces + [w_gate_up.astype(BF16), b_gate.reshape(1, hk)]


def _gla_proj(x, mod, norm_g, w_in, w_gate_up, b_gate, tm):
    bsz, L, d = x.shape
    consts = _gla_proj_consts(norm_g, w_in, w_gate_up, b_gate)
    hk, hv = consts[1].shape[1], consts[3].shape[1]
    row = lambda n: pl.BlockSpec((1, tm, n), lambda b, t: (b, t, 0))
    return pl.pallas_call(
        functools.partial(_gla_proj_kernel, q_scale=float((hk // GLA_HEADS) ** -0.5)),
        grid=(bsz, L // tm),
        in_specs=[row(d), _mod_spec(mod, tm, d, 0), _mod_spec(mod, tm, d, 1)] + [_const_spec(a) for a in consts],
        out_specs=[row(hk), row(hk), row(hv), row(hv), row(hk)],
        out_shape=[
            jax.ShapeDtypeStruct((bsz, L, hk), F32),
            jax.ShapeDtypeStruct((bsz, L, hk), F32),
            jax.ShapeDtypeStruct((bsz, L, hv), BF16),
            jax.ShapeDtypeStruct((bsz, L, hv), F32),
            jax.ShapeDtypeStruct((bsz, L, hk), F32),
        ],
        compiler_params=_params("arbitrary", "arbitrary"),
        name="gla_proj",
    )(x, mod, mod, *consts)


def _gla_layer_kernel(x_ref, sh_ref, sc_ref, ga_ref, shf_ref, scf_ref, s0_ref, mnorm_ref, wq_ref, wk_ref, wv_ref,
                      wr_ref, wg_ref, wgu_ref, bg_ref, onorm_ref, wout_ref, fnorm_ref,
                      x1_ref, hf_ref, sf_ref, st_scr, o_scr, b_scr, k_scr, *, chunk, q_scale):
    t = pl.program_id(1)
    n_heads, dv, dk = st_scr.shape

    @pl.when(t == 0)
    def _():
        for h in range(n_heads):
            st_scr[h] = s0_ref[0, h].T

    x = x_ref[0]
    tm = x.shape[0]
    n_chunks = tm // chunk
    hb = (_rms(x, mnorm_ref[...]) * (1.0 + sc_ref[0]) + sh_ref[0]).astype(BF16)
    q = _dot(hb, wq_ref[...]) * q_scale
    k = _dot(hb, wk_ref[...])
    v = _dot(hb, wv_ref[...]).astype(BF16)
    r = _dot(hb, wr_ref[...])
    z = _dot(_dot(hb, wg_ref[...]).astype(BF16), wgu_ref[...]) + bg_ref[...]
    gk = (jnp.minimum(z, 0.0) - jnp.log(1.0 + jnp.exp(-jnp.abs(z)))) * (1.0 / GLA_GATE_NORM)
    tri = jnp.where(lax.broadcasted_iota(jnp.int32, (tm, tm), 0) >= lax.broadcasted_iota(jnp.int32, (tm, tm), 1),
                    1.0, 0.0).astype(BF16)
    g_hi, g_lo = _split_bf16(gk)
    cum = _dot(tri, g_hi) + _dot(tri, g_lo)
    causal = (lax.broadcasted_iota(jnp.int32, (chunk, chunk), 0)
              >= lax.broadcasted_iota(jnp.int32, (chunk, chunk), 1))

    def local_gate(c, cols):
        b = cum[c * chunk:(c + 1) * chunk, cols]
        return b - cum[c * chunk - 1:c * chunk, cols] if c else b

    total_decay = jnp.concatenate([local_gate(c, slice(None))[chunk - 1:chunk] for c in range(n_chunks)], axis=0)
    factored_ok = jnp.max(-total_decay) <= GLA_SAFE_EXPONENT

    @pl.when(factored_ok)
    def _():
        for h in range(n_heads):
            ks = slice(h * dk, (h + 1) * dk)
            vs = slice(h * dv, (h + 1) * dv)
            st = st_scr[h]
            for c in range(n_chunks):
                rows = slice(c * chunk, (c + 1) * chunk)
                b = local_gate(c, ks)
                qt = (q[rows, ks] * jnp.exp(b)).astype(BF16)
                kt = (k[rows, ks] * jnp.exp(-b)).astype(BF16)
                vc = v[rows, vs]
                attn = jnp.where(causal, _dot_nt(qt, kt), 0.0).astype(BF16)
                o_scr[rows, vs] = _dot(attn, vc) + _dot_nt(qt, st.astype(BF16))
                st = (st + _dot_tn(vc, kt)) * jnp.exp(b[chunk - 1:chunk, :])
            st_scr[h] = st

    @pl.when(jnp.logical_not(factored_ok))
    def _():
        k_scr[...] = k
        lane = lax.broadcasted_iota(jnp.int32, (chunk, chunk), 1)
        for h in range(n_heads):
            ks = slice(h * dk, (h + 1) * dk)
            vs = slice(h * dv, (h + 1) * dv)
            st = st_scr[h]
            for c in range(n_chunks):
                rows = slice(c * chunk, (c + 1) * chunk)
                b = local_gate(c, ks)
                b_scr[rows, ks] = b
                qc = q[rows, ks]

                def key_rows(grp, attn, c=c, ks=ks, b=b, qc=qc):
                    first = pl.multiple_of(c * chunk + grp * SUBLANES, SUBLANES)
                    b_grp = b_scr[pl.ds(first, SUBLANES), ks]
                    k_grp = k_scr[pl.ds(first, SUBLANES), ks]
                    for i in range(SUBLANES):
                        decay = jnp.exp(jnp.minimum(b - b_grp[i:i + 1], 0.0))
                        col = jnp.sum(qc * k_grp[i:i + 1] * decay, axis=-1, keepdims=True)
                        attn = jnp.where(lane == grp * SUBLANES + i, col, attn)
                    return attn

                attn = lax.fori_loop(0, chunk // SUBLANES, key_rows, jnp.zeros((chunk, chunk), F32))
                attn = jnp.where(causal, attn, 0.0).astype(BF16)
                vc = v[rows, vs]
                b_end = b[chunk - 1:chunk, :]
                qt = (qc * jnp.exp(b)).astype(BF16)
                o_scr[rows, vs] = _dot(attn, vc) + _dot_nt(qt, st.astype(BF16))
                kd = (k[rows, ks] * jnp.exp(b_end - b)).astype(BF16)
                st = st * jnp.exp(b_end) + _dot_tn(vc, kd)
            st_scr[h] = st

    parts = []
    for h in range(n_heads):
        vs = slice(h * dv, (h + 1) * dv)
        parts.append((_rms(o_scr[:, vs], onorm_ref[...]) * _silu(r[:, vs])).astype(BF16))
    m = _dot(jnp.concatenate(parts, axis=-1), wout_ref[...])
    _residual_ffn_norm(x, m, ga_ref[0], fnorm_ref[...], scf_ref[0], shf_ref[0], x1_ref, hf_ref)

    @pl.when(t == pl.num_programs(1) - 1)
    def _():
        for h in range(n_heads):
            sf_ref[0, h] = st_scr[h].T


def _gla_layer(x, mod, s0, proj_consts, out_consts, tm):
    bsz, L, d = x.shape
    _, n_heads, dk, dv = s0.shape
    consts = proj_consts + out_consts
    row = lambda a: pl.BlockSpec((1, tm, a.shape[-1]), lambda b, t: (b, t, 0))
    sspec = pl.BlockSpec((1, n_heads, dk, dv), lambda b, t: (b, 0, 0, 0))
    return pl.pallas_call(
        functools.partial(_gla_layer_kernel, chunk=math.gcd(tm, GLA_CHUNK), q_scale=float(dk ** -0.5)),
        grid=(bsz, L // tm),
        in_specs=[row(x)] + [_mod_spec(mod, tm, d, c) for c in range(5)] + [sspec]
        + [_const_spec(a) for a in consts],
        out_specs=[row(x), row(x), sspec],
        out_shape=[jax.ShapeDtypeStruct((bsz, L, d), F32), jax.ShapeDtypeStruct((bsz, L, d), BF16),
                   jax.ShapeDtypeStruct(s0.shape, F32)],
        scratch_shapes=[pltpu.VMEM((n_heads, dv, dk), F32), pltpu.VMEM((tm, n_heads * dv), F32),
                        pltpu.VMEM((tm, n_heads * dk), F32), pltpu.VMEM((tm, n_heads * dk), F32)],
        compiler_params=_params("arbitrary", "arbitrary"),
        name="gla_layer",
    )(x, mod, mod, mod, mod, mod, s0, *consts)


def _gla_step_kernel(q_ref, k_ref, v_ref, g_ref, s_ref, o_ref, sn_ref, *, nb, dk, dv):
    eye = (lax.broadcasted_iota(jnp.int32, (dk, dk), 0) == lax.broadcasted_iota(jnp.int32, (dk, dk), 1))

    def column(row_vec):
        return jnp.sum(jnp.where(eye, row_vec, 0.0), axis=1, keepdims=True)

    for j in range(nb):
        for h in range(GLA_HEADS):
            ks = slice(h * dk, (h + 1) * dk)
            vs = slice(h * dv, (h + 1) * dv)
            decay = column(jnp.exp(g_ref[j:j + 1, ks]))
            kc = column(k_ref[j:j + 1, ks])
            qc = column(q_ref[j:j + 1, ks])
            s_new = s_ref[j, h] * decay + kc * v_ref[j:j + 1, vs].astype(F32)
            sn_ref[j, h] = s_new
            o_ref[j:j + 1, vs] = jnp.sum(qc * s_new, axis=0, keepdims=True)


def _gla_step(q, k, v, gk, s0):
    n, hk = q.shape
    hv = v.shape[-1]
    dk, dv = hk // GLA_HEADS, hv // GLA_HEADS
    nb = 8
    row = lambda w: pl.BlockSpec((nb, w), lambda i: (i, 0))
    sspec = pl.BlockSpec((nb, GLA_HEADS, dk, dv), lambda i: (i, 0, 0, 0))
    return pl.pallas_call(
        functools.partial(_gla_step_kernel, nb=nb, dk=dk, dv=dv),
        grid=(n // nb,),
        in_specs=[row(hk), row(hk), row(hv), row(hk), sspec],
        out_specs=[row(hv), sspec],
        out_shape=[jax.ShapeDtypeStruct((n, hv), F32),
                   jax.ShapeDtypeStruct((n, GLA_HEADS, dk, dv), F32)],
        compiler_params=_params("arbitrary"),
        name="gla_step",
    )(q, k, v, gk, s0)


def _residual_ffn_norm(x, m, ga, fnorm, scf, shf, x1_ref, hf_ref):
    x1 = x + ga * m
    x1_ref[0] = x1
    hf_ref[0] = (_rms(x1, fnorm) * (1.0 + scf) + shf).astype(BF16)


def _gla_out_kernel(o_ref, r_ref, x_ref, ga_ref, shf_ref, scf_ref, onorm_ref, wout_ref, fnorm_ref,
                    x1_ref, hf_ref, *, dv):
    parts = []
    for h in range(GLA_HEADS):
        vs = slice(h * dv, (h + 1) * dv)
        y = _rms(o_ref[0, :, vs], onorm_ref[...]) * _silu(r_ref[0, :, vs])
        parts.append(y.astype(BF16))
    m = _dot(jnp.concatenate(parts, axis=-1), wout_ref[...])
    _residual_ffn_norm(x_ref[0], m, ga_ref[0], fnorm_ref[...], scf_ref[0], shf_ref[0], x1_ref, hf_ref)


def _mla_out_kernel(o_ref, x_ref, ga_ref, shf_ref, scf_ref, wout_ref, fnorm_ref, x1_ref, hf_ref):
    m = _dot(o_ref[0], wout_ref[...])
    _residual_ffn_norm(x_ref[0], m, ga_ref[0], fnorm_ref[...], scf_ref[0], shf_ref[0], x1_ref, hf_ref)


def _mixer_out(kernel_fn, acts, x, mod, consts, tm, name):
    bsz, L, d = x.shape
    row = lambda a: pl.BlockSpec((1, tm, a.shape[-1]), lambda b, t: (b, t, 0))
    return pl.pallas_call(
        kernel_fn,
        grid=(bsz, L // tm),
        in_specs=[row(a) for a in acts] + [row(x), _mod_spec(mod, tm, d, 2), _mod_spec(mod, tm, d, 3),
                                           _mod_spec(mod, tm, d, 4)] + [_const_spec(a) for a in consts],
        out_specs=[row(x), row(x)],
        out_shape=[jax.ShapeDtypeStruct((bsz, L, d), F32), jax.ShapeDtypeStruct((bsz, L, d), BF16)],
        compiler_params=_params("arbitrary", "arbitrary"),
        name=name,
    )(*acts, x, mod, mod, mod, *consts)


def _moe_kernel(hf_ref, x1_ref, gf_ref, wr_ref, br_ref, wg_ref, wu_ref, wd_ref, y_ref, acc_scr, gate_scr, grp_scr):
    g = pl.program_id(2)
    hf = hf_ref[0]

    @pl.when(g == 0)
    def _():
        logits = _dot(hf, wr_ref[...]) + br_ref[...]
        lane = lax.broadcasted_iota(jnp.int32, logits.shape, 1)
        big = jnp.int32(1 << 20)
        is_grp = lane < MOE_GROUPS
        lg = jnp.where(is_grp, logits, -jnp.inf)
        mg = jnp.max(lg, axis=-1, keepdims=True)
        g_val = 1.0 / jnp.sum(jnp.exp(lg - mg), axis=-1, keepdims=True)
        g_idx = jnp.min(jnp.where(is_grp & (lg == mg), lane, big), axis=-1, keepdims=True)
        lo = MOE_GROUPS + g_idx * MOE_EXP_PER_GROUP
        in_grp = (lane >= lo) & (lane < lo + MOE_EXP_PER_GROUP)
        le = jnp.where(in_grp, logits, -jnp.inf)
        me = jnp.max(le, axis=-1, keepdims=True)
        pe = jnp.exp(le - me)
        p = pe / jnp.sum(pe, axis=-1, keepdims=True)
        p1 = jnp.max(p, axis=-1, keepdims=True)
        i1 = jnp.min(jnp.where(in_grp & (p == p1), lane, big), axis=-1, keepdims=True)
        rest = jnp.where(in_grp & (lane != i1), p, -1.0)
        p2 = jnp.max(rest, axis=-1, keepdims=True)
        i2 = jnp.min(jnp.where(rest == p2, lane, big), axis=-1, keepdims=True)
        norm = g_val / (p1 + p2)
        w1, w2 = p1 * norm, p2 * norm
        for i in range(MOE_EXP_PER_GROUP):
            col = jnp.where(i1 == lo + i, w1, 0.0) + jnp.where(i2 == lo + i, w2, 0.0)
            gate_scr[i] = jnp.broadcast_to(col, gate_scr.shape[1:])
        grp_scr[...] = jnp.broadcast_to(g_idx, grp_scr.shape)
        acc_scr[...] = jnp.zeros_like(acc_scr)

    mine = grp_scr[...] == g
    total = None
    for i in range(MOE_EXP_PER_GROUP):
        gate = jnp.where(mine, gate_scr[i], 0.0)
        hid = _silu(_dot(hf, wg_ref[i])) * _dot(hf, wu_ref[i])
        hid = hid * jnp.concatenate([gate] * (hid.shape[1] // LANES), axis=1)
        part = _dot(hid.astype(BF16), wd_ref[i])
        total = part if total is None else total + part
    acc_scr[...] += total

    @pl.when(g == pl.num_programs(2) - 1)
    def _():
        y_ref[0] = x1_ref[0] + gf_ref[0] * acc_scr[...]


def _moe(hf, x1, mod, w_route, b_route, w_gate, w_up, w_down, tm):
    bsz, L, d = x1.shape
    n_exp, _, d_ff = w_gate.shape
    row = pl.BlockSpec((1, tm, d), lambda b, t, e: (b, t, 0))
    if mod.shape[1] == 1:
        gspec = pl.BlockSpec((1, 1, d), lambda b, t, e: (b, 0, 5))
    else:
        gspec = pl.BlockSpec((1, tm, d), lambda b, t, e: (b, t, 5))
    return pl.pallas_call(
        _moe_kernel,
        grid=(bsz, L // tm, n_exp // MOE_EXP_PER_GROUP),
        in_specs=[row, row, gspec,
                  pl.BlockSpec(w_route.shape, lambda b, t, e: (0, 0)),
                  pl.BlockSpec(b_route.shape, lambda b, t, e: (0, 0)),
                  pl.BlockSpec((MOE_EXP_PER_GROUP, d, d_ff), lambda b, t, e: (e, 0, 0)),
                  pl.BlockSpec((MOE_EXP_PER_GROUP, d, d_ff), lambda b, t, e: (e, 0, 0)),
                  pl.BlockSpec((MOE_EXP_PER_GROUP, d_ff, d), lambda b, t, e: (e, 0, 0))],
        out_specs=row,
        out_shape=jax.ShapeDtypeStruct((bsz, L, d), F32),
        scratch_shapes=[pltpu.VMEM((tm, d), F32), pltpu.VMEM((MOE_EXP_PER_GROUP, tm, LANES), F32),
                        pltpu.VMEM((tm, LANES), jnp.int32)],
        compiler_params=_params("arbitrary", "arbitrary", "arbitrary"),
        name="moe",
    )(hf, x1, mod, w_route, b_route, w_gate, w_up, w_down)


def _kv_kernel(x_ref, innorm_ref, wc_ref, wr_ref, wrot_ref, latnorm_ref, cos_ref, sin_ref, ckv_ref, kr_ref):
    hn = _rms(x_ref[0], innorm_ref[...]).astype(BF16)
    ckv_ref[0] = _rms(_dot(hn, wc_ref[...]), latnorm_ref[...])
    kr_ref[0] = _dot(hn, wr_ref[...]) * cos_ref[...] + _dot(hn, wrot_ref[...]) * sin_ref[...]


def _rope_spec(table, tm):
    if table.shape[0] == 1:
        return pl.BlockSpec((1, LANES), lambda b, t: (0, 0))
    return pl.BlockSpec((tm, LANES), lambda b, t: (t, 0))


def _shared_kv(x, consts, cos, sin, tm):
    bsz, L, d = x.shape
    rank = consts[1].shape[1]
    row = lambda n: pl.BlockSpec((1, tm, n), lambda b, t: (b, t, 0))
    return pl.pallas_call(
        _kv_kernel,
        grid=(bsz, L // tm),
        in_specs=[row(d)] + [_const_spec(a) for a in consts] + [_rope_spec(cos, tm), _rope_spec(sin, tm)],
        out_specs=[row(rank), row(LANES)],
        out_shape=[jax.ShapeDtypeStruct((bsz, L, rank), F32), jax.ShapeDtypeStruct((bsz, L, LANES), F32)],
        compiler_params=_params("arbitrary", "arbitrary"),
        name="shared_kv",
    )(x, *consts, cos, sin)


def _mla_q_kernel(x_ref, sh_ref, sc_ref, mnorm_ref, wdq_ref, qlat_ref, wuq_ref, wuqrot_ref, qnorm_ref,
                  cos_ref, sin_ref, q_ref, *, scale):
    hm = _rms(x_ref[0], mnorm_ref[...]) * (1.0 + sc_ref[0]) + sh_ref[0]
    cq = _rms(_dot(hm.astype(BF16), wdq_ref[...]), qlat_ref[...]).astype(BF16)
    qp = _dot(cq, wuq_ref[...])
    qr = _dot(cq, wuqrot_ref[...])
    for h in range(MLA_HEADS):
        hs = slice(h * LANES, (h + 1) * LANES)
        a = qp[:, hs] * cos_ref[...] + qr[:, hs] * sin_ref[...]
        inv = lax.rsqrt(jnp.sum(a * a, axis=-1, keepdims=True) * (1.0 / MLA_QK) + NORM_EPS)
        q_ref[0, :, hs] = (a * inv * (qnorm_ref[...] * scale)).astype(q_ref.dtype)


def _mla_q(x, mod, consts, cos, sin, tm, out_dtype, scale):
    bsz, L, d = x.shape
    n = MLA_HEADS * LANES
    row = lambda w: pl.BlockSpec((1, tm, w), lambda b, t: (b, t, 0))
    return pl.pallas_call(
        functools.partial(_mla_q_kernel, scale=scale),
        grid=(bsz, L // tm),
        in_specs=[row(d), _mod_spec(mod, tm, d, 0), _mod_spec(mod, tm, d, 1)] + [_const_spec(a) for a in consts]
        + [_rope_spec(cos, tm), _rope_spec(sin, tm)],
        out_specs=row(n),
        out_shape=jax.ShapeDtypeStruct((bsz, L, n), out_dtype),
        compiler_params=_params("arbitrary", "arbitrary"),
        name="mla_q",
    )(x, mod, mod, *consts, cos, sin)


def _kv_up_kernel(ckv_ref, kr_ref, wuk_ref, knorm_ref, *rest, vt_tile):
    if vt_tile:
        wuvt_ref, k_ref, vt_ref = rest
    else:
        (k_ref,) = rest
    c = ckv_ref[0].astype(BF16)
    kn = _dot(c, wuk_ref[...])
    for h in range(MLA_HEADS):
        hs = slice(h * LANES, (h + 1) * LANES)
        a = kn[:, hs] + kr_ref[0]
        inv = lax.rsqrt(jnp.sum(a * a, axis=-1, keepdims=True) * (1.0 / MLA_QK) + NORM_EPS)
        k_ref[0, :, hs] = (a * inv * knorm_ref[...]).astype(k_ref.dtype)
    if vt_tile:
        for s in range(c.shape[0] // vt_tile):
            vt = _dot_nt(wuvt_ref[...], c[s * vt_tile:(s + 1) * vt_tile])
            vt_ref[0, s] = vt.astype(vt_ref.dtype)


def _kv_up(ckv, kr_slot, consts, tm, out_dtype, vt_tile=0):
    bsz, L, rank = ckv.shape
    n = MLA_HEADS * LANES
    row = lambda w: pl.BlockSpec((1, tm, w), lambda b, t: (b, t, 0))
    out_specs = [row(n)]
    out_shape = [jax.ShapeDtypeStruct((bsz, L, n), out_dtype)]
    if vt_tile:
        nv = consts[-1].shape[0]
        out_specs.append(pl.BlockSpec((1, tm // vt_tile, nv, vt_tile), lambda b, t: (b, t, 0, 0)))
        out_shape.append(jax.ShapeDtypeStruct((bsz, L // vt_tile, nv, vt_tile), out_dtype))
    return pl.pallas_call(
        functools.partial(_kv_up_kernel, vt_tile=vt_tile),
        grid=(bsz, L // tm),
        in_specs=[row(rank), row(LANES)] + [_const_spec(a) for a in consts],
        out_specs=out_specs,
        out_shape=out_shape,
        compiler_params=_params("arbitrary", "arbitrary"),
        name="kv_up",
    )(ckv, kr_slot, *consts)


def _flash_kernel(q_ref, k_ref, vt_ref, o_ref, *, tq, tk):
    qi = pl.program_id(2)
    sub = tk // tq
    n_full = (qi * tq) // tk

    def update(carry, c, masked):
        new = []
        off = pl.multiple_of(c * tk, tk)
        scores = [_dot_nt(k_ref[0, pl.ds(off, tk), hh * LANES:(hh + 1) * LANES],
                          q_ref[0, :, hh * LANES:(hh + 1) * LANES]) for hh in range(len(carry))]
        for hh, (m, acc) in enumerate(carry):
            s = scores[hh]
            if masked:
                s = jnp.where(key_minus_query <= qi * tq - c * tk, s, -jnp.inf)
            m_new = jnp.maximum(m, jnp.max(s, axis=0, keepdims=True))
            pb = jnp.exp2(s - m_new).astype(BF16)
            acc = acc * jnp.exp2(m - m_new)
            for t in range(sub):
                vt = vt_ref[0, c * sub + t, hh * MLA_V:(hh + 1) * MLA_V, :]
                acc = acc + _dot(jnp.concatenate([vt, ones], axis=0), pb[t * tq:(t + 1) * tq])
            new.append((m_new, acc))
        return tuple(new)

    ones = jnp.ones((16, tq), BF16)
    key_minus_query = (lax.broadcasted_iota(jnp.int32, (tk, tq), 0)
                       - lax.broadcasted_iota(jnp.int32, (tk, tq), 1))
    one = (jnp.full((1, tq), NEG_BIG, F32), jnp.zeros((MLA_V + 16, tq), F32))
    carry = lax.fori_loop(0, n_full, lambda c, cr: update(cr, c, False), (one,) * FLASH_HEADS)
    carry = update(carry, n_full, True)
    out_t = jnp.concatenate([acc[:MLA_V] / acc[MLA_V:MLA_V + 1] for (_, acc) in carry], axis=0)
    o_ref[0] = out_t.T.astype(o_ref.dtype)


def _flash(q, k, vt, tq):
    bsz, S, _ = q.shape
    nh = FLASH_HEADS
    qspec = pl.BlockSpec((1, tq, nh * LANES), lambda b, h, i: (b, i, h))
    kspec = pl.BlockSpec((1, S, nh * LANES), lambda b, h, i: (b, 0, h))
    vspec = pl.BlockSpec((1, S // tq, nh * MLA_V, tq), lambda b, h, i: (b, 0, h, 0))
    return pl.pallas_call(
        functools.partial(_flash_kernel, tq=tq, tk=min(FLASH_KEYS, S)),
        grid=(bsz, MLA_HEADS // nh, S // tq),
        in_specs=[qspec, kspec, vspec],
        out_specs=pl.BlockSpec((1, tq, nh * MLA_V), lambda b, h, i: (b, i, h)),
        out_shape=jax.ShapeDtypeStruct((bsz, S, MLA_HEADS * MLA_V), BF16),
        compiler_params=_params("arbitrary", "arbitrary", "arbitrary"),
        name="flash",
    )(q, k, vt)


def _paged_kernel(pt_ref, ckv_hbm, krt_hbm, q_ref, knew_ref, cnew_ref, knorm_ref, wukt_ref, wukt_slot_ref,
                  wuv_slot_ref, o_ref, m_scr, l_scr, acc_scr, lhs_scr, c_scr, cbuf, krbuf, sems, *, n_pages):
    b, j = pl.program_id(0), pl.program_id(1)
    steps = pl.num_programs(1)
    step = b * steps + j
    last_step = pl.num_programs(0) * steps - 1
    slot = step % 2
    nh = MLA_HEADS
    n_kn = nh * MLA_NOPE
    page = ckv_hbm.shape[1]
    tile = 2 * page
    head = lax.broadcasted_iota(jnp.int32, (nh, 1), 0)

    def page_copies(pid, p, to_slot):
        return (pltpu.make_async_copy(ckv_hbm.at[pid], cbuf.at[to_slot, pl.ds(p * page, page), :],
                                      sems.at[to_slot, 0]),
                pltpu.make_async_copy(krt_hbm.at[pid], krbuf.at[to_slot, :, pl.ds(p * page, page)],
                                      sems.at[to_slot, 1]))

    def start_pages(for_step, to_slot, positions):
        fb, fj = for_step // steps, for_step % steps
        for p in positions:
            for cp in page_copies(pt_ref[fb, fj * n_pages + p], p, to_slot):
                cp.start()

    def wait_pages(in_slot):
        for p in range(n_pages):
            for cp in page_copies(0, p, in_slot):
                cp.wait()

    @pl.when(step == 0)
    def _():
        start_pages(step, slot, range(n_pages))

    qg = q_ref[0] * knorm_ref[...]

    @pl.when(j == 0)
    def _():
        m_scr[...] = jnp.full_like(m_scr, NEG_BIG)
        l_scr[...] = jnp.zeros_like(l_scr)
        acc_scr[...] = jnp.zeros_like(acc_scr)
        qb = qg.astype(BF16)
        qabs = jnp.zeros((nh, lhs_scr.shape[1]), F32)
        for h in range(nh):
            qabs = qabs + jnp.where(head == h, _dot(qb, wukt_slot_ref[h]), 0.0)
        lhs_scr[:n_kn, :] = wukt_ref[...]
        lhs_scr[n_kn:, :] = jnp.concatenate([qabs, jnp.zeros_like(qabs)], axis=0).astype(BF16)

    wait_pages(slot)
    nxt = jnp.minimum(step + 1, last_step)
    q_rope = qg[:, MLA_NOPE:MLA_QK].astype(BF16)
    scores = []
    for i in range(n_pages // 2):
        start_pages(nxt, 1 - slot, (2 * i, 2 * i + 1))
        c2 = cbuf[slot, i * tile:(i + 1) * tile, :].astype(BF16)
        c_scr[i * tile:(i + 1) * tile, :] = c2
        both = _dot_nt(lhs_scr[...], c2)
        knt = both[:n_kn]
        ss = jnp.sum((knt * knt).reshape(nh, MLA_NOPE, tile), axis=1)
        kr2 = krbuf[slot, :, i * tile:(i + 1) * tile]
        ss = ss + jnp.sum(kr2 * kr2, axis=0, keepdims=True)
        num = both[n_kn:n_kn + nh] + _dot(q_rope, kr2.astype(BF16))
        scores.append(num * lax.rsqrt(ss * (1.0 / MLA_QK) + NORM_EPS))
    s = jnp.concatenate(scores, axis=1)
    m = m_scr[...]
    m_new = jnp.maximum(m, jnp.max(s, axis=-1, keepdims=True))
    pr = jnp.exp(s - m_new)
    corr = jnp.exp(m - m_new)
    l = l_scr[...] * corr + jnp.sum(pr, axis=-1, keepdims=True)
    pb = pr.astype(BF16)
    pv = [None, None]
    for i in range(n_pages // 2):
        part = _dot(pb[:, i * tile:(i + 1) * tile], c_scr[i * tile:(i + 1) * tile, :])
        pv[i % 2] = part if pv[i % 2] is None else pv[i % 2] + part
    acc = acc_scr[...] * corr + (pv[0] + pv[1])
    m_scr[...] = m_new
    l_scr[...] = l
    acc_scr[...] = acc

    @pl.when(j == steps - 1)
    def _():
        s_new = jnp.sum(q_ref[0] * knew_ref[0], axis=-1, keepdims=True)
        m_fin = jnp.maximum(m_new, s_new)
        corr_fin = jnp.exp(m_new - m_fin)
        p_new = jnp.exp(s_new - m_fin)
        l_fin = l * corr_fin + p_new
        o_lat = ((acc * corr_fin + p_new * cnew_ref[0]) / l_fin).astype(BF16)
        out = jnp.zeros((nh, LANES), F32)
        for h in range(nh):
            out = out + jnp.where(head == h, _dot(o_lat, wuv_slot_ref[h]), 0.0)
        o_ref[0] = out

    @pl.when(step == last_step)
    def _():
        wait_pages(1 - slot)


def _paged_attention(q, k_new, c_new, cache_ckv, cache_krope_t, page_table, consts, n_pages):
    nb, nh, _ = q.shape
    _, page, rank = cache_ckv.shape
    rope = cache_krope_t.shape[1]
    steps = page_table.shape[1] // n_pages
    assert n_pages % 2 == 0 and steps * n_pages == page_table.shape[1]
    per_b = lambda a: pl.BlockSpec((1,) + a.shape[1:], lambda b, j, pt: (b,) + (0,) * (a.ndim - 1))
    const = lambda a: pl.BlockSpec(a.shape, lambda b, j, pt: (0,) * a.ndim)
    hbm = pl.BlockSpec(memory_space=pl.ANY)
    grid_spec = pltpu.PrefetchScalarGridSpec(
        num_scalar_prefetch=1,
        grid=(nb, steps),
        in_specs=[hbm, hbm, per_b(q), per_b(k_new), per_b(c_new)] + [const(a) for a in consts],
        out_specs=pl.BlockSpec((1, nh, LANES), lambda b, j, pt: (b, 0, 0)),
        scratch_shapes=[pltpu.VMEM((nh, 1), F32), pltpu.VMEM((nh, 1), F32), pltpu.VMEM((nh, rank), F32),
                        pltpu.VMEM((nh * MLA_NOPE + 2 * nh, rank), BF16),
                        pltpu.VMEM((n_pages * page, rank), BF16),
                        pltpu.VMEM((2, n_pages * page, rank), F32),
                        pltpu.VMEM((2, rope, n_pages * page), F32),
                        pltpu.SemaphoreType.DMA((2, 2))],
    )
    return pl.pallas_call(
        functools.partial(_paged_kernel, n_pages=n_pages),
        grid_spec=grid_spec,
        out_shape=jax.ShapeDtypeStruct((nb, nh, LANES), F32),
        compiler_params=_params("arbitrary", "arbitrary"),
        name="paged_attn",
    )(page_table, cache_ckv, cache_krope_t, q, k_new, c_new, *consts)


def _rotate_half_cols(w):
    half = w.shape[-1] // 2
    return jnp.concatenate([-w[..., half:], w[..., :half]], axis=-1)


def _slot(nope, rope):
    ref = nope if nope is not None else rope
    lead = ref.shape[:-1]
    z = lambda n: jnp.zeros(lead + (n,), ref.dtype)
    a = nope if nope is not None else z(MLA_NOPE)
    b = rope if rope is not None else z(MLA_ROPE)
    return jnp.concatenate([a, b, z(LANES - MLA_QK)], axis=-1)


def _rope_tables(pos):
    inv = ROPE_THETA ** (-jnp.arange(0, MLA_ROPE, 2, dtype=F32) / MLA_ROPE)
    ang = pos.astype(F32)[:, None] * inv[None, :]
    ang = jnp.concatenate([ang, ang], axis=-1)
    n = pos.shape[0]
    cos = _slot(jnp.ones((n, MLA_NOPE), F32), jnp.cos(ang))
    sin = _slot(jnp.zeros((n, MLA_NOPE), F32), jnp.sin(ang))
    return cos, sin


def _prepare(w):
    d = w['mix_norm'].shape[-1]
    p = {}
    p['route_w'] = [jnp.pad(jnp.concatenate([w['moe_w_group'][l], w['moe_w_router'][l]], axis=-1),
                            ((0, 0), (0, LANES - MOE_GROUPS - MOE_EXPERTS))).astype(BF16) for l in range(2)]
    p['route_b'] = [jnp.pad(jnp.concatenate([w['moe_b_group'][l], w['moe_b_router'][l]], axis=-1),
                            (0, LANES - MOE_GROUPS - MOE_EXPERTS)).reshape(1, LANES) for l in range(2)]
    p['moe_gate'] = w['moe_w_gate'].astype(BF16)
    p['moe_up'] = w['moe_w_up'].astype(BF16)
    p['moe_down'] = w['moe_w_down'].astype(BF16)
    p['gla_out'] = [w['gla_out_norm'][0].reshape(1, -1), w['gla_w_out'][0].astype(BF16),
                    w['ffn_norm'][0].reshape(1, d)]
    rank = w['kv_lat_norm'].shape[0]
    w_r = w['kv_w_dkv'][:, rank:]
    p['kv'] = [w['kv_in_norm'].reshape(1, d), w['kv_w_dkv'][:, :rank].astype(BF16),
               _slot(None, w_r).astype(BF16), _slot(None, _rotate_half_cols(w_r)).astype(BF16),
               w['kv_lat_norm'].reshape(1, rank)]
    q_rank = w['mla_w_dq'].shape[-1]
    wuq = w['mla_w_uq'][0].reshape(q_rank, MLA_HEADS, MLA_QK)
    wuq_n, wuq_r = wuq[..., :MLA_NOPE], wuq[..., MLA_NOPE:]
    p['mla_q'] = [w['mix_norm'][1].reshape(1, d), w['mla_w_dq'][0].astype(BF16),
                  w['mla_q_lat_norm'][0].reshape(1, q_rank),
                  _slot(wuq_n, wuq_r).reshape(q_rank, -1).astype(BF16),
                  _slot(None, _rotate_half_cols(wuq_r)).reshape(q_rank, -1).astype(BF16),
                  jnp.pad(w['mla_q_norm'][0], (0, LANES - MLA_QK)).reshape(1, LANES)]
    knorm = jnp.pad(w['kv_k_norm'], (0, LANES - MLA_QK)).reshape(1, LANES)
    wuk, wuv = w['kv_w_uk'], w['kv_w_uv']
    p['kv_up'] = [_slot(wuk, None).reshape(rank, -1).astype(BF16), knorm]
    p['wuv_t'] = wuv.reshape(rank, -1).T.astype(BF16)
    wukt = wuk.transpose(1, 2, 0)
    p['paged'] = [knorm, wukt.reshape(MLA_HEADS * MLA_NOPE, rank).astype(BF16),
                  jnp.pad(wukt, ((0, 0), (0, LANES - MLA_NOPE), (0, 0))).astype(BF16),
                  jnp.pad(wuv.transpose(1, 0, 2), ((0, 0), (0, 0), (0, LANES - MLA_V))).astype(BF16)]
    wo = w['mla_w_out'][0].reshape(MLA_HEADS, MLA_V, d)
    p['mla_out_slot'] = [jnp.pad(wo, ((0, 0), (0, LANES - MLA_V), (0, 0))).reshape(MLA_HEADS * LANES, d)
                         .astype(BF16), w['ffn_norm'][1].reshape(1, d)]
    p['mla_out'] = [w['mla_w_out'][0].astype(BF16), w['ffn_norm'][1].reshape(1, d)]
    return p


def _moe_layer(l, hf, x1, mod, p):
    return _moe(hf, x1, mod, p['route_w'][l], p['route_b'][l], p['moe_gate'][l], p['moe_up'][l],
                p['moe_down'][l], min(MOE_ROWS, x1.shape[1]))


def _trunk(x, mods, pos, gla_s0, paged, w, p, tm):
    cos, sin = _rope_tables(pos)
    gla_w = (w['mix_norm'][0], w['gla_w_in'][0], w['gla_w_gate_up'][0], w['gla_b_gate'][0])
    if paged is None:
        x1, hf, s_fin = _gla_layer(x, mods[0], gla_s0, _gla_proj_consts(*gla_w), p['gla_out'], tm)
    else:
        q, k, v, r, gk = _gla_proj(x, mods[0], *gla_w, tm)
        o, s_fin = _gla_step(q[0], k[0], v[0], gk[0], gla_s0)
        x1, hf = _mixer_out(functools.partial(_gla_out_kernel, dv=v.shape[-1] // GLA_HEADS), [o[None], r], x,
                            mods[0], p['gla_out'], tm, "gla_out")
    x2 = _moe_layer(0, hf, x1, mods[0], p)
    ckv, kr_slot = _shared_kv(x2, p['kv'], cos, sin, tm)
    if paged is None:
        qh = _mla_q(x2, mods[1], p['mla_q'], cos, sin, tm, BF16, MLA_QK ** -0.5 * math.log2(math.e))
        tq = min(FLASH_T, x.shape[1])
        kh, vt = _kv_up(ckv, kr_slot, p['kv_up'] + [p['wuv_t']], tm, BF16, vt_tile=tq)
        o = _flash(qh, kh, vt, tq)
        out_consts = p['mla_out']
    else:
        n = x.shape[1]
        qh = _mla_q(x2, mods[1], p['mla_q'], cos, sin, tm, F32, MLA_QK ** -0.5)
        (kh,) = _kv_up(ckv, kr_slot, p['kv_up'], tm, F32)
        o = _paged_attention(qh.reshape(n, MLA_HEADS, LANES), kh.reshape(n, MLA_HEADS, LANES),
                             ckv.reshape(n, 1, -1), paged[0], jnp.swapaxes(paged[1], 1, 2), paged[2], p['paged'],
                             min(32, paged[2].shape[1]))
        o = o.reshape(1, n, MLA_HEADS * LANES).astype(BF16)
        out_consts = p['mla_out_slot']
    x3, hf = _mixer_out(_mla_out_kernel, [o], x2, mods[1], out_consts, tm, "mla_out")
    y = _moe_layer(1, hf, x3, mods[1], p)
    return y, s_fin, ckv, kr_slot[..., MLA_NOPE:MLA_QK]


def _forward(x_prompt, x_sample, c_prompt, c_sample, state_gla, cache_ckv, cache_krope, page_table, w, past_len):
    bsz, seq, d = x_prompt.shape
    nd, dl, _ = x_sample.shape
    assert dl == 1, "decode path handles one new token per sequence"
    p = _prepare(w)
    mod = _ada_mod(jnp.concatenate([c_prompt, c_sample], axis=0), w['ada_w'], w['ada_b'])
    mods_p = [mod[l, :bsz].reshape(bsz, 1, -1) for l in range(2)]
    mods_s = [mod[l, bsz:].reshape(1, nd, -1) for l in range(2)]
    s0_p = jnp.zeros((bsz,) + state_gla.shape[2:], state_gla.dtype)
    y_p, s_p, ckv_p, kr_p = _trunk(x_prompt, mods_p, jnp.arange(seq), s0_p, None, w, p, min(512, seq))
    y_s, s_s, ckv_s, kr_s = _trunk(x_sample.reshape(1, nd, d), mods_s, jnp.full((1,), past_len), state_gla[0],
                                   (cache_ckv, cache_krope, page_table), w, p, nd)
    return (y_p, y_s.reshape(nd, 1, d), s_p[None], s_s[None], ckv_p, kr_p,
            ckv_s.reshape(nd, 1, -1), kr_s.reshape(nd, 1, -1))


def kernel(x_prompt, x_sample, c_prompt, c_sample, state_gla, cache_ckv, cache_krope, page_table, ada_w, ada_b, mix_norm, ffn_norm, gla_w_in, gla_w_gate_up, gla_b_gate, gla_out_norm, gla_w_out, mla_w_dq, mla_q_lat_norm, mla_w_uq, mla_q_norm, mla_w_out, kv_in_norm, kv_w_dkv, kv_lat_norm, kv_w_uk, kv_w_uv, kv_k_norm, moe_w_group, moe_b_group, moe_w_router, moe_b_router, moe_w_gate, moe_w_up, moe_w_down):
    w = dict(ada_w=ada_w, ada_b=ada_b, mix_norm=mix_norm, ffn_norm=ffn_norm,
             gla_w_in=gla_w_in, gla_w_gate_up=gla_w_gate_up, gla_b_gate=gla_b_gate,
             gla_out_norm=gla_out_norm, gla_w_out=gla_w_out,
             mla_w_dq=mla_w_dq, mla_q_lat_norm=mla_q_lat_norm, mla_w_uq=mla_w_uq,
             mla_q_norm=mla_q_norm, mla_w_out=mla_w_out,
             kv_in_norm=kv_in_norm, kv_w_dkv=kv_w_dkv, kv_lat_norm=kv_lat_norm,
             kv_w_uk=kv_w_uk, kv_w_uv=kv_w_uv, kv_k_norm=kv_k_norm,
             moe_w_group=moe_w_group, moe_b_group=moe_b_group, moe_w_router=moe_w_router,
             moe_b_router=moe_b_router, moe_w_gate=moe_w_gate, moe_w_up=moe_w_up, moe_w_down=moe_w_down)
    past_len = page_table.shape[1] * cache_ckv.shape[1]
    return _forward(x_prompt, x_sample, c_prompt, c_sample, state_gla, cache_ckv, cache_krope, page_table, w,
                    past_len)
```

```python
import functools
import math

import jax
import jax.numpy as jnp
from jax import lax
from jax.experimental import pallas as pl
from jax.experimental.pallas import tpu as pltpu

F32 = jnp.float32
BF16 = jnp.bfloat16

NORM_EPS = 1e-6
GLA_HEADS = 4
GLA_GATE_NORM = 16.0
GLA_CHUNK = 256
GLA_SAFE_EXPONENT = 80.0
MLA_HEADS = 8
MLA_NOPE = 64
MLA_ROPE = 32
MLA_QK = MLA_NOPE + MLA_ROPE
MLA_V = 64
ROPE_THETA = 10000.0
MOE_GROUPS = 4
MOE_EXP_PER_GROUP = 4
MOE_EXPERTS = MOE_GROUPS * MOE_EXP_PER_GROUP
LANES = 128
SUBLANES = 8
NEG_BIG = -1e30
FLASH_T = 256
FLASH_KEYS = 1024
MOE_ROWS = 1024
PAGES_PER_STEP = 64
FLASH_HEADS = 4
VMEM_LIMIT = 52 * 1024 * 1024


def _params(*sem):
    return pltpu.CompilerParams(dimension_semantics=sem, vmem_limit_bytes=VMEM_LIMIT)


def _dot(a, b):
    return jnp.dot(a, b, preferred_element_type=F32)


def _dot_nt(a, b):
    return lax.dot_general(a, b, (((1,), (1,)), ((), ())), preferred_element_type=F32)


def _dot_tn(a, b):
    return lax.dot_general(a, b, (((0,), (0,)), ((), ())), preferred_element_type=F32)


def _rms(x, g):
    return x * lax.rsqrt(jnp.mean(x * x, axis=-1, keepdims=True) + NORM_EPS) * g


def _silu(x):
    return x / (1.0 + jnp.exp(-x))


def _split_bf16(x):
    hi = x.astype(BF16)
    lo = (x - hi.astype(F32)).astype(BF16)
    return hi, lo


def _ada_kernel(c_ref, w_ref, b_ref, o_ref):
    a = _silu(c_ref[...]).astype(BF16)
    o_ref[0] = _dot(a, w_ref[0].astype(BF16)) + b_ref[0]


def _ada_mod(c, ada_w, ada_b):
    depth, d, n = ada_w.shape
    r = c.shape[0]
    tn = 1536
    return pl.pallas_call(
        _ada_kernel,
        grid=(depth, n // tn),
        in_specs=[
            pl.BlockSpec((r, d), lambda l, j: (0, 0)),
            pl.BlockSpec((1, d, tn), lambda l, j: (l, 0, j)),
            pl.BlockSpec((1, 1, tn), lambda l, j: (l, 0, j)),
        ],
        out_specs=pl.BlockSpec((1, r, tn), lambda l, j: (l, 0, j)),
        out_shape=jax.ShapeDtypeStruct((depth, r, n), F32),
        compiler_params=_params("arbitrary", "arbitrary"),
        name="ada_mod",
    )(c, ada_w, ada_b.reshape(depth, 1, n))


def _mod_spec(mod, tm, d, chunk):
    if mod.shape[1] == 1:
        return pl.BlockSpec((1, 1, d), lambda b, t: (b, 0, chunk))
    return pl.BlockSpec((1, tm, d), lambda b, t: (b, t, chunk))


def _const_spec(arr):
    nd = arr.ndim
    return pl.BlockSpec(arr.shape, lambda b, t: (0,) * nd)


def _gla_proj_kernel(x_ref, sh_ref, sc_ref, g_ref, wq_ref, wk_ref, wv_ref, wr_ref, wg_ref, wgu_ref, bg_ref,
                     q_ref, k_ref, v_ref, r_ref, gk_ref, *, q_scale):
    hm = _rms(x_ref[0], g_ref[...]) * (1.0 + sc_ref[0]) + sh_ref[0]
    hb = hm.astype(BF16)
    q_ref[0] = _dot(hb, wq_ref[...]) * q_scale
    k_ref[0] = _dot(hb, wk_ref[...])
    v_ref[0] = _dot(hb, wv_ref[...]).astype(BF16)
    r_ref[0] = _dot(hb, wr_ref[...])
    g_low = _dot(hb, wg_ref[...])
    z = _dot(g_low.astype(BF16), wgu_ref[...]) + bg_ref[...]
    log_sig = jnp.minimum(z, 0.0) - jnp.log(1.0 + jnp.exp(-jnp.abs(z)))
    gk_ref[0] = log_sig * (1.0 / GLA_GATE_NORM)


def _gla_proj_consts(norm_g, w_in, w_gate_up, b_gate):
    hk = w_gate_up.shape[1]
    hv = (w_in.shape[1] - 2 * hk - w_gate_up.shape[0]) // 2
    cuts = [0, hk, 2 * hk, 2 * hk + hv, 2 * hk + 2 * hv, w_in.shape[1]]
    pieces = [w_in[:, a:b].astype(BF16) for a, b in zip(cuts[:-1], cuts[1:])]
    return [norm_g.reshape(1, -1)] + pieces + [w_gate_up.astype(BF16), b_gate.reshape(1, hk)]


def _gla_proj(x, mod, norm_g, w_in, w_gate_up, b_gate, tm):
    bsz, L, d = x.shape
    consts = _gla_proj_consts(norm_g, w_in, w_gate_up, b_gate)
    hk, hv = consts[1].shape[1], consts[3].shape[1]
    row = lambda n: pl.BlockSpec((1, tm, n), lambda b, t: (b, t, 0))
    return pl.pallas_call(
        functools.partial(_gla_proj_kernel, q_scale=float((hk // GLA_HEADS) ** -0.5)),
        grid=(bsz, L // tm),
        in_specs=[row(d), _mod_spec(mod, tm, d, 0), _mod_spec(mod, tm, d, 1)] + [_const_spec(a) for a in consts],
        out_specs=[row(hk), row(hk), row(hv), row(hv), row(hk)],
        out_shape=[
            jax.ShapeDtypeStruct((bsz, L, hk), F32),
            jax.ShapeDtypeStruct((bsz, L, hk), F32),
            jax.ShapeDtypeStruct((bsz, L, hv), BF16),
            jax.ShapeDtypeStruct((bsz, L, hv), F32),
            jax.ShapeDtypeStruct((bsz, L, hk), F32),
        ],
        compiler_params=_params("arbitrary", "arbitrary"),
        name="gla_proj",
    )(x, mod, mod, *consts)


def _gla_layer_kernel(x_ref, sh_ref, sc_ref, ga_ref, shf_ref, scf_ref, s0_ref, mnorm_ref, wq_ref, wk_ref, wv_ref,
                      wr_ref, wg_ref, wgu_ref, bg_ref, onorm_ref, wout_ref, fnorm_ref,
                      x1_ref, hf_ref, sf_ref, st_scr, o_scr, b_scr, k_scr, *, chunk, q_scale):
    t = pl.program_id(1)
    n_heads, dv, dk = st_scr.shape

    @pl.when(t == 0)
    def _():
        for h in range(n_heads):
            st_scr[h] = s0_ref[0, h].T

    x = x_ref[0]
    tm = x.shape[0]
    n_chunks = tm // chunk
    hb = (_rms(x, mnorm_ref[...]) * (1.0 + sc_ref[0]) + sh_ref[0]).astype(BF16)
    q = _dot(hb, wq_ref[...]) * q_scale
    k = _dot(hb, wk_ref[...])
    v = _dot(hb, wv_ref[...]).astype(BF16)
    r = _dot(hb, wr_ref[...])
    z = _dot(_dot(hb, wg_ref[...]).astype(BF16), wgu_ref[...]) + bg_ref[...]
    gk = (jnp.minimum(z, 0.0) - jnp.log(1.0 + jnp.exp(-jnp.abs(z)))) * (1.0 / GLA_GATE_NORM)
    tri = jnp.where(lax.broadcasted_iota(jnp.int32, (tm, tm), 0) >= lax.broadcasted_iota(jnp.int32, (tm, tm), 1),
                    1.0, 0.0).astype(BF16)
    g_hi, g_lo = _split_bf16(gk)
    cum = _dot(tri, g_hi) + _dot(tri, g_lo)
    causal = (lax.broadcasted_iota(jnp.int32, (chunk, chunk), 0)
              >= lax.broadcasted_iota(jnp.int32, (chunk, chunk), 1))

    def local_gate(c, cols):
        b = cum[c * chunk:(c + 1) * chunk, cols]
        return b - cum[c * chunk - 1:c * chunk, cols] if c else b

    total_decay = jnp.concatenate([local_gate(c, slice(None))[chunk - 1:chunk] for c in range(n_chunks)], axis=0)
    factored_ok = jnp.max(-total_decay) <= GLA_SAFE_EXPONENT

    @pl.when(factored_ok)
    def _():
        for h in range(n_heads):
            ks = slice(h * dk, (h + 1) * dk)
            vs = slice(h * dv, (h + 1) * dv)
            st = st_scr[h]
            for c in range(n_chunks):
                rows = slice(c * chunk, (c + 1) * chunk)
                b = local_gate(c, ks)
                qt = (q[rows, ks] * jnp.exp(b)).astype(BF16)
                kt = (k[rows, ks] * jnp.exp(-b)).astype(BF16)
                vc = v[rows, vs]
                attn = jnp.where(causal, _dot_nt(qt, kt), 0.0).astype(BF16)
                o_scr[rows, vs] = _dot(attn, vc) + _dot_nt(qt, st.astype(BF16))
                st = (st + _dot_tn(vc, kt)) * jnp.exp(b[chunk - 1:chunk, :])
            st_scr[h] = st

    @pl.when(jnp.logical_not(factored_ok))
    def _():
        k_scr[...] = k
        lane = lax.broadcasted_iota(jnp.int32, (chunk, chunk), 1)
        for h in range(n_heads):
            ks = slice(h * dk, (h + 1) * dk)
            vs = slice(h * dv, (h + 1) * dv)
            st = st_scr[h]
            for c in range(n_chunks):
                rows = slice(c * chunk, (c + 1) * chunk)
                b = local_gate(c, ks)
                b_scr[rows, ks] = b
                qc = q[rows, ks]

                def key_rows(grp, attn, c=c, ks=ks, b=b, qc=qc):
                    first = pl.multiple_of(c * chunk + grp * SUBLANES, SUBLANES)
                    b_grp = b_scr[pl.ds(first, SUBLANES), ks]
                    k_grp = k_scr[pl.ds(first, SUBLANES), ks]
                    for i in range(SUBLANES):
                        decay = jnp.exp(jnp.minimum(b - b_grp[i:i + 1], 0.0))
                        col = jnp.sum(qc * k_grp[i:i + 1] * decay, axis=-1, keepdims=True)
                        attn = jnp.where(lane == grp * SUBLANES + i, col, attn)
                    return attn

                attn = lax.fori_loop(0, chunk // SUBLANES, key_rows, jnp.zeros((chunk, chunk), F32))
                attn = jnp.where(causal, attn, 0.0).astype(BF16)
                vc = v[rows, vs]
                b_end = b[chunk - 1:chunk, :]
                qt = (qc * jnp.exp(b)).astype(BF16)
                o_scr[rows, vs] = _dot(attn, vc) + _dot_nt(qt, st.astype(BF16))
                kd = (k[rows, ks] * jnp.exp(b_end - b)).astype(BF16)
                st = st * jnp.exp(b_end) + _dot_tn(vc, kd)
            st_scr[h] = st

    parts = []
    for h in range(n_heads):
        vs = slice(h * dv, (h + 1) * dv)
        parts.append((_rms(o_scr[:, vs], onorm_ref[...]) * _silu(r[:, vs])).astype(BF16))
    m = _dot(jnp.concatenate(parts, axis=-1), wout_ref[...])
    _residual_ffn_norm(x, m, ga_ref[0], fnorm_ref[...], scf_ref[0], shf_ref[0], x1_ref, hf_ref)

    @pl.when(t == pl.num_programs(1) - 1)
    def _():
        for h in range(n_heads):
            sf_ref[0, h] = st_scr[h].T


def _gla_layer(x, mod, s0, proj_consts, out_consts, tm):
    bsz, L, d = x.shape
    _, n_heads, dk, dv = s0.shape
    consts = proj_consts + out_consts
    row = lambda a: pl.BlockSpec((1, tm, a.shape[-1]), lambda b, t: (b, t, 0))
    sspec = pl.BlockSpec((1, n_heads, dk, dv), lambda b, t: (b, 0, 0, 0))
    return pl.pallas_call(
        functools.partial(_gla_layer_kernel, chunk=math.gcd(tm, GLA_CHUNK), q_scale=float(dk ** -0.5)),
        grid=(bsz, L // tm),
        in_specs=[row(x)] + [_mod_spec(mod, tm, d, c) for c in range(5)] + [sspec]
        + [_const_spec(a) for a in consts],
        out_specs=[row(x), row(x), sspec],
        out_shape=[jax.ShapeDtypeStruct((bsz, L, d), F32), jax.ShapeDtypeStruct((bsz, L, d), BF16),
                   jax.ShapeDtypeStruct(s0.shape, F32)],
        scratch_shapes=[pltpu.VMEM((n_heads, dv, dk), F32), pltpu.VMEM((tm, n_heads * dv), F32),
                        pltpu.VMEM((tm, n_heads * dk), F32), pltpu.VMEM((tm, n_heads * dk), F32)],
        compiler_params=_params("arbitrary", "arbitrary"),
        name="gla_layer",
    )(x, mod, mod, mod, mod, mod, s0, *consts)


def _gla_step_kernel(q_ref, k_ref, v_ref, g_ref, s_ref, o_ref, sn_ref, *, nb, dk, dv):
    eye = (lax.broadcasted_iota(jnp.int32, (dk, dk), 0) == lax.broadcasted_iota(jnp.int32, (dk, dk), 1))

    def column(row_vec):
        return jnp.sum(jnp.where(eye, row_vec, 0.0), axis=1, keepdims=True)

    for j in range(nb):
        for h in range(GLA_HEADS):
            ks = slice(h * dk, (h + 1) * dk)
            vs = slice(h * dv, (h + 1) * dv)
            decay = column(jnp.exp(g_ref[j:j + 1, ks]))
            kc = column(k_ref[j:j + 1, ks])
            qc = column(q_ref[j:j + 1, ks])
            s_new = s_ref[j, h] * decay + kc * v_ref[j:j + 1, vs].astype(F32)
            sn_ref[j, h] = s_new
            o_ref[j:j + 1, vs] = jnp.sum(qc * s_new, axis=0, keepdims=True)


def _gla_step(q, k, v, gk, s0):
    n, hk = q.shape
    hv = v.shape[-1]
    dk, dv = hk // GLA_HEADS, hv // GLA_HEADS
    nb = 8
    row = lambda w: pl.BlockSpec((nb, w), lambda i: (i, 0))
    sspec = pl.BlockSpec((nb, GLA_HEADS, dk, dv), lambda i: (i, 0, 0, 0))
    return pl.pallas_call(
        functools.partial(_gla_step_kernel, nb=nb, dk=dk, dv=dv),
        grid=(n // nb,),
        in_specs=[row(hk), row(hk), row(hv), row(hk), sspec],
        out_specs=[row(hv), sspec],
        out_shape=[jax.ShapeDtypeStruct((n, hv), F32),
                   jax.ShapeDtypeStruct((n, GLA_HEADS, dk, dv), F32)],
        compiler_params=_params("arbitrary"),
        name="gla_step",
    )(q, k, v, gk, s0)


def _residual_ffn_norm(x, m, ga, fnorm, scf, shf, x1_ref, hf_ref):
    x1 = x + ga * m
    x1_ref[0] = x1
    hf_ref[0] = (_rms(x1, fnorm) * (1.0 + scf) + shf).astype(BF16)


def _gla_out_kernel(o_ref, r_ref, x_ref, ga_ref, shf_ref, scf_ref, onorm_ref, wout_ref, fnorm_ref,
                    x1_ref, hf_ref, *, dv):
    parts = []
    for h in range(GLA_HEADS):
        vs = slice(h * dv, (h + 1) * dv)
        y = _rms(o_ref[0, :, vs], onorm_ref[...]) * _silu(r_ref[0, :, vs])
        parts.append(y.astype(BF16))
    m = _dot(jnp.concatenate(parts, axis=-1), wout_ref[...])
    _residual_ffn_norm(x_ref[0], m, ga_ref[0], fnorm_ref[...], scf_ref[0], shf_ref[0], x1_ref, hf_ref)


def _mla_out_kernel(o_ref, x_ref, ga_ref, shf_ref, scf_ref, wout_ref, fnorm_ref, x1_ref, hf_ref):
    m = _dot(o_ref[0], wout_ref[...])
    _residual_ffn_norm(x_ref[0], m, ga_ref[0], fnorm_ref[...], scf_ref[0], shf_ref[0], x1_ref, hf_ref)


def _mixer_out(kernel_fn, acts, x, mod, consts, tm, name):
    bsz, L, d = x.shape
    row = lambda a: pl.BlockSpec((1, tm, a.shape[-1]), lambda b, t: (b, t, 0))
    return pl.pallas_call(
        kernel_fn,
        grid=(bsz, L // tm),
        in_specs=[row(a) for a in acts] + [row(x), _mod_spec(mod, tm, d, 2), _mod_spec(mod, tm, d, 3),
                                           _mod_spec(mod, tm, d, 4)] + [_const_spec(a) for a in consts],
        out_specs=[row(x), row(x)],
        out_shape=[jax.ShapeDtypeStruct((bsz, L, d), F32), jax.ShapeDtypeStruct((bsz, L, d), BF16)],
        compiler_params=_params("arbitrary", "arbitrary"),
        name=name,
    )(*acts, x, mod, mod, mod, *consts)


def _moe_kernel(hf_ref, x1_ref, gf_ref, wr_ref, br_ref, wg_ref, wu_ref, wd_ref, y_ref, acc_scr, gate_scr, grp_scr):
    g = pl.program_id(2)
    hf = hf_ref[0]

    @pl.when(g == 0)
    def _():
        logits = _dot(hf, wr_ref[...]) + br_ref[...]
        lane = lax.broadcasted_iota(jnp.int32, logits.shape, 1)
        big = jnp.int32(1 << 20)
        is_grp = lane < MOE_GROUPS
        lg = jnp.where(is_grp, logits, -jnp.inf)
        mg = jnp.max(lg, axis=-1, keepdims=True)
        g_val = 1.0 / jnp.sum(jnp.exp(lg - mg), axis=-1, keepdims=True)
        g_idx = jnp.min(jnp.where(is_grp & (lg == mg), lane, big), axis=-1, keepdims=True)
        lo = MOE_GROUPS + g_idx * MOE_EXP_PER_GROUP
        in_grp = (lane >= lo) & (lane < lo + MOE_EXP_PER_GROUP)
        le = jnp.where(in_grp, logits, -jnp.inf)
        me = jnp.max(le, axis=-1, keepdims=True)
        pe = jnp.exp(le - me)
        p = pe / jnp.sum(pe, axis=-1, keepdims=True)
        p1 = jnp.max(p, axis=-1, keepdims=True)
        i1 = jnp.min(jnp.where(in_grp & (p == p1), lane, big), axis=-1, keepdims=True)
        rest = jnp.where(in_grp & (lane != i1), p, -1.0)
        p2 = jnp.max(rest, axis=-1, keepdims=True)
        i2 = jnp.min(jnp.where(rest == p2, lane, big), axis=-1, keepdims=True)
        norm = g_val / (p1 + p2)
        w1, w2 = p1 * norm, p2 * norm
        for i in range(MOE_EXP_PER_GROUP):
            col = jnp.where(i1 == lo + i, w1, 0.0) + jnp.where(i2 == lo + i, w2, 0.0)
            gate_scr[i] = jnp.broadcast_to(col, gate_scr.shape[1:])
        grp_scr[...] = jnp.broadcast_to(g_idx, grp_scr.shape)
        acc_scr[...] = jnp.zeros_like(acc_scr)

    mine = grp_scr[...] == g
    total = None
    for i in range(MOE_EXP_PER_GROUP):
        gate = jnp.where(mine, gate_scr[i], 0.0)
        hid = _silu(_dot(hf, wg_ref[i])) * _dot(hf, wu_ref[i])
        hid = hid * jnp.concatenate([gate] * (hid.shape[1] // LANES), axis=1)
        part = _dot(hid.astype(BF16), wd_ref[i])
        total = part if total is None else total + part
    acc_scr[...] += total

    @pl.when(g == pl.num_programs(2) - 1)
    def _():
        y_ref[0] = x1_ref[0] + gf_ref[0] * acc_scr[...]


def _moe(hf, x1, mod, w_route, b_route, w_gate, w_up, w_down, tm):
    bsz, L, d = x1.shape
    n_exp, _, d_ff = w_gate.shape
    row = pl.BlockSpec((1, tm, d), lambda b, t, e: (b, t, 0))
    if mod.shape[1] == 1:
        gspec = pl.BlockSpec((1, 1, d), lambda b, t, e: (b, 0, 5))
    else:
        gspec = pl.BlockSpec((1, tm, d), lambda b, t, e: (b, t, 5))
    return pl.pallas_call(
        _moe_kernel,
        grid=(bsz, L // tm, n_exp // MOE_EXP_PER_GROUP),
        in_specs=[row, row, gspec,
                  pl.BlockSpec(w_route.shape, lambda b, t, e: (0, 0)),
                  pl.BlockSpec(b_route.shape, lambda b, t, e: (0, 0)),
                  pl.BlockSpec((MOE_EXP_PER_GROUP, d, d_ff), lambda b, t, e: (e, 0, 0)),
                  pl.BlockSpec((MOE_EXP_PER_GROUP, d, d_ff), lambda b, t, e: (e, 0, 0)),
                  pl.BlockSpec((MOE_EXP_PER_GROUP, d_ff, d), lambda b, t, e: (e, 0, 0))],
        out_specs=row,
        out_shape=jax.ShapeDtypeStruct((bsz, L, d), F32),
        scratch_shapes=[pltpu.VMEM((tm, d), F32), pltpu.VMEM((MOE_EXP_PER_GROUP, tm, LANES), F32),
                        pltpu.VMEM((tm, LANES), jnp.int32)],
        compiler_params=_params("arbitrary", "arbitrary", "arbitrary"),
        name="moe",
    )(hf, x1, mod, w_route, b_route, w_gate, w_up, w_down)


def _kv_kernel(x_ref, innorm_ref, wc_ref, wr_ref, wrot_ref, latnorm_ref, cos_ref, sin_ref, ckv_ref, kr_ref):
    hn = _rms(x_ref[0], innorm_ref[...]).astype(BF16)
    ckv_ref[0] = _rms(_dot(hn, wc_ref[...]), latnorm_ref[...])
    kr_ref[0] = _dot(hn, wr_ref[...]) * cos_ref[...] + _dot(hn, wrot_ref[...]) * sin_ref[...]


def _rope_spec(table, tm):
    if table.shape[0] == 1:
        return pl.BlockSpec((1, LANES), lambda b, t: (0, 0))
    return pl.BlockSpec((tm, LANES), lambda b, t: (t, 0))


def _shared_kv(x, consts, cos, sin, tm):
    bsz, L, d = x.shape
    rank = consts[1].shape[1]
    row = lambda n: pl.BlockSpec((1, tm, n), lambda b, t: (b, t, 0))
    return pl.pallas_call(
        _kv_kernel,
        grid=(bsz, L // tm),
        in_specs=[row(d)] + [_const_spec(a) for a in consts] + [_rope_spec(cos, tm), _rope_spec(sin, tm)],
        out_specs=[row(rank), row(LANES)],
        out_shape=[jax.ShapeDtypeStruct((bsz, L, rank), F32), jax.ShapeDtypeStruct((bsz, L, LANES), F32)],
        compiler_params=_params("arbitrary", "arbitrary"),
        name="shared_kv",
    )(x, *consts, cos, sin)


def _mla_q_kernel(x_ref, sh_ref, sc_ref, mnorm_ref, wdq_ref, qlat_ref, wuq_ref, wuqrot_ref, qnorm_ref,
                  cos_ref, sin_ref, q_ref, *, scale):
    hm = _rms(x_ref[0], mnorm_ref[...]) * (1.0 + sc_ref[0]) + sh_ref[0]
    cq = _rms(_dot(hm.astype(BF16), wdq_ref[...]), qlat_ref[...]).astype(BF16)
    qp = _dot(cq, wuq_ref[...])
    qr = _dot(cq, wuqrot_ref[...])
    for h in range(MLA_HEADS):
        hs = slice(h * LANES, (h + 1) * LANES)
        a = qp[:, hs] * cos_ref[...] + qr[:, hs] * sin_ref[...]
        inv = lax.rsqrt(jnp.sum(a * a, axis=-1, keepdims=True) * (1.0 / MLA_QK) + NORM_EPS)
        q_ref[0, :, hs] = (a * inv * (qnorm_ref[...] * scale)).astype(q_ref.dtype)


def _mla_q(x, mod, consts, cos, sin, tm, out_dtype, scale):
    bsz, L, d = x.shape
    n = MLA_HEADS * LANES
    row = lambda w: pl.BlockSpec((1, tm, w), lambda b, t: (b, t, 0))
    return pl.pallas_call(
        functools.partial(_mla_q_kernel, scale=scale),
        grid=(bsz, L // tm),
        in_specs=[row(d), _mod_spec(mod, tm, d, 0), _mod_spec(mod, tm, d, 1)] + [_const_spec(a) for a in consts]
        + [_rope_spec(cos, tm), _rope_spec(sin, tm)],
        out_specs=row(n),
        out_shape=jax.ShapeDtypeStruct((bsz, L, n), out_dtype),
        compiler_params=_params("arbitrary", "arbitrary"),
        name="mla_q",
    )(x, mod, mod, *consts, cos, sin)


def _kv_up_kernel(ckv_ref, kr_ref, wuk_ref, knorm_ref, *rest, vt_tile):
    if vt_tile:
        wuvt_ref, k_ref, vt_ref = rest
    else:
        (k_ref,) = rest
    c = ckv_ref[0].astype(BF16)
    kn = _dot(c, wuk_ref[...])
    for h in range(MLA_HEADS):
        hs = slice(h * LANES, (h + 1) * LANES)
        a = kn[:, hs] + kr_ref[0]
        inv = lax.rsqrt(jnp.sum(a * a, axis=-1, keepdims=True) * (1.0 / MLA_QK) + NORM_EPS)
        k_ref[0, :, hs] = (a * inv * knorm_ref[...]).astype(k_ref.dtype)
    if vt_tile:
        for s in range(c.shape[0] // vt_tile):
            vt = _dot_nt(wuvt_ref[...], c[s * vt_tile:(s + 1) * vt_tile])
            vt_ref[0, s] = vt.astype(vt_ref.dtype)


def _kv_up(ckv, kr_slot, consts, tm, out_dtype, vt_tile=0):
    bsz, L, rank = ckv.shape
    n = MLA_HEADS * LANES
    row = lambda w: pl.BlockSpec((1, tm, w), lambda b, t: (b, t, 0))
    out_specs = [row(n)]
    out_shape = [jax.ShapeDtypeStruct((bsz, L, n), out_dtype)]
    if vt_tile:
        nv = consts[-1].shape[0]
        out_specs.append(pl.BlockSpec((1, tm // vt_tile, nv, vt_tile), lambda b, t: (b, t, 0, 0)))
        out_shape.append(jax.ShapeDtypeStruct((bsz, L // vt_tile, nv, vt_tile), out_dtype))
    return pl.pallas_call(
        functools.partial(_kv_up_kernel, vt_tile=vt_tile),
        grid=(bsz, L // tm),
        in_specs=[row(rank), row(LANES)] + [_const_spec(a) for a in consts],
        out_specs=out_specs,
        out_shape=out_shape,
        compiler_params=_params("arbitrary", "arbitrary"),
        name="kv_up",
    )(ckv, kr_slot, *consts)


def _mla_pre_kernel(x_ref, sh_ref, sc_ref, *refs, scale, vt_tile, n_kv, n_q, n_up):
    kv_consts = refs[:n_kv]
    q_consts = refs[n_kv:n_kv + n_q]
    up_consts = refs[n_kv + n_q:n_kv + n_q + n_up]
    cos_ref, sin_ref, ckv_ref, kr_ref, q_ref, k_ref, vt_ref = refs[n_kv + n_q + n_up:]
    _kv_kernel(x_ref, *kv_consts, cos_ref, sin_ref, ckv_ref, kr_ref)
    _mla_q_kernel(x_ref, sh_ref, sc_ref, *q_consts, cos_ref, sin_ref, q_ref, scale=scale)
    _kv_up_kernel(ckv_ref, kr_ref, *up_consts, k_ref, vt_ref, vt_tile=vt_tile)


def _mla_pre(x, mod, kv_consts, q_consts, up_consts, cos, sin, tm, scale, vt_tile):
    bsz, L, d = x.shape
    rank = kv_consts[1].shape[1]
    n = MLA_HEADS * LANES
    nv = up_consts[-1].shape[0]
    consts = kv_consts + q_consts + up_consts
    row = lambda w: pl.BlockSpec((1, tm, w), lambda b, t: (b, t, 0))
    return pl.pallas_call(
        functools.partial(_mla_pre_kernel, scale=scale, vt_tile=vt_tile, n_kv=len(kv_consts), n_q=len(q_consts),
                          n_up=len(up_consts)),
        grid=(bsz, L // tm),
        in_specs=[row(d), _mod_spec(mod, tm, d, 0), _mod_spec(mod, tm, d, 1)] + [_const_spec(a) for a in consts]
        + [_rope_spec(cos, tm), _rope_spec(sin, tm)],
        out_specs=[row(rank), row(LANES), row(n), row(n),
                   pl.BlockSpec((1, tm // vt_tile, nv, vt_tile), lambda b, t: (b, t, 0, 0))],
        out_shape=[jax.ShapeDtypeStruct((bsz, L, rank), F32), jax.ShapeDtypeStruct((bsz, L, LANES), F32),
                   jax.ShapeDtypeStruct((bsz, L, n), BF16), jax.ShapeDtypeStruct((bsz, L, n), BF16),
                   jax.ShapeDtypeStruct((bsz, L // vt_tile, nv, vt_tile), BF16)],
        compiler_params=_params("arbitrary", "arbitrary"),
        name="mla_pre",
    )(x, mod, mod, *consts, cos, sin)


def _flash_kernel(q_ref, k_ref, vt_ref, o_ref, *, tq, tk):
    qi = pl.program_id(2)
    sub = tk // tq
    n_full = (qi * tq) // tk

    def update(carry, c, masked):
        new = []
        off = pl.multiple_of(c * tk, tk)
        scores = [_dot_nt(k_ref[0, pl.ds(off, tk), hh * LANES:(hh + 1) * LANES],
                          q_ref[0, :, hh * LANES:(hh + 1) * LANES]) for hh in range(len(carry))]
        for hh, (m, acc) in enumerate(carry):
            s = scores[hh]
            if masked:
                s = jnp.where(key_minus_query <= qi * tq - c * tk, s, -jnp.inf)
            m_new = jnp.maximum(m, jnp.max(s, axis=0, keepdims=True))
            pb = jnp.exp2(s - m_new).astype(BF16)
            acc = acc * jnp.exp2(m - m_new)
            for t in range(sub):
                vt = vt_ref[0, c * sub + t, hh * MLA_V:(hh + 1) * MLA_V, :]
                acc = acc + _dot(jnp.concatenate([vt, ones], axis=0), pb[t * tq:(t + 1) * tq])
            new.append((m_new, acc))
        return tuple(new)

    ones = jnp.ones((16, tq), BF16)
    key_minus_query = (lax.broadcasted_iota(jnp.int32, (tk, tq), 0)
                       - lax.broadcasted_iota(jnp.int32, (tk, tq), 1))
    one = (jnp.full((1, tq), NEG_BIG, F32), jnp.zeros((MLA_V + 16, tq), F32))
    carry = lax.fori_loop(0, n_full, lambda c, cr: update(cr, c, False), (one,) * FLASH_HEADS)
    carry = update(carry, n_full, True)
    out_t = jnp.concatenate([acc[:MLA_V] / acc[MLA_V:MLA_V + 1] for (_, acc) in carry], axis=0)
    o_ref[0] = out_t.T.astype(o_ref.dtype)


def _flash(q, k, vt, tq):
    bsz, S, _ = q.shape
    nh = FLASH_HEADS
    qspec = pl.BlockSpec((1, tq, nh * LANES), lambda b, h, i: (b, i, h))
    kspec = pl.BlockSpec((1, S, nh * LANES), lambda b, h, i: (b, 0, h))
    vspec = pl.BlockSpec((1, S // tq, nh * MLA_V, tq), lambda b, h, i: (b, 0, h, 0))
    return pl.pallas_call(
        functools.partial(_flash_kernel, tq=tq, tk=min(FLASH_KEYS, S)),
        grid=(bsz, MLA_HEADS // nh, S // tq),
        in_specs=[qspec, kspec, vspec],
        out_specs=pl.BlockSpec((1, tq, nh * MLA_V), lambda b, h, i: (b, i, h)),
        out_shape=jax.ShapeDtypeStruct((bsz, S, MLA_HEADS * MLA_V), BF16),
        compiler_params=_params("arbitrary", "arbitrary", "arbitrary"),
        name="flash",
    )(q, k, vt)


def _paged_kernel(pt_ref, ckv_hbm, krt_hbm, q_ref, knew_ref, cnew_ref, knorm_ref, wukt_ref, wukt_slot_ref,
                  wuv_slot_ref, o_ref, m_scr, l_scr, acc_scr, lhs_scr, c_scr, cbuf, krbuf, sems, *, n_pages):
    b, j = pl.program_id(0), pl.program_id(1)
    steps = pl.num_programs(1)
    step = b * steps + j
    last_step = pl.num_programs(0) * steps - 1
    slot = step % 2
    nh = MLA_HEADS
    n_kn = nh * MLA_NOPE
    page = ckv_hbm.shape[1]
    tile = 2 * page
    head = lax.broadcasted_iota(jnp.int32, (nh, 1), 0)

    def page_copies(pid, p, to_slot):
        return (pltpu.make_async_copy(ckv_hbm.at[pid], cbuf.at[to_slot, pl.ds(p * page, page), :],
                                      sems.at[to_slot, 0]),
                pltpu.make_async_copy(krt_hbm.at[pid], krbuf.at[to_slot, :, pl.ds(p * page, page)],
                                      sems.at[to_slot, 1]))

    def start_pages(for_step, to_slot, positions):
        fb, fj = for_step // steps, for_step % steps
        for p in positions:
            for cp in page_copies(pt_ref[fb, fj * n_pages + p], p, to_slot):
                cp.start()

    def wait_pages(in_slot):
        for p in range(n_pages):
            for cp in page_copies(0, p, in_slot):
                cp.wait()

    @pl.when(step == 0)
    def _():
        start_pages(step, slot, range(n_pages))

    qg = q_ref[0] * knorm_ref[...]

    @pl.when(j == 0)
    def _():
        m_scr[...] = jnp.full_like(m_scr, NEG_BIG)
        l_scr[...] = jnp.zeros_like(l_scr)
        acc_scr[...] = jnp.zeros_like(acc_scr)
        qb = qg.astype(BF16)
        qabs = jnp.zeros((nh, lhs_scr.shape[1]), F32)
        for h in range(nh):
            qabs = qabs + jnp.where(head == h, _dot(qb, wukt_slot_ref[h]), 0.0)
        lhs_scr[:n_kn, :] = wukt_ref[...]
        lhs_scr[n_kn:, :] = jnp.concatenate([qabs, jnp.zeros_like(qabs)], axis=0).astype(BF16)

    wait_pages(slot)
    nxt = jnp.minimum(step + 1, last_step)
    q_rope = qg[:, MLA_NOPE:MLA_QK].astype(BF16)
    scores = []
    for i in range(n_pages // 2):
        start_pages(nxt, 1 - slot, (2 * i, 2 * i + 1))
        c2 = cbuf[slot, i * tile:(i + 1) * tile, :].astype(BF16)
        c_scr[i * tile:(i + 1) * tile, :] = c2
        both = _dot_nt(lhs_scr[...], c2)
        knt = both[:n_kn]
        ss = jnp.sum((knt * knt).reshape(nh, MLA_NOPE, tile), axis=1)
        kr2 = krbuf[slot, :, i * tile:(i + 1) * tile]
        ss = ss + jnp.sum(kr2 * kr2, axis=0, keepdims=True)
        num = both[n_kn:n_kn + nh] + _dot(q_rope, kr2.astype(BF16))
        scores.append(num * lax.rsqrt(ss * (1.0 / MLA_QK) + NORM_EPS))
    s = jnp.concatenate(scores, axis=1)
    m = m_scr[...]
    m_new = jnp.maximum(m, jnp.max(s, axis=-1, keepdims=True))
    pr = jnp.exp(s - m_new)
    corr = jnp.exp(m - m_new)
    l = l_scr[...] * corr + jnp.sum(pr, axis=-1, keepdims=True)
    pb = pr.astype(BF16)
    pv = [None, None]
    for i in range(n_pages // 2):
        part = _dot(pb[:, i * tile:(i + 1) * tile], c_scr[i * tile:(i + 1) * tile, :])
        pv[i % 2] = part if pv[i % 2] is None else pv[i % 2] + part
    acc = acc_scr[...] * corr + (pv[0] + pv[1])
    m_scr[...] = m_new
    l_scr[...] = l
    acc_scr[...] = acc

    @pl.when(j == steps - 1)
    def _():
        s_new = jnp.sum(q_ref[0] * knew_ref[0], axis=-1, keepdims=True)
        m_fin = jnp.maximum(m_new, s_new)
        corr_fin = jnp.exp(m_new - m_fin)
        p_new = jnp.exp(s_new - m_fin)
        l_fin = l * corr_fin + p_new
        o_lat = ((acc * corr_fin + p_new * cnew_ref[0]) / l_fin).astype(BF16)
        out = jnp.zeros((nh, LANES), F32)
        for h in range(nh):
            out = out + jnp.where(head == h, _dot(o_lat, wuv_slot_ref[h]), 0.0)
        o_ref[0] = out

    @pl.when(step == last_step)
    def _():
        wait_pages(1 - slot)


def _paged_attention(q, k_new, c_new, cache_ckv, cache_krope_t, page_table, consts, n_pages):
    nb, nh, _ = q.shape
    _, page, rank = cache_ckv.shape
    rope = cache_krope_t.shape[1]
    steps = page_table.shape[1] // n_pages
    assert n_pages % 2 == 0 and steps * n_pages == page_table.shape[1]
    per_b = lambda a: pl.BlockSpec((1,) + a.shape[1:], lambda b, j, pt: (b,) + (0,) * (a.ndim - 1))
    const = lambda a: pl.BlockSpec(a.shape, lambda b, j, pt: (0,) * a.ndim)
    hbm = pl.BlockSpec(memory_space=pl.ANY)
    grid_spec = pltpu.PrefetchScalarGridSpec(
        num_scalar_prefetch=1,
        grid=(nb, steps),
        in_specs=[hbm, hbm, per_b(q), per_b(k_new), per_b(c_new)] + [const(a) for a in consts],
        out_specs=pl.BlockSpec((1, nh, LANES), lambda b, j, pt: (b, 0, 0)),
        scratch_shapes=[pltpu.VMEM((nh, 1), F32), pltpu.VMEM((nh, 1), F32), pltpu.VMEM((nh, rank), F32),
                        pltpu.VMEM((nh * MLA_NOPE + 2 * nh, rank), BF16),
                        pltpu.VMEM((n_pages * page, rank), BF16),
                        pltpu.VMEM((2, n_pages * page, rank), F32),
                        pltpu.VMEM((2, rope, n_pages * page), F32),
                        pltpu.SemaphoreType.DMA((2, 2))],
    )
    return pl.pallas_call(
        functools.partial(_paged_kernel, n_pages=n_pages),
        grid_spec=grid_spec,
        out_shape=jax.ShapeDtypeStruct((nb, nh, LANES), F32),
        compiler_params=_params("arbitrary", "arbitrary"),
        name="paged_attn",
    )(page_table, cache_ckv, cache_krope_t, q, k_new, c_new, *consts)


def _rotate_half_cols(w):
    half = w.shape[-1] // 2
    return jnp.concatenate([-w[..., half:], w[..., :half]], axis=-1)


def _slot(nope, rope):
    ref = nope if nope is not None else rope
    lead = ref.shape[:-1]
    z = lambda n: jnp.zeros(lead + (n,), ref.dtype)
    a = nope if nope is not None else z(MLA_NOPE)
    b = rope if rope is not None else z(MLA_ROPE)
    return jnp.concatenate([a, b, z(LANES - MLA_QK)], axis=-1)


def _rope_tables(pos):
    inv = ROPE_THETA ** (-jnp.arange(0, MLA_ROPE, 2, dtype=F32) / MLA_ROPE)
    ang = pos.astype(F32)[:, None] * inv[None, :]
    ang = jnp.concatenate([ang, ang], axis=-1)
    n = pos.shape[0]
    cos = _slot(jnp.ones((n, MLA_NOPE), F32), jnp.cos(ang))
    sin = _slot(jnp.zeros((n, MLA_NOPE), F32), jnp.sin(ang))
    return cos, sin


def _prepare(w):
    d = w['mix_norm'].shape[-1]
    p = {}
    p['route_w'] = [jnp.pad(jnp.concatenate([w['moe_w_group'][l], w['moe_w_router'][l]], axis=-1),
                            ((0, 0), (0, LANES - MOE_GROUPS - MOE_EXPERTS))).astype(BF16) for l in range(2)]
    p['route_b'] = [jnp.pad(jnp.concatenate([w['moe_b_group'][l], w['moe_b_router'][l]], axis=-1),
                            (0, LANES - MOE_GROUPS - MOE_EXPERTS)).reshape(1, LANES) for l in range(2)]
    p['moe_gate'] = w['moe_w_gate'].astype(BF16)
    p['moe_up'] = w['moe_w_up'].astype(BF16)
    p['moe_down'] = w['moe_w_down'].astype(BF16)
    p['gla_out'] = [w['gla_out_norm'][0].reshape(1, -1), w['gla_w_out'][0].astype(BF16),
                    w['ffn_norm'][0].reshape(1, d)]
    rank = w['kv_lat_norm'].shape[0]
    w_r = w['kv_w_dkv'][:, rank:]
    p['kv'] = [w['kv_in_norm'].reshape(1, d), w['kv_w_dkv'][:, :rank].astype(BF16),
               _slot(None, w_r).astype(BF16), _slot(None, _rotate_half_cols(w_r)).astype(BF16),
               w['kv_lat_norm'].reshape(1, rank)]
    q_rank = w['mla_w_dq'].shape[-1]
    wuq = w['mla_w_uq'][0].reshape(q_rank, MLA_HEADS, MLA_QK)
    wuq_n, wuq_r = wuq[..., :MLA_NOPE], wuq[..., MLA_NOPE:]
    p['mla_q'] = [w['mix_norm'][1].reshape(1, d), w['mla_w_dq'][0].astype(BF16),
                  w['mla_q_lat_norm'][0].reshape(1, q_rank),
                  _slot(wuq_n, wuq_r).reshape(q_rank, -1).astype(BF16),
                  _slot(None, _rotate_half_cols(wuq_r)).reshape(q_rank, -1).astype(BF16),
                  jnp.pad(w['mla_q_norm'][0], (0, LANES - MLA_QK)).reshape(1, LANES)]
    knorm = jnp.pad(w['kv_k_norm'], (0, LANES - MLA_QK)).reshape(1, LANES)
    wuk, wuv = w['kv_w_uk'], w['kv_w_uv']
    p['kv_up'] = [_slot(wuk, None).reshape(rank, -1).astype(BF16), knorm]
    p['wuv_t'] = wuv.reshape(rank, -1).T.astype(BF16)
    wukt = wuk.transpose(1, 2, 0)
    p['paged'] = [knorm, wukt.reshape(MLA_HEADS * MLA_NOPE, rank).astype(BF16),
                  jnp.pad(wukt, ((0, 0), (0, LANES - MLA_NOPE), (0, 0))).astype(BF16),
                  jnp.pad(wuv.transpose(1, 0, 2), ((0, 0), (0, 0), (0, LANES - MLA_V))).astype(BF16)]
    wo = w['mla_w_out'][0].reshape(MLA_HEADS, MLA_V, d)
    p['mla_out_slot'] = [jnp.pad(wo, ((0, 0), (0, LANES - MLA_V), (0, 0))).reshape(MLA_HEADS * LANES, d)
                         .astype(BF16), w['ffn_norm'][1].reshape(1, d)]
    p['mla_out'] = [w['mla_w_out'][0].astype(BF16), w['ffn_norm'][1].reshape(1, d)]
    return p


def _moe_layer(l, hf, x1, mod, p):
    return _moe(hf, x1, mod, p['route_w'][l], p['route_b'][l], p['moe_gate'][l], p['moe_up'][l],
                p['moe_down'][l], min(MOE_ROWS, x1.shape[1]))


def _trunk(x, mods, pos, gla_s0, paged, w, p, tm):
    cos, sin = _rope_tables(pos)
    gla_w = (w['mix_norm'][0], w['gla_w_in'][0], w['gla_w_gate_up'][0], w['gla_b_gate'][0])
    if paged is None:
        x1, hf, s_fin = _gla_layer(x, mods[0], gla_s0, _gla_proj_consts(*gla_w), p['gla_out'], tm)
    else:
        q, k, v, r, gk = _gla_proj(x, mods[0], *gla_w, tm)
        o, s_fin = _gla_step(q[0], k[0], v[0], gk[0], gla_s0)
        x1, hf = _mixer_out(functools.partial(_gla_out_kernel, dv=v.shape[-1] // GLA_HEADS), [o[None], r], x,
                            mods[0], p['gla_out'], tm, "gla_out")
    x2 = _moe_layer(0, hf, x1, mods[0], p)
    if paged is None:
        tq = min(FLASH_T, x.shape[1])
        ckv, kr_slot, qh, kh, vt = _mla_pre(x2, mods[1], p['kv'], p['mla_q'], p['kv_up'] + [p['wuv_t']], cos, sin,
                                            tm, MLA_QK ** -0.5 * math.log2(math.e), tq)
        o = _flash(qh, kh, vt, tq)
        out_consts = p['mla_out']
    else:
        n = x.shape[1]
        ckv, kr_slot = _shared_kv(x2, p['kv'], cos, sin, tm)
        qh = _mla_q(x2, mods[1], p['mla_q'], cos, sin, tm, F32, MLA_QK ** -0.5)
        (kh,) = _kv_up(ckv, kr_slot, p['kv_up'], tm, F32)
        o = _paged_attention(qh.reshape(n, MLA_HEADS, LANES), kh.reshape(n, MLA_HEADS, LANES),
                             ckv.reshape(n, 1, -1), paged[0], jnp.swapaxes(paged[1], 1, 2), paged[2], p['paged'],
                             min(PAGES_PER_STEP, paged[2].shape[1]))
        o = o.reshape(1, n, MLA_HEADS * LANES).astype(BF16)
        out_consts = p['mla_out_slot']
    x3, hf = _mixer_out(_mla_out_kernel, [o], x2, mods[1], out_consts, tm, "mla_out")
    y = _moe_layer(1, hf, x3, mods[1], p)
    return y, s_fin, ckv, kr_slot[..., MLA_NOPE:MLA_QK]


def _forward(x_prompt, x_sample, c_prompt, c_sample, state_gla, cache_ckv, cache_krope, page_table, w, past_len):
    bsz, seq, d = x_prompt.shape
    nd, dl, _ = x_sample.shape
    assert dl == 1, "decode path handles one new token per sequence"
    p = _prepare(w)
    mod = _ada_mod(jnp.concatenate([c_prompt, c_sample], axis=0), w['ada_w'], w['ada_b'])
    mods_p = [mod[l, :bsz].reshape(bsz, 1, -1) for l in range(2)]
    mods_s = [mod[l, bsz:].reshape(1, nd, -1) for l in range(2)]
    s0_p = jnp.zeros((bsz,) + state_gla.shape[2:], state_gla.dtype)
    y_p, s_p, ckv_p, kr_p = _trunk(x_prompt, mods_p, jnp.arange(seq), s0_p, None, w, p, min(512, seq))
    y_s, s_s, ckv_s, kr_s = _trunk(x_sample.reshape(1, nd, d), mods_s, jnp.full((1,), past_len), state_gla[0],
                                   (cache_ckv, cache_krope, page_table), w, p, nd)
    return (y_p, y_s.reshape(nd, 1, d), s_p[None], s_s[None], ckv_p, kr_p,
            ckv_s.reshape(nd, 1, -1), kr_s.reshape(nd, 1, -1))


def kernel(x_prompt, x_sample, c_prompt, c_sample, state_gla, cache_ckv, cache_krope, page_table, ada_w, ada_b, mix_norm, ffn_norm, gla_w_in, gla_w_gate_up, gla_b_gate, gla_out_norm, gla_w_out, mla_w_dq, mla_q_lat_norm, mla_w_uq, mla_q_norm, mla_w_out, kv_in_norm, kv_w_dkv, kv_lat_norm, kv_w_uk, kv_w_uv, kv_k_norm, moe_w_group, moe_b_group, moe_w_router, moe_b_router, moe_w_gate, moe_w_up, moe_w_down):
    w = dict(ada_w=ada_w, ada_b=ada_b, mix_norm=mix_norm, ffn_norm=ffn_norm,
             gla_w_in=gla_w_in, gla_w_gate_up=gla_w_gate_up, gla_b_gate=gla_b_gate,
             gla_out_norm=gla_out_norm, gla_w_out=gla_w_out,
             mla_w_dq=mla_w_dq, mla_q_lat_norm=mla_q_lat_norm, mla_w_uq=mla_w_uq,
             mla_q_norm=mla_q_norm, mla_w_out=mla_w_out,
             kv_in_norm=kv_in_norm, kv_w_dkv=kv_w_dkv, kv_lat_norm=kv_lat_norm,
             kv_w_uk=kv_w_uk, kv_w_uv=kv_w_uv, kv_k_norm=kv_k_norm,
             moe_w_group=moe_w_group, moe_b_group=moe_b_group, moe_w_router=moe_w_router,
             moe_b_router=moe_b_router, moe_w_gate=moe_w_gate, moe_w_up=moe_w_up, moe_w_down=moe_w_down)
    past_len = page_table.shape[1] * cache_ckv.shape[1]
    return _forward(x_prompt, x_sample, c_prompt, c_sample, state_gla, cache_ckv, cache_krope, page_table, w,
                    past_len)
```

```python
import functools
import math

import jax
import jax.numpy as jnp
from jax import lax
from jax.experimental import pallas as pl
from jax.experimental.pallas import tpu as pltpu

F32 = jnp.float32
BF16 = jnp.bfloat16

NORM_EPS = 1e-6
GLA_HEADS = 4
GLA_GATE_NORM = 16.0
GLA_CHUNK = 256
GLA_SAFE_EXPONENT = 80.0
MLA_HEADS = 8
MLA_NOPE = 64
MLA_ROPE = 32
MLA_QK = MLA_NOPE + MLA_ROPE
MLA_V = 64
ROPE_THETA = 10000.0
MOE_GROUPS = 4
MOE_EXP_PER_GROUP = 4
MOE_EXPERTS = MOE_GROUPS * MOE_EXP_PER_GROUP
LANES = 128
SUBLANES = 8
NEG_BIG = -1e30
FLASH_T = 512
FLASH_KEYS = 1024
MOE_ROWS = 1024
PAGES_PER_STEP = 64
FLASH_HEADS = 4
VMEM_LIMIT = 52 * 1024 * 1024


def _params(*sem):
    return pltpu.CompilerParams(dimension_semantics=sem, vmem_limit_bytes=VMEM_LIMIT)


def _dot(a, b):
    return jnp.dot(a, b, preferred_element_type=F32)


def _dot_nt(a, b):
    return lax.dot_general(a, b, (((1,), (1,)), ((), ())), preferred_element_type=F32)


def _dot_tn(a, b):
    return lax.dot_general(a, b, (((0,), (0,)), ((), ())), preferred_element_type=F32)


def _rms(x, g):
    return x * lax.rsqrt(jnp.mean(x * x, axis=-1, keepdims=True) + NORM_EPS) * g


def _silu(x):
    return x / (1.0 + jnp.exp(-x))


def _split_bf16(x):
    hi = x.astype(BF16)
    lo = (x - hi.astype(F32)).astype(BF16)
    return hi, lo


def _ada_kernel(c_ref, w_ref, b_ref, o_ref):
    a = _silu(c_ref[...]).astype(BF16)
    o_ref[0] = _dot(a, w_ref[0].astype(BF16)) + b_ref[0]


def _ada_mod(c, ada_w, ada_b):
    depth, d, n = ada_w.shape
    r = c.shape[0]
    tn = 1536
    return pl.pallas_call(
        _ada_kernel,
        grid=(depth, n // tn),
        in_specs=[
            pl.BlockSpec((r, d), lambda l, j: (0, 0)),
            pl.BlockSpec((1, d, tn), lambda l, j: (l, 0, j)),
            pl.BlockSpec((1, 1, tn), lambda l, j: (l, 0, j)),
        ],
        out_specs=pl.BlockSpec((1, r, tn), lambda l, j: (l, 0, j)),
        out_shape=jax.ShapeDtypeStruct((depth, r, n), F32),
        compiler_params=_params("arbitrary", "arbitrary"),
        name="ada_mod",
    )(c, ada_w, ada_b.reshape(depth, 1, n))


def _mod_spec(mod, tm, d, chunk):
    if mod.shape[1] == 1:
        return pl.BlockSpec((1, 1, d), lambda b, t: (b, 0, chunk))
    return pl.BlockSpec((1, tm, d), lambda b, t: (b, t, chunk))


def _const_spec(arr):
    nd = arr.ndim
    return pl.BlockSpec(arr.shape, lambda b, t: (0,) * nd)


def _gla_proj_kernel(x_ref, sh_ref, sc_ref, g_ref, wq_ref, wk_ref, wv_ref, wr_ref, wg_ref, wgu_ref, bg_ref,
                     q_ref, k_ref, v_ref, r_ref, gk_ref, *, q_scale):
    hm = _rms(x_ref[0], g_ref[...]) * (1.0 + sc_ref[0]) + sh_ref[0]
    hb = hm.astype(BF16)
    q_ref[0] = _dot(hb, wq_ref[...]) * q_scale
    k_ref[0] = _dot(hb, wk_ref[...])
    v_ref[0] = _dot(hb, wv_ref[...]).astype(BF16)
    r_ref[0] = _dot(hb, wr_ref[...])
    g_low = _dot(hb, wg_ref[...])
    z = _dot(g_low.astype(BF16), wgu_ref[...]) + bg_ref[...]
    log_sig = jnp.minimum(z, 0.0) - jnp.log(1.0 + jnp.exp(-jnp.abs(z)))
    gk_ref[0] = log_sig * (1.0 / GLA_GATE_NORM)


def _gla_proj_consts(norm_g, w_in, w_gate_up, b_gate):
    hk = w_gate_up.shape[1]
    hv = (w_in.shape[1] - 2 * hk - w_gate_up.shape[0]) // 2
    cuts = [0, hk, 2 * hk, 2 * hk + hv, 2 * hk + 2 * hv, w_in.shape[1]]
    pieces = [w_in[:, a:b].astype(BF16) for a, b in zip(cuts[:-1], cuts[1:])]
    return [norm_g.reshape(1, -1)] + pieces + [w_gate_up.astype(BF16), b_gate.reshape(1, hk)]


def _gla_proj(x, mod, norm_g, w_in, w_gate_up, b_gate, tm):
    bsz, L, d = x.shape
    consts = _gla_proj_consts(norm_g, w_in, w_gate_up, b_gate)
    hk, hv = consts[1].shape[1], consts[3].shape[1]
    row = lambda n: pl.BlockSpec((1, tm, n), lambda b, t: (b, t, 0))
    return pl.pallas_call(
        functools.partial(_gla_proj_kernel, q_scale=float((hk // GLA_HEADS) ** -0.5)),
        grid=(bsz, L // tm),
        in_specs=[row(d), _mod_spec(mod, tm, d, 0), _mod_spec(mod, tm, d, 1)] + [_const_spec(a) for a in consts],
        out_specs=[row(hk), row(hk), row(hv), row(hv), row(hk)],
        out_shape=[
            jax.ShapeDtypeStruct((bsz, L, hk), F32),
            jax.ShapeDtypeStruct((bsz, L, hk), F32),
            jax.ShapeDtypeStruct((bsz, L, hv), BF16),
            jax.ShapeDtypeStruct((bsz, L, hv), F32),
            jax.ShapeDtypeStruct((bsz, L, hk), F32),
        ],
        compiler_params=_params("arbitrary", "arbitrary"),
        name="gla_proj",
    )(x, mod, mod, *consts)


def _gla_layer_kernel(x_ref, sh_ref, sc_ref, ga_ref, shf_ref, scf_ref, s0_ref, mnorm_ref, wq_ref, wk_ref, wv_ref,
                      wr_ref, wg_ref, wgu_ref, bg_ref, onorm_ref, wout_ref, fnorm_ref,
                      x1_ref, hf_ref, sf_ref, st_scr, o_scr, b_scr, k_scr, *, chunk, q_scale):
    t = pl.program_id(1)
    n_heads, dv, dk = st_scr.shape

    @pl.when(t == 0)
    def _():
        for h in range(n_heads):
            st_scr[h] = s0_ref[0, h].T

    x = x_ref[0]
    tm = x.shape[0]
    n_chunks = tm // chunk
    hb = (_rms(x, mnorm_ref[...]) * (1.0 + sc_ref[0]) + sh_ref[0]).astype(BF16)
    q = _dot(hb, wq_ref[...]) * q_scale
    k = _dot(hb, wk_ref[...])
    v = _dot(hb, wv_ref[...]).astype(BF16)
    r = _dot(hb, wr_ref[...])
    z = _dot(_dot(hb, wg_ref[...]).astype(BF16), wgu_ref[...]) + bg_ref[...]
    gk = (jnp.minimum(z, 0.0) - jnp.log(1.0 + jnp.exp(-jnp.abs(z)))) * (1.0 / GLA_GATE_NORM)
    tri = jnp.where(lax.broadcasted_iota(jnp.int32, (tm, tm), 0) >= lax.broadcasted_iota(jnp.int32, (tm, tm), 1),
                    1.0, 0.0).astype(BF16)
    g_hi, g_lo = _split_bf16(gk)
    cum = _dot(tri, g_hi) + _dot(tri, g_lo)
    causal = (lax.broadcasted_iota(jnp.int32, (chunk, chunk), 0)
              >= lax.broadcasted_iota(jnp.int32, (chunk, chunk), 1))

    def local_gate(c, cols):
        b = cum[c * chunk:(c + 1) * chunk, cols]
        return b - cum[c * chunk - 1:c * chunk, cols] if c else b

    total_decay = jnp.concatenate([local_gate(c, slice(None))[chunk - 1:chunk] for c in range(n_chunks)], axis=0)
    factored_ok = jnp.max(-total_decay) <= GLA_SAFE_EXPONENT

    @pl.when(factored_ok)
    def _():
        for h in range(n_heads):
            ks = slice(h * dk, (h + 1) * dk)
            vs = slice(h * dv, (h + 1) * dv)
            st = st_scr[h]
            for c in range(n_chunks):
                rows = slice(c * chunk, (c + 1) * chunk)
                b = local_gate(c, ks)
                qt = (q[rows, ks] * jnp.exp(b)).astype(BF16)
                kt = (k[rows, ks] * jnp.exp(-b)).astype(BF16)
                vc = v[rows, vs]
                attn = jnp.where(causal, _dot_nt(qt, kt), 0.0).astype(BF16)
                o_scr[rows, vs] = _dot(attn, vc) + _dot_nt(qt, st.astype(BF16))
                st = (st + _dot_tn(vc, kt)) * jnp.exp(b[chunk - 1:chunk, :])
            st_scr[h] = st

    @pl.when(jnp.logical_not(factored_ok))
    def _():
        k_scr[...] = k
        lane = lax.broadcasted_iota(jnp.int32, (chunk, chunk), 1)
        for h in range(n_heads):
            ks = slice(h * dk, (h + 1) * dk)
            vs = slice(h * dv, (h + 1) * dv)
            st = st_scr[h]
            for c in range(n_chunks):
                rows = slice(c * chunk, (c + 1) * chunk)
                b = local_gate(c, ks)
                b_scr[rows, ks] = b
                qc = q[rows, ks]

                def key_rows(grp, attn, c=c, ks=ks, b=b, qc=qc):
                    first = pl.multiple_of(c * chunk + grp * SUBLANES, SUBLANES)
                    b_grp = b_scr[pl.ds(first, SUBLANES), ks]
                    k_grp = k_scr[pl.ds(first, SUBLANES), ks]
                    for i in range(SUBLANES):
                        decay = jnp.exp(jnp.minimum(b - b_grp[i:i + 1], 0.0))
                        col = jnp.sum(qc * k_grp[i:i + 1] * decay, axis=-1, keepdims=True)
                        attn = jnp.where(lane == grp * SUBLANES + i, col, attn)
                    return attn

                attn = lax.fori_loop(0, chunk // SUBLANES, key_rows, jnp.zeros((chunk, chunk), F32))
                attn = jnp.where(causal, attn, 0.0).astype(BF16)
                vc = v[rows, vs]
                b_end = b[chunk - 1:chunk, :]
                qt = (qc * jnp.exp(b)).astype(BF16)
                o_scr[rows, vs] = _dot(attn, vc) + _dot_nt(qt, st.astype(BF16))
                kd = (k[rows, ks] * jnp.exp(b_end - b)).astype(BF16)
                st = st * jnp.exp(b_end) + _dot_tn(vc, kd)
            st_scr[h] = st

    parts = []
    for h in range(n_heads):
        vs = slice(h * dv, (h + 1) * dv)
        parts.append((_rms(o_scr[:, vs], onorm_ref[...]) * _silu(r[:, vs])).astype(BF16))
    m = _dot(jnp.concatenate(parts, axis=-1), wout_ref[...])
    _residual_ffn_norm(x, m, ga_ref[0], fnorm_ref[...], scf_ref[0], shf_ref[0], x1_ref, hf_ref)

    @pl.when(t == pl.num_programs(1) - 1)
    def _():
        for h in range(n_heads):
            sf_ref[0, h] = st_scr[h].T


def _gla_layer(x, mod, s0, proj_consts, out_consts, tm):
    bsz, L, d = x.shape
    _, n_heads, dk, dv = s0.shape
    consts = proj_consts + out_consts
    row = lambda a: pl.BlockSpec((1, tm, a.shape[-1]), lambda b, t: (b, t, 0))
    sspec = pl.BlockSpec((1, n_heads, dk, dv), lambda b, t: (b, 0, 0, 0))
    return pl.pallas_call(
        functools.partial(_gla_layer_kernel, chunk=math.gcd(tm, GLA_CHUNK), q_scale=float(dk ** -0.5)),
        grid=(bsz, L // tm),
        in_specs=[row(x)] + [_mod_spec(mod, tm, d, c) for c in range(5)] + [sspec]
        + [_const_spec(a) for a in consts],
        out_specs=[row(x), row(x), sspec],
        out_shape=[jax.ShapeDtypeStruct((bsz, L, d), F32), jax.ShapeDtypeStruct((bsz, L, d), BF16),
                   jax.ShapeDtypeStruct(s0.shape, F32)],
        scratch_shapes=[pltpu.VMEM((n_heads, dv, dk), F32), pltpu.VMEM((tm, n_heads * dv), F32),
                        pltpu.VMEM((tm, n_heads * dk), F32), pltpu.VMEM((tm, n_heads * dk), F32)],
        compiler_params=_params("arbitrary", "arbitrary"),
        name="gla_layer",
    )(x, mod, mod, mod, mod, mod, s0, *consts)


def _gla_step_kernel(q_ref, k_ref, v_ref, g_ref, s_ref, o_ref, sn_ref, *, nb, dk, dv):
    eye = (lax.broadcasted_iota(jnp.int32, (dk, dk), 0) == lax.broadcasted_iota(jnp.int32, (dk, dk), 1))

    def column(row_vec):
        return jnp.sum(jnp.where(eye, row_vec, 0.0), axis=1, keepdims=True)

    for j in range(nb):
        for h in range(GLA_HEADS):
            ks = slice(h * dk, (h + 1) * dk)
            vs = slice(h * dv, (h + 1) * dv)
            decay = column(jnp.exp(g_ref[j:j + 1, ks]))
            kc = column(k_ref[j:j + 1, ks])
            qc = column(q_ref[j:j + 1, ks])
            s_new = s_ref[j, h] * decay + kc * v_ref[j:j + 1, vs].astype(F32)
            sn_ref[j, h] = s_new
            o_ref[j:j + 1, vs] = jnp.sum(qc * s_new, axis=0, keepdims=True)


def _gla_step(q, k, v, gk, s0):
    n, hk = q.shape
    hv = v.shape[-1]
    dk, dv = hk // GLA_HEADS, hv // GLA_HEADS
    nb = 8
    row = lambda w: pl.BlockSpec((nb, w), lambda i: (i, 0))
    sspec = pl.BlockSpec((nb, GLA_HEADS, dk, dv), lambda i: (i, 0, 0, 0))
    return pl.pallas_call(
        functools.partial(_gla_step_kernel, nb=nb, dk=dk, dv=dv),
        grid=(n // nb,),
        in_specs=[row(hk), row(hk), row(hv), row(hk), sspec],
        out_specs=[row(hv), sspec],
        out_shape=[jax.ShapeDtypeStruct((n, hv), F32),
                   jax.ShapeDtypeStruct((n, GLA_HEADS, dk, dv), F32)],
        compiler_params=_params("arbitrary"),
        name="gla_step",
    )(q, k, v, gk, s0)


def _residual_ffn_norm(x, m, ga, fnorm, scf, shf, x1_ref, hf_ref):
    x1 = x + ga * m
    x1_ref[0] = x1
    hf_ref[0] = (_rms(x1, fnorm) * (1.0 + scf) + shf).astype(BF16)


def _gla_out_kernel(o_ref, r_ref, x_ref, ga_ref, shf_ref, scf_ref, onorm_ref, wout_ref, fnorm_ref,
                    x1_ref, hf_ref, *, dv):
    parts = []
    for h in range(GLA_HEADS):
        vs = slice(h * dv, (h + 1) * dv)
        y = _rms(o_ref[0, :, vs], onorm_ref[...]) * _silu(r_ref[0, :, vs])
        parts.append(y.astype(BF16))
    m = _dot(jnp.concatenate(parts, axis=-1), wout_ref[...])
    _residual_ffn_norm(x_ref[0], m, ga_ref[0], fnorm_ref[...], scf_ref[0], shf_ref[0], x1_ref, hf_ref)


def _mla_out_kernel(o_ref, x_ref, ga_ref, shf_ref, scf_ref, wout_ref, fnorm_ref, x1_ref, hf_ref):
    m = _dot(o_ref[0], wout_ref[...])
    _residual_ffn_norm(x_ref[0], m, ga_ref[0], fnorm_ref[...], scf_ref[0], shf_ref[0], x1_ref, hf_ref)


def _mixer_out(kernel_fn, acts, x, mod, consts, tm, name):
    bsz, L, d = x.shape
    row = lambda a: pl.BlockSpec((1, tm, a.shape[-1]), lambda b, t: (b, t, 0))
    return pl.pallas_call(
        kernel_fn,
        grid=(bsz, L // tm),
        in_specs=[row(a) for a in acts] + [row(x), _mod_spec(mod, tm, d, 2), _mod_spec(mod, tm, d, 3),
                                           _mod_spec(mod, tm, d, 4)] + [_const_spec(a) for a in consts],
        out_specs=[row(x), row(x)],
        out_shape=[jax.ShapeDtypeStruct((bsz, L, d), F32), jax.ShapeDtypeStruct((bsz, L, d), BF16)],
        compiler_params=_params("arbitrary", "arbitrary"),
        name=name,
    )(*acts, x, mod, mod, mod, *consts)


def _moe_kernel(hf_ref, x1_ref, gf_ref, wr_ref, br_ref, wg_ref, wu_ref, wd_ref, y_ref, acc_scr, gate_scr, grp_scr):
    g = pl.program_id(2)
    hf = hf_ref[0]

    @pl.when(g == 0)
    def _():
        logits = _dot(hf, wr_ref[...]) + br_ref[...]
        lane = lax.broadcasted_iota(jnp.int32, logits.shape, 1)
        big = jnp.int32(1 << 20)
        is_grp = lane < MOE_GROUPS
        lg = jnp.where(is_grp, logits, -jnp.inf)
        mg = jnp.max(lg, axis=-1, keepdims=True)
        g_val = 1.0 / jnp.sum(jnp.exp(lg - mg), axis=-1, keepdims=True)
        g_idx = jnp.min(jnp.where(is_grp & (lg == mg), lane, big), axis=-1, keepdims=True)
        lo = MOE_GROUPS + g_idx * MOE_EXP_PER_GROUP
        in_grp = (lane >= lo) & (lane < lo + MOE_EXP_PER_GROUP)
        le = jnp.where(in_grp, logits, -jnp.inf)
        me = jnp.max(le, axis=-1, keepdims=True)
        pe = jnp.exp(le - me)
        p = pe / jnp.sum(pe, axis=-1, keepdims=True)
        p1 = jnp.max(p, axis=-1, keepdims=True)
        i1 = jnp.min(jnp.where(in_grp & (p == p1), lane, big), axis=-1, keepdims=True)
        rest = jnp.where(in_grp & (lane != i1), p, -1.0)
        p2 = jnp.max(rest, axis=-1, keepdims=True)
        i2 = jnp.min(jnp.where(rest == p2, lane, big), axis=-1, keepdims=True)
        norm = g_val / (p1 + p2)
        w1, w2 = p1 * norm, p2 * norm
        for i in range(MOE_EXP_PER_GROUP):
            col = jnp.where(i1 == lo + i, w1, 0.0) + jnp.where(i2 == lo + i, w2, 0.0)
            gate_scr[i] = jnp.broadcast_to(col, gate_scr.shape[1:])
        grp_scr[...] = jnp.broadcast_to(g_idx, grp_scr.shape)
        acc_scr[...] = jnp.zeros_like(acc_scr)

    mine = grp_scr[...] == g
    total = None
    for i in range(MOE_EXP_PER_GROUP):
        gate = jnp.where(mine, gate_scr[i], 0.0)
        hid = _silu(_dot(hf, wg_ref[i])) * _dot(hf, wu_ref[i])
        hid = hid * jnp.concatenate([gate] * (hid.shape[1] // LANES), axis=1)
        part = _dot(hid.astype(BF16), wd_ref[i])
        total = part if total is None else total + part
    acc_scr[...] += total

    @pl.when(g == pl.num_programs(2) - 1)
    def _():
        y_ref[0] = x1_ref[0] + gf_ref[0] * acc_scr[...]


def _moe(hf, x1, mod, w_route, b_route, w_gate, w_up, w_down, tm):
    bsz, L, d = x1.shape
    n_exp, _, d_ff = w_gate.shape
    row = pl.BlockSpec((1, tm, d), lambda b, t, e: (b, t, 0))
    if mod.shape[1] == 1:
        gspec = pl.BlockSpec((1, 1, d), lambda b, t, e: (b, 0, 5))
    else:
        gspec = pl.BlockSpec((1, tm, d), lambda b, t, e: (b, t, 5))
    return pl.pallas_call(
        _moe_kernel,
        grid=(bsz, L // tm, n_exp // MOE_EXP_PER_GROUP),
        in_specs=[row, row, gspec,
                  pl.BlockSpec(w_route.shape, lambda b, t, e: (0, 0)),
                  pl.BlockSpec(b_route.shape, lambda b, t, e: (0, 0)),
                  pl.BlockSpec((MOE_EXP_PER_GROUP, d, d_ff), lambda b, t, e: (e, 0, 0)),
                  pl.BlockSpec((MOE_EXP_PER_GROUP, d, d_ff), lambda b, t, e: (e, 0, 0)),
                  pl.BlockSpec((MOE_EXP_PER_GROUP, d_ff, d), lambda b, t, e: (e, 0, 0))],
        out_specs=row,
        out_shape=jax.ShapeDtypeStruct((bsz, L, d), F32),
        scratch_shapes=[pltpu.VMEM((tm, d), F32), pltpu.VMEM((MOE_EXP_PER_GROUP, tm, LANES), F32),
                        pltpu.VMEM((tm, LANES), jnp.int32)],
        compiler_params=_params("arbitrary", "arbitrary", "arbitrary"),
        name="moe",
    )(hf, x1, mod, w_route, b_route, w_gate, w_up, w_down)


def _mla_out_moe_kernel(o_ref, x_ref, ga_ref, shf_ref, scf_ref, gf_ref, wout_ref, fnorm_ref, wr_ref, br_ref,
                        wg_ref, wu_ref, wd_ref, y_ref, acc_scr, gate_scr, grp_scr, hf_scr):
    @pl.when(pl.program_id(2) == 0)
    def _():
        _mla_out_kernel(o_ref, x_ref, ga_ref, shf_ref, scf_ref, wout_ref, fnorm_ref, y_ref, hf_scr)

    _moe_kernel(hf_scr, y_ref, gf_ref, wr_ref, br_ref, wg_ref, wu_ref, wd_ref, y_ref, acc_scr, gate_scr, grp_scr)


def _mla_out_moe(o, x, mod, out_consts, w_route, b_route, w_gate, w_up, w_down, tm):
    bsz, L, d = x.shape
    n_exp, _, d_ff = w_gate.shape
    row = lambda a: pl.BlockSpec((1, tm, a.shape[-1]), lambda b, t, e: (b, t, 0))

    def mod_chunk(chunk):
        if mod.shape[1] == 1:
            return pl.BlockSpec((1, 1, d), lambda b, t, e: (b, 0, chunk))
        return pl.BlockSpec((1, tm, d), lambda b, t, e: (b, t, chunk))

    const = lambda a: pl.BlockSpec(a.shape, lambda b, t, e: (0,) * a.ndim)
    expert = lambda r, c: pl.BlockSpec((MOE_EXP_PER_GROUP, r, c), lambda b, t, e: (e, 0, 0))
    return pl.pallas_call(
        _mla_out_moe_kernel,
        grid=(bsz, L // tm, n_exp // MOE_EXP_PER_GROUP),
        in_specs=[row(o), row(x)] + [mod_chunk(c) for c in (2, 3, 4, 5)] + [const(a) for a in out_consts]
        + [const(w_route), const(b_route), expert(d, d_ff), expert(d, d_ff), expert(d_ff, d)],
        out_specs=row(x),
        out_shape=jax.ShapeDtypeStruct((bsz, L, d), F32),
        scratch_shapes=[pltpu.VMEM((tm, d), F32), pltpu.VMEM((MOE_EXP_PER_GROUP, tm, LANES), F32),
                        pltpu.VMEM((tm, LANES), jnp.int32), pltpu.VMEM((1, tm, d), BF16)],
        compiler_params=_params("arbitrary", "arbitrary", "arbitrary"),
        name="mla_out_moe",
    )(o, x, mod, mod, mod, mod, *out_consts, w_route, b_route, w_gate, w_up, w_down)


def _kv_kernel(x_ref, innorm_ref, wc_ref, wr_ref, wrot_ref, latnorm_ref, cos_ref, sin_ref, ckv_ref, kr_ref):
    hn = _rms(x_ref[0], innorm_ref[...]).astype(BF16)
    ckv_ref[0] = _rms(_dot(hn, wc_ref[...]), latnorm_ref[...])
    kr_ref[0] = _dot(hn, wr_ref[...]) * cos_ref[...] + _dot(hn, wrot_ref[...]) * sin_ref[...]


def _rope_spec(table, tm):
    if table.shape[0] == 1:
        return pl.BlockSpec((1, LANES), lambda b, t: (0, 0))
    return pl.BlockSpec((tm, LANES), lambda b, t: (t, 0))


def _shared_kv(x, consts, cos, sin, tm):
    bsz, L, d = x.shape
    rank = consts[1].shape[1]
    row = lambda n: pl.BlockSpec((1, tm, n), lambda b, t: (b, t, 0))
    return pl.pallas_call(
        _kv_kernel,
        grid=(bsz, L // tm),
        in_specs=[row(d)] + [_const_spec(a) for a in consts] + [_rope_spec(cos, tm), _rope_spec(sin, tm)],
        out_specs=[row(rank), row(LANES)],
        out_shape=[jax.ShapeDtypeStruct((bsz, L, rank), F32), jax.ShapeDtypeStruct((bsz, L, LANES), F32)],
        compiler_params=_params("arbitrary", "arbitrary"),
        name="shared_kv",
    )(x, *consts, cos, sin)


def _mla_q_kernel(x_ref, sh_ref, sc_ref, mnorm_ref, wdq_ref, qlat_ref, wuq_ref, wuqrot_ref, qnorm_ref,
                  cos_ref, sin_ref, q_ref, *, scale):
    hm = _rms(x_ref[0], mnorm_ref[...]) * (1.0 + sc_ref[0]) + sh_ref[0]
    cq = _rms(_dot(hm.astype(BF16), wdq_ref[...]), qlat_ref[...]).astype(BF16)
    qp = _dot(cq, wuq_ref[...])
    qr = _dot(cq, wuqrot_ref[...])
    for h in range(MLA_HEADS):
        hs = slice(h * LANES, (h + 1) * LANES)
        a = qp[:, hs] * cos_ref[...] + qr[:, hs] * sin_ref[...]
        inv = lax.rsqrt(jnp.sum(a * a, axis=-1, keepdims=True) * (1.0 / MLA_QK) + NORM_EPS)
        q_ref[0, :, hs] = (a * inv * (qnorm_ref[...] * scale)).astype(q_ref.dtype)


def _mla_q(x, mod, consts, cos, sin, tm, out_dtype, scale):
    bsz, L, d = x.shape
    n = MLA_HEADS * LANES
    row = lambda w: pl.BlockSpec((1, tm, w), lambda b, t: (b, t, 0))
    return pl.pallas_call(
        functools.partial(_mla_q_kernel, scale=scale),
        grid=(bsz, L // tm),
        in_specs=[row(d), _mod_spec(mod, tm, d, 0), _mod_spec(mod, tm, d, 1)] + [_const_spec(a) for a in consts]
        + [_rope_spec(cos, tm), _rope_spec(sin, tm)],
        out_specs=row(n),
        out_shape=jax.ShapeDtypeStruct((bsz, L, n), out_dtype),
        compiler_params=_params("arbitrary", "arbitrary"),
        name="mla_q",
    )(x, mod, mod, *consts, cos, sin)


def _kv_up_kernel(ckv_ref, kr_ref, wuk_ref, knorm_ref, *rest, vt_tile):
    if vt_tile:
        wuvt_ref, k_ref, vt_ref = rest
    else:
        (k_ref,) = rest
    c = ckv_ref[0].astype(BF16)
    kn = _dot(c, wuk_ref[...])
    for h in range(MLA_HEADS):
        hs = slice(h * LANES, (h + 1) * LANES)
        a = kn[:, hs] + kr_ref[0]
        inv = lax.rsqrt(jnp.sum(a * a, axis=-1, keepdims=True) * (1.0 / MLA_QK) + NORM_EPS)
        k_ref[0, :, hs] = (a * inv * knorm_ref[...]).astype(k_ref.dtype)
    if vt_tile:
        for s in range(c.shape[0] // vt_tile):
            vt = _dot_nt(wuvt_ref[...], c[s * vt_tile:(s + 1) * vt_tile])
            vt_ref[0, s] = vt.astype(vt_ref.dtype)


def _kv_up(ckv, kr_slot, consts, tm, out_dtype, vt_tile=0):
    bsz, L, rank = ckv.shape
    n = MLA_HEADS * LANES
    row = lambda w: pl.BlockSpec((1, tm, w), lambda b, t: (b, t, 0))
    out_specs = [row(n)]
    out_shape = [jax.ShapeDtypeStruct((bsz, L, n), out_dtype)]
    if vt_tile:
        nv = consts[-1].shape[0]
        out_specs.append(pl.BlockSpec((1, tm // vt_tile, nv, vt_tile), lambda b, t: (b, t, 0, 0)))
        out_shape.append(jax.ShapeDtypeStruct((bsz, L // vt_tile, nv, vt_tile), out_dtype))
    return pl.pallas_call(
        functools.partial(_kv_up_kernel, vt_tile=vt_tile),
        grid=(bsz, L // tm),
        in_specs=[row(rank), row(LANES)] + [_const_spec(a) for a in consts],
        out_specs=out_specs,
        out_shape=out_shape,
        compiler_params=_params("arbitrary", "arbitrary"),
        name="kv_up",
    )(ckv, kr_slot, *consts)


def _mla_pre_kernel(x_ref, sh_ref, sc_ref, *refs, scale, vt_tile, n_kv, n_q, n_up):
    kv_consts = refs[:n_kv]
    q_consts = refs[n_kv:n_kv + n_q]
    up_consts = refs[n_kv + n_q:n_kv + n_q + n_up]
    cos_ref, sin_ref, ckv_ref, kr_ref, q_ref, k_ref, vt_ref = refs[n_kv + n_q + n_up:]
    _kv_kernel(x_ref, *kv_consts, cos_ref, sin_ref, ckv_ref, kr_ref)
    _mla_q_kernel(x_ref, sh_ref, sc_ref, *q_consts, cos_ref, sin_ref, q_ref, scale=scale)
    _kv_up_kernel(ckv_ref, kr_ref, *up_consts, k_ref, vt_ref, vt_tile=vt_tile)


def _mla_pre(x, mod, kv_consts, q_consts, up_consts, cos, sin, tm, scale, vt_tile):
    bsz, L, d = x.shape
    rank = kv_consts[1].shape[1]
    n = MLA_HEADS * LANES
    nv = up_consts[-1].shape[0]
    consts = kv_consts + q_consts + up_consts
    row = lambda w: pl.BlockSpec((1, tm, w), lambda b, t: (b, t, 0))
    return pl.pallas_call(
        functools.partial(_mla_pre_kernel, scale=scale, vt_tile=vt_tile, n_kv=len(kv_consts), n_q=len(q_consts),
                          n_up=len(up_consts)),
        grid=(bsz, L // tm),
        in_specs=[row(d), _mod_spec(mod, tm, d, 0), _mod_spec(mod, tm, d, 1)] + [_const_spec(a) for a in consts]
        + [_rope_spec(cos, tm), _rope_spec(sin, tm)],
        out_specs=[row(rank), row(LANES), row(n), row(n),
                   pl.BlockSpec((1, tm // vt_tile, nv, vt_tile), lambda b, t: (b, t, 0, 0))],
        out_shape=[jax.ShapeDtypeStruct((bsz, L, rank), F32), jax.ShapeDtypeStruct((bsz, L, LANES), F32),
                   jax.ShapeDtypeStruct((bsz, L, n), BF16), jax.ShapeDtypeStruct((bsz, L, n), BF16),
                   jax.ShapeDtypeStruct((bsz, L // vt_tile, nv, vt_tile), BF16)],
        compiler_params=_params("arbitrary", "arbitrary"),
        name="mla_pre",
    )(x, mod, mod, *consts, cos, sin)


def _flash_kernel(q_ref, k_ref, vt_ref, o_ref, *, tq, tk):
    qi = pl.program_id(2)
    sub = tk // tq
    n_full = (qi * tq) // tk

    def update(carry, c, masked):
        new = []
        off = pl.multiple_of(c * tk, tk)
        scores = [_dot_nt(k_ref[0, pl.ds(off, tk), hh * LANES:(hh + 1) * LANES],
                          q_ref[0, :, hh * LANES:(hh + 1) * LANES]) for hh in range(len(carry))]
        for hh, (m, acc) in enumerate(carry):
            s = scores[hh]
            if masked:
                s = jnp.where(key_minus_query <= qi * tq - c * tk, s, -jnp.inf)
            m_new = jnp.maximum(m, jnp.max(s, axis=0, keepdims=True))
            pb = jnp.exp2(s - m_new).astype(BF16)
            acc = acc * jnp.exp2(m - m_new)
            for t in range(sub):
                vt = vt_ref[0, c * sub + t, hh * MLA_V:(hh + 1) * MLA_V, :]
                acc = acc + _dot(jnp.concatenate([vt, ones], axis=0), pb[t * tq:(t + 1) * tq])
            new.append((m_new, acc))
        return tuple(new)

    ones = jnp.ones((16, tq), BF16)
    key_minus_query = (lax.broadcasted_iota(jnp.int32, (tk, tq), 0)
                       - lax.broadcasted_iota(jnp.int32, (tk, tq), 1))
    one = (jnp.full((1, tq), NEG_BIG, F32), jnp.zeros((MLA_V + 16, tq), F32))
    carry = lax.fori_loop(0, n_full, lambda c, cr: update(cr, c, False), (one,) * FLASH_HEADS)
    carry = update(carry, n_full, True)
    out_t = jnp.concatenate([acc[:MLA_V] / acc[MLA_V:MLA_V + 1] for (_, acc) in carry], axis=0)
    o_ref[0] = out_t.T.astype(o_ref.dtype)


def _flash(q, k, vt, tq):
    bsz, S, _ = q.shape
    nh = FLASH_HEADS
    qspec = pl.BlockSpec((1, tq, nh * LANES), lambda b, h, i: (b, i, h))
    kspec = pl.BlockSpec((1, S, nh * LANES), lambda b, h, i: (b, 0, h))
    vspec = pl.BlockSpec((1, S // tq, nh * MLA_V, tq), lambda b, h, i: (b, 0, h, 0))
    return pl.pallas_call(
        functools.partial(_flash_kernel, tq=tq, tk=min(FLASH_KEYS, S)),
        grid=(bsz, MLA_HEADS // nh, S // tq),
        in_specs=[qspec, kspec, vspec],
        out_specs=pl.BlockSpec((1, tq, nh * MLA_V), lambda b, h, i: (b, i, h)),
        out_shape=jax.ShapeDtypeStruct((bsz, S, MLA_HEADS * MLA_V), BF16),
        compiler_params=_params("arbitrary", "arbitrary", "arbitrary"),
        name="flash",
    )(q, k, vt)


def _paged_kernel(pt_ref, ckv_hbm, krt_hbm, q_ref, knew_ref, cnew_ref, knorm_ref, wukt_ref, wukt_slot_ref,
                  wuv_slot_ref, o_ref, m_scr, l_scr, acc_scr, lhs_scr, c_scr, cbuf, krbuf, sems, *, n_pages):
    b, j = pl.program_id(0), pl.program_id(1)
    steps = pl.num_programs(1)
    step = b * steps + j
    last_step = pl.num_programs(0) * steps - 1
    slot = step % 2
    nh = MLA_HEADS
    n_kn = nh * MLA_NOPE
    page = ckv_hbm.shape[1]
    tile = 2 * page
    head = lax.broadcasted_iota(jnp.int32, (nh, 1), 0)

    def page_copies(pid, p, to_slot):
        return (pltpu.make_async_copy(ckv_hbm.at[pid], cbuf.at[to_slot, pl.ds(p * page, page), :],
                                      sems.at[to_slot, 0]),
                pltpu.make_async_copy(krt_hbm.at[pid], krbuf.at[to_slot, :, pl.ds(p * page, page)],
                                      sems.at[to_slot, 1]))

    def start_pages(for_step, to_slot, positions):
        fb, fj = for_step // steps, for_step % steps
        for p in positions:
            for cp in page_copies(pt_ref[fb, fj * n_pages + p], p, to_slot):
                cp.start()

    def wait_pages(in_slot):
        for p in range(n_pages):
            for cp in page_copies(0, p, in_slot):
                cp.wait()

    @pl.when(step == 0)
    def _():
        start_pages(step, slot, range(n_pages))

    qg = q_ref[0] * knorm_ref[...]

    @pl.when(j == 0)
    def _():
        m_scr[...] = jnp.full_like(m_scr, NEG_BIG)
        l_scr[...] = jnp.zeros_like(l_scr)
        acc_scr[...] = jnp.zeros_like(acc_scr)
        qb = qg.astype(BF16)
        qabs = jnp.zeros((nh, lhs_scr.shape[1]), F32)
        for h in range(nh):
            qabs = qabs + jnp.where(head == h, _dot(qb, wukt_slot_ref[h]), 0.0)
        lhs_scr[:n_kn, :] = wukt_ref[...]
        lhs_scr[n_kn:, :] = jnp.concatenate([qabs, jnp.zeros_like(qabs)], axis=0).astype(BF16)

    wait_pages(slot)
    nxt = jnp.minimum(step + 1, last_step)
    q_rope = qg[:, MLA_NOPE:MLA_QK].astype(BF16)
    scores = []
    for i in range(n_pages // 2):
        start_pages(nxt, 1 - slot, (2 * i, 2 * i + 1))
        c2 = cbuf[slot, i * tile:(i + 1) * tile, :].astype(BF16)
        c_scr[i * tile:(i + 1) * tile, :] = c2
        both = _dot_nt(lhs_scr[...], c2)
        knt = both[:n_kn]
        ss = jnp.sum((knt * knt).reshape(nh, MLA_NOPE, tile), axis=1)
        kr2 = krbuf[slot, :, i * tile:(i + 1) * tile]
        ss = ss + jnp.sum(kr2 * kr2, axis=0, keepdims=True)
        num = both[n_kn:n_kn + nh] + _dot(q_rope, kr2.astype(BF16))
        scores.append(num * lax.rsqrt(ss * (1.0 / MLA_QK) + NORM_EPS))
    s = jnp.concatenate(scores, axis=1)
    m = m_scr[...]
    m_new = jnp.maximum(m, jnp.max(s, axis=-1, keepdims=True))
    pr = jnp.exp(s - m_new)
    corr = jnp.exp(m - m_new)
    l = l_scr[...] * corr + jnp.sum(pr, axis=-1, keepdims=True)
    pb = pr.astype(BF16)
    pv = [None, None]
    for i in range(n_pages // 2):
        part = _dot(pb[:, i * tile:(i + 1) * tile], c_scr[i * tile:(i + 1) * tile, :])
        pv[i % 2] = part if pv[i % 2] is None else pv[i % 2] + part
    acc = acc_scr[...] * corr + (pv[0] + pv[1])
    m_scr[...] = m_new
    l_scr[...] = l
    acc_scr[...] = acc

    @pl.when(j == steps - 1)
    def _():
        s_new = jnp.sum(q_ref[0] * knew_ref[0], axis=-1, keepdims=True)
        m_fin = jnp.maximum(m_new, s_new)
        corr_fin = jnp.exp(m_new - m_fin)
        p_new = jnp.exp(s_new - m_fin)
        l_fin = l * corr_fin + p_new
        o_lat = ((acc * corr_fin + p_new * cnew_ref[0]) / l_fin).astype(BF16)
        out = jnp.zeros((nh, LANES), F32)
        for h in range(nh):
            out = out + jnp.where(head == h, _dot(o_lat, wuv_slot_ref[h]), 0.0)
        o_ref[0] = out

    @pl.when(step == last_step)
    def _():
        wait_pages(1 - slot)


def _paged_attention(q, k_new, c_new, cache_ckv, cache_krope_t, page_table, consts, n_pages):
    nb, nh, _ = q.shape
    _, page, rank = cache_ckv.shape
    rope = cache_krope_t.shape[1]
    steps = page_table.shape[1] // n_pages
    assert n_pages % 2 == 0 and steps * n_pages == page_table.shape[1]
    per_b = lambda a: pl.BlockSpec((1,) + a.shape[1:], lambda b, j, pt: (b,) + (0,) * (a.ndim - 1))
    const = lambda a: pl.BlockSpec(a.shape, lambda b, j, pt: (0,) * a.ndim)
    hbm = pl.BlockSpec(memory_space=pl.ANY)
    grid_spec = pltpu.PrefetchScalarGridSpec(
        num_scalar_prefetch=1,
        grid=(nb, steps),
        in_specs=[hbm, hbm, per_b(q), per_b(k_new), per_b(c_new)] + [const(a) for a in consts],
        out_specs=pl.BlockSpec((1, nh, LANES), lambda b, j, pt: (b, 0, 0)),
        scratch_shapes=[pltpu.VMEM((nh, 1), F32), pltpu.VMEM((nh, 1), F32), pltpu.VMEM((nh, rank), F32),
                        pltpu.VMEM((nh * MLA_NOPE + 2 * nh, rank), BF16),
                        pltpu.VMEM((n_pages * page, rank), BF16),
                        pltpu.VMEM((2, n_pages * page, rank), F32),
                        pltpu.VMEM((2, rope, n_pages * page), F32),
                        pltpu.SemaphoreType.DMA((2, 2))],
    )
    return pl.pallas_call(
        functools.partial(_paged_kernel, n_pages=n_pages),
        grid_spec=grid_spec,
        out_shape=jax.ShapeDtypeStruct((nb, nh, LANES), F32),
        compiler_params=_params("arbitrary", "arbitrary"),
        name="paged_attn",
    )(page_table, cache_ckv, cache_krope_t, q, k_new, c_new, *consts)


def _rotate_half_cols(w):
    half = w.shape[-1] // 2
    return jnp.concatenate([-w[..., half:], w[..., :half]], axis=-1)


def _slot(nope, rope):
    ref = nope if nope is not None else rope
    lead = ref.shape[:-1]
    z = lambda n: jnp.zeros(lead + (n,), ref.dtype)
    a = nope if nope is not None else z(MLA_NOPE)
    b = rope if rope is not None else z(MLA_ROPE)
    return jnp.concatenate([a, b, z(LANES - MLA_QK)], axis=-1)


def _rope_tables(pos):
    inv = ROPE_THETA ** (-jnp.arange(0, MLA_ROPE, 2, dtype=F32) / MLA_ROPE)
    ang = pos.astype(F32)[:, None] * inv[None, :]
    ang = jnp.concatenate([ang, ang], axis=-1)
    n = pos.shape[0]
    cos = _slot(jnp.ones((n, MLA_NOPE), F32), jnp.cos(ang))
    sin = _slot(jnp.zeros((n, MLA_NOPE), F32), jnp.sin(ang))
    return cos, sin


def _prepare(w):
    d = w['mix_norm'].shape[-1]
    p = {}
    p['route_w'] = [jnp.pad(jnp.concatenate([w['moe_w_group'][l], w['moe_w_router'][l]], axis=-1),
                            ((0, 0), (0, LANES - MOE_GROUPS - MOE_EXPERTS))).astype(BF16) for l in range(2)]
    p['route_b'] = [jnp.pad(jnp.concatenate([w['moe_b_group'][l], w['moe_b_router'][l]], axis=-1),
                            (0, LANES - MOE_GROUPS - MOE_EXPERTS)).reshape(1, LANES) for l in range(2)]
    p['moe_gate'] = w['moe_w_gate'].astype(BF16)
    p['moe_up'] = w['moe_w_up'].astype(BF16)
    p['moe_down'] = w['moe_w_down'].astype(BF16)
    p['gla_out'] = [w['gla_out_norm'][0].reshape(1, -1), w['gla_w_out'][0].astype(BF16),
                    w['ffn_norm'][0].reshape(1, d)]
    rank = w['kv_lat_norm'].shape[0]
    w_r = w['kv_w_dkv'][:, rank:]
    p['kv'] = [w['kv_in_norm'].reshape(1, d), w['kv_w_dkv'][:, :rank].astype(BF16),
               _slot(None, w_r).astype(BF16), _slot(None, _rotate_half_cols(w_r)).astype(BF16),
               w['kv_lat_norm'].reshape(1, rank)]
    q_rank = w['mla_w_dq'].shape[-1]
    wuq = w['mla_w_uq'][0].reshape(q_rank, MLA_HEADS, MLA_QK)
    wuq_n, wuq_r = wuq[..., :MLA_NOPE], wuq[..., MLA_NOPE:]
    p['mla_q'] = [w['mix_norm'][1].reshape(1, d), w['mla_w_dq'][0].astype(BF16),
                  w['mla_q_lat_norm'][0].reshape(1, q_rank),
                  _slot(wuq_n, wuq_r).reshape(q_rank, -1).astype(BF16),
                  _slot(None, _rotate_half_cols(wuq_r)).reshape(q_rank, -1).astype(BF16),
                  jnp.pad(w['mla_q_norm'][0], (0, LANES - MLA_QK)).reshape(1, LANES)]
    knorm = jnp.pad(w['kv_k_norm'], (0, LANES - MLA_QK)).reshape(1, LANES)
    wuk, wuv = w['kv_w_uk'], w['kv_w_uv']
    p['kv_up'] = [_slot(wuk, None).reshape(rank, -1).astype(BF16), knorm]
    p['wuv_t'] = wuv.reshape(rank, -1).T.astype(BF16)
    wukt = wuk.transpose(1, 2, 0)
    p['paged'] = [knorm, wukt.reshape(MLA_HEADS * MLA_NOPE, rank).astype(BF16),
                  jnp.pad(wukt, ((0, 0), (0, LANES - MLA_NOPE), (0, 0))).astype(BF16),
                  jnp.pad(wuv.transpose(1, 0, 2), ((0, 0), (0, 0), (0, LANES - MLA_V))).astype(BF16)]
    wo = w['mla_w_out'][0].reshape(MLA_HEADS, MLA_V, d)
    p['mla_out_slot'] = [jnp.pad(wo, ((0, 0), (0, LANES - MLA_V), (0, 0))).reshape(MLA_HEADS * LANES, d)
                         .astype(BF16), w['ffn_norm'][1].reshape(1, d)]
    p['mla_out'] = [w['mla_w_out'][0].astype(BF16), w['ffn_norm'][1].reshape(1, d)]
    return p


def _moe_layer(l, hf, x1, mod, p):
    return _moe(hf, x1, mod, p['route_w'][l], p['route_b'][l], p['moe_gate'][l], p['moe_up'][l],
                p['moe_down'][l], min(MOE_ROWS, x1.shape[1]))


def _trunk(x, mods, pos, gla_s0, paged, w, p, tm):
    cos, sin = _rope_tables(pos)
    gla_w = (w['mix_norm'][0], w['gla_w_in'][0], w['gla_w_gate_up'][0], w['gla_b_gate'][0])
    if paged is None:
        x1, hf, s_fin = _gla_layer(x, mods[0], gla_s0, _gla_proj_consts(*gla_w), p['gla_out'], tm)
    else:
        q, k, v, r, gk = _gla_proj(x, mods[0], *gla_w, tm)
        o, s_fin = _gla_step(q[0], k[0], v[0], gk[0], gla_s0)
        x1, hf = _mixer_out(functools.partial(_gla_out_kernel, dv=v.shape[-1] // GLA_HEADS), [o[None], r], x,
                            mods[0], p['gla_out'], tm, "gla_out")
    x2 = _moe_layer(0, hf, x1, mods[0], p)
    if paged is None:
        tq = min(FLASH_T, x.shape[1])
        ckv, kr_slot, qh, kh, vt = _mla_pre(x2, mods[1], p['kv'], p['mla_q'], p['kv_up'] + [p['wuv_t']], cos, sin,
                                            tm, MLA_QK ** -0.5 * math.log2(math.e), tq)
        o = _flash(qh, kh, vt, tq)
        out_consts = p['mla_out']
    else:
        n = x.shape[1]
        ckv, kr_slot = _shared_kv(x2, p['kv'], cos, sin, tm)
        qh = _mla_q(x2, mods[1], p['mla_q'], cos, sin, tm, F32, MLA_QK ** -0.5)
        (kh,) = _kv_up(ckv, kr_slot, p['kv_up'], tm, F32)
        o = _paged_attention(qh.reshape(n, MLA_HEADS, LANES), kh.reshape(n, MLA_HEADS, LANES),
                             ckv.reshape(n, 1, -1), paged[0], jnp.swapaxes(paged[1], 1, 2), paged[2], p['paged'],
                             min(PAGES_PER_STEP, paged[2].shape[1]))
        o = o.reshape(1, n, MLA_HEADS * LANES).astype(BF16)
        out_consts = p['mla_out_slot']
    y = _mla_out_moe(o, x2, mods[1], out_consts, p['route_w'][1], p['route_b'][1], p['moe_gate'][1],
                     p['moe_up'][1], p['moe_down'][1], min(MOE_ROWS, x2.shape[1]))
    return y, s_fin, ckv, kr_slot[..., MLA_NOPE:MLA_QK]


def _forward(x_prompt, x_sample, c_prompt, c_sample, state_gla, cache_ckv, cache_krope, page_table, w, past_len):
    bsz, seq, d = x_prompt.shape
    nd, dl, _ = x_sample.shape
    assert dl == 1, "decode path handles one new token per sequence"
    p = _prepare(w)
    mod = _ada_mod(jnp.concatenate([c_prompt, c_sample], axis=0), w['ada_w'], w['ada_b'])
    mods_p = [mod[l, :bsz].reshape(bsz, 1, -1) for l in range(2)]
    mods_s = [mod[l, bsz:].reshape(1, nd, -1) for l in range(2)]
    s0_p = jnp.zeros((bsz,) + state_gla.shape[2:], state_gla.dtype)
    y_p, s_p, ckv_p, kr_p = _trunk(x_prompt, mods_p, jnp.arange(seq), s0_p, None, w, p, min(512, seq))
    y_s, s_s, ckv_s, kr_s = _trunk(x_sample.reshape(1, nd, d), mods_s, jnp.full((1,), past_len), state_gla[0],
                                   (cache_ckv, cache_krope, page_table), w, p, nd)
    return (y_p, y_s.reshape(nd, 1, d), s_p[None], s_s[None], ckv_p, kr_p,
            ckv_s.reshape(nd, 1, -1), kr_s.reshape(nd, 1, -1))


def kernel(x_prompt, x_sample, c_prompt, c_sample, state_gla, cache_ckv, cache_krope, page_table, ada_w, ada_b, mix_norm, ffn_norm, gla_w_in, gla_w_gate_up, gla_b_gate, gla_out_norm, gla_w_out, mla_w_dq, mla_q_lat_norm, mla_w_uq, mla_q_norm, mla_w_out, kv_in_norm, kv_w_dkv, kv_lat_norm, kv_w_uk, kv_w_uv, kv_k_norm, moe_w_group, moe_b_group, moe_w_router, moe_b_router, moe_w_gate, moe_w_up, moe_w_down):
    w = dict(ada_w=ada_w, ada_b=ada_b, mix_norm=mix_norm, ffn_norm=ffn_norm,
             gla_w_in=gla_w_in, gla_w_gate_up=gla_w_gate_up, gla_b_gate=gla_b_gate,
             gla_out_norm=gla_out_norm, gla_w_out=gla_w_out,
             mla_w_dq=mla_w_dq, mla_q_lat_norm=mla_q_lat_norm, mla_w_uq=mla_w_uq,
             mla_q_norm=mla_q_norm, mla_w_out=mla_w_out,
             kv_in_norm=kv_in_norm, kv_w_dkv=kv_w_dkv, kv_lat_norm=kv_lat_norm,
             kv_w_uk=kv_w_uk, kv_w_uv=kv_w_uv, kv_k_norm=kv_k_norm,
             moe_w_group=moe_w_group, moe_b_group=moe_b_group, moe_w_router=moe_w_router,
             moe_b_router=moe_b_router, moe_w_gate=moe_w_gate, moe_w_up=moe_w_up, moe_w_down=moe_w_down)
    past_len = page_table.shape[1] * cache_ckv.shape[1]
    return _forward(x_prompt, x_sample, c_prompt, c_sample, state_gla, cache_ckv, cache_krope, page_table, w,
                    past_len)
```

```python
import functools
import math

import jax
import jax.numpy as jnp
from jax import lax
from jax.experimental import pallas as pl
from jax.experimental.pallas import tpu as pltpu

F32 = jnp.float32
BF16 = jnp.bfloat16

NORM_EPS = 1e-6
GLA_HEADS = 4
GLA_GATE_NORM = 16.0
GLA_CHUNK = 256
GLA_SAFE_EXPONENT = 80.0
MLA_HEADS = 8
MLA_NOPE = 64
MLA_ROPE = 32
MLA_QK = MLA_NOPE + MLA_ROPE
MLA_V = 64
ROPE_THETA = 10000.0
MOE_GROUPS = 4
MOE_EXP_PER_GROUP = 4
MOE_EXPERTS = MOE_GROUPS * MOE_EXP_PER_GROUP
LANES = 128
SUBLANES = 8
NEG_BIG = -1e30
FLASH_T = 512
FLASH_KEYS = 1024
MOE_ROWS = 1024
PAGES_PER_STEP = 64
FLASH_HEADS = 4
VMEM_LIMIT = 52 * 1024 * 1024


def _params(*sem):
    return pltpu.CompilerParams(dimension_semantics=sem, vmem_limit_bytes=VMEM_LIMIT)


def _dot(a, b):
    return jnp.dot(a, b, preferred_element_type=F32)


def _dot_nt(a, b):
    return lax.dot_general(a, b, (((1,), (1,)), ((), ())), preferred_element_type=F32)


def _dot_tn(a, b):
    return lax.dot_general(a, b, (((0,), (0,)), ((), ())), preferred_element_type=F32)


def _rms(x, g):
    return x * lax.rsqrt(jnp.mean(x * x, axis=-1, keepdims=True) + NORM_EPS) * g


def _silu(x):
    return x / (1.0 + jnp.exp(-x))


def _split_bf16(x):
    hi = x.astype(BF16)
    lo = (x - hi.astype(F32)).astype(BF16)
    return hi, lo


def _ada_kernel(c_ref, w_ref, b_ref, o_ref):
    a = _silu(c_ref[...]).astype(BF16)
    o_ref[0] = _dot(a, w_ref[0].astype(BF16)) + b_ref[0]


def _ada_mod(c, ada_w, ada_b):
    depth, d, n = ada_w.shape
    r = c.shape[0]
    tn = 1536
    return pl.pallas_call(
        _ada_kernel,
        grid=(depth, n // tn),
        in_specs=[
            pl.BlockSpec((r, d), lambda l, j: (0, 0)),
            pl.BlockSpec((1, d, tn), lambda l, j: (l, 0, j)),
            pl.BlockSpec((1, 1, tn), lambda l, j: (l, 0, j)),
        ],
        out_specs=pl.BlockSpec((1, r, tn), lambda l, j: (l, 0, j)),
        out_shape=jax.ShapeDtypeStruct((depth, r, n), F32),
        compiler_params=_params("arbitrary", "arbitrary"),
        name="ada_mod",
    )(c, ada_w, ada_b.reshape(depth, 1, n))


def _mod_spec(mod, tm, d, chunk):
    if mod.shape[1] == 1:
        return pl.BlockSpec((1, 1, d), lambda b, t: (b, 0, chunk))
    return pl.BlockSpec((1, tm, d), lambda b, t: (b, t, chunk))


def _const_spec(arr):
    nd = arr.ndim
    return pl.BlockSpec(arr.shape, lambda b, t: (0,) * nd)


def _gla_proj_kernel(x_ref, sh_ref, sc_ref, g_ref, wq_ref, wk_ref, wv_ref, wr_ref, wg_ref, wgu_ref, bg_ref,
                     q_ref, k_ref, v_ref, r_ref, gk_ref, *, q_scale):
    hm = _rms(x_ref[0], g_ref[...]) * (1.0 + sc_ref[0]) + sh_ref[0]
    hb = hm.astype(BF16)
    q_ref[0] = _dot(hb, wq_ref[...]) * q_scale
    k_ref[0] = _dot(hb, wk_ref[...])
    v_ref[0] = _dot(hb, wv_ref[...]).astype(BF16)
    r_ref[0] = _dot(hb, wr_ref[...])
    g_low = _dot(hb, wg_ref[...])
    z = _dot(g_low.astype(BF16), wgu_ref[...]) + bg_ref[...]
    log_sig = jnp.minimum(z, 0.0) - jnp.log(1.0 + jnp.exp(-jnp.abs(z)))
    gk_ref[0] = log_sig * (1.0 / GLA_GATE_NORM)


def _gla_proj_consts(norm_g, w_in, w_gate_up, b_gate):
    hk = w_gate_up.shape[1]
    hv = (w_in.shape[1] - 2 * hk - w_gate_up.shape[0]) // 2
    cuts = [0, hk, 2 * hk, 2 * hk + hv, 2 * hk + 2 * hv, w_in.shape[1]]
    pieces = [w_in[:, a:b].astype(BF16) for a, b in zip(cuts[:-1], cuts[1:])]
    return [norm_g.reshape(1, -1)] + pieces + [w_gate_up.astype(BF16), b_gate.reshape(1, hk)]


def _gla_proj(x, mod, norm_g, w_in, w_gate_up, b_gate, tm):
    bsz, L, d = x.shape
    consts = _gla_proj_consts(norm_g, w_in, w_gate_up, b_gate)
    hk, hv = consts[1].shape[1], consts[3].shape[1]
    row = lambda n: pl.BlockSpec((1, tm, n), lambda b, t: (b, t, 0))
    return pl.pallas_call(
        functools.partial(_gla_proj_kernel, q_scale=float((hk // GLA_HEADS) ** -0.5)),
        grid=(bsz, L // tm),
        in_specs=[row(d), _mod_spec(mod, tm, d, 0), _mod_spec(mod, tm, d, 1)] + [_const_spec(a) for a in consts],
        out_specs=[row(hk), row(hk), row(hv), row(hv), row(hk)],
        out_shape=[
            jax.ShapeDtypeStruct((bsz, L, hk), F32),
            jax.ShapeDtypeStruct((bsz, L, hk), F32),
            jax.ShapeDtypeStruct((bsz, L, hv), BF16),
            jax.ShapeDtypeStruct((bsz, L, hv), F32),
            jax.ShapeDtypeStruct((bsz, L, hk), F32),
        ],
        compiler_params=_params("arbitrary", "arbitrary"),
        name="gla_proj",
    )(x, mod, mod, *consts)


def _gla_layer_kernel(x_ref, sh_ref, sc_ref, ga_ref, shf_ref, scf_ref, s0_ref, mnorm_ref, wq_ref, wk_ref, wv_ref,
                      wr_ref, wg_ref, wgu_ref, bg_ref, onorm_ref, wout_ref, fnorm_ref,
                      x1_ref, hf_ref, sf_ref, st_scr, o_scr, b_scr, k_scr, *, chunk, q_scale):
    t = pl.program_id(1)
    n_heads, dv, dk = st_scr.shape

    @pl.when(t == 0)
    def _():
        for h in range(n_heads):
            st_scr[h] = s0_ref[0, h].T

    x = x_ref[0]
    tm = x.shape[0]
    n_chunks = tm // chunk
    hb = (_rms(x, mnorm_ref[...]) * (1.0 + sc_ref[0]) + sh_ref[0]).astype(BF16)
    q = _dot(hb, wq_ref[...]) * q_scale
    k = _dot(hb, wk_ref[...])
    v = _dot(hb, wv_ref[...]).astype(BF16)
    r = _dot(hb, wr_ref[...])
    z = _dot(_dot(hb, wg_ref[...]).astype(BF16), wgu_ref[...]) + bg_ref[...]
    gk = (jnp.minimum(z, 0.0) - jnp.log(1.0 + jnp.exp(-jnp.abs(z)))) * (1.0 / GLA_GATE_NORM)
    tri = jnp.where(lax.broadcasted_iota(jnp.int32, (tm, tm), 0) >= lax.broadcasted_iota(jnp.int32, (tm, tm), 1),
                    1.0, 0.0).astype(BF16)
    g_hi, g_lo = _split_bf16(gk)
    cum = _dot(tri, g_hi) + _dot(tri, g_lo)
    causal = (lax.broadcasted_iota(jnp.int32, (chunk, chunk), 0)
              >= lax.broadcasted_iota(jnp.int32, (chunk, chunk), 1))

    def local_gate(c, cols):
        b = cum[c * chunk:(c + 1) * chunk, cols]
        return b - cum[c * chunk - 1:c * chunk, cols] if c else b

    total_decay = jnp.concatenate([local_gate(c, slice(None))[chunk - 1:chunk] for c in range(n_chunks)], axis=0)
    factored_ok = jnp.max(-total_decay) <= GLA_SAFE_EXPONENT

    @pl.when(factored_ok)
    def _():
        for h in range(n_heads):
            ks = slice(h * dk, (h + 1) * dk)
            vs = slice(h * dv, (h + 1) * dv)
            st = st_scr[h]
            for c in range(n_chunks):
                rows = slice(c * chunk, (c + 1) * chunk)
                b = local_gate(c, ks)
                qt = (q[rows, ks] * jnp.exp(b)).astype(BF16)
                kt = (k[rows, ks] * jnp.exp(-b)).astype(BF16)
                vc = v[rows, vs]
                attn = jnp.where(causal, _dot_nt(qt, kt), 0.0).astype(BF16)
                o_scr[rows, vs] = _dot(attn, vc) + _dot_nt(qt, st.astype(BF16))
                st = (st + _dot_tn(vc, kt)) * jnp.exp(b[chunk - 1:chunk, :])
            st_scr[h] = st

    @pl.when(jnp.logical_not(factored_ok))
    def _():
        k_scr[...] = k
        lane = lax.broadcasted_iota(jnp.int32, (chunk, chunk), 1)
        for h in range(n_heads):
            ks = slice(h * dk, (h + 1) * dk)
            vs = slice(h * dv, (h + 1) * dv)
            st = st_scr[h]
            for c in range(n_chunks):
                rows = slice(c * chunk, (c + 1) * chunk)
                b = local_gate(c, ks)
                b_scr[rows, ks] = b
                qc = q[rows, ks]

                def key_rows(grp, attn, c=c, ks=ks, b=b, qc=qc):
                    first = pl.multiple_of(c * chunk + grp * SUBLANES, SUBLANES)
                    b_grp = b_scr[pl.ds(first, SUBLANES), ks]
                    k_grp = k_scr[pl.ds(first, SUBLANES), ks]
                    for i in range(SUBLANES):
                        decay = jnp.exp(jnp.minimum(b - b_grp[i:i + 1], 0.0))
                        col = jnp.sum(qc * k_grp[i:i + 1] * decay, axis=-1, keepdims=True)
                        attn = jnp.where(lane == grp * SUBLANES + i, col, attn)
                    return attn

                attn = lax.fori_loop(0, chunk // SUBLANES, key_rows, jnp.zeros((chunk, chunk), F32))
                attn = jnp.where(causal, attn, 0.0).astype(BF16)
                vc = v[rows, vs]
                b_end = b[chunk - 1:chunk, :]
                qt = (qc * jnp.exp(b)).astype(BF16)
                o_scr[rows, vs] = _dot(attn, vc) + _dot_nt(qt, st.astype(BF16))
                kd = (k[rows, ks] * jnp.exp(b_end - b)).astype(BF16)
                st = st * jnp.exp(b_end) + _dot_tn(vc, kd)
            st_scr[h] = st

    parts = []
    for h in range(n_heads):
        vs = slice(h * dv, (h + 1) * dv)
        parts.append((_rms(o_scr[:, vs], onorm_ref[...]) * _silu(r[:, vs])).astype(BF16))
    m = _dot(jnp.concatenate(parts, axis=-1), wout_ref[...])
    _residual_ffn_norm(x, m, ga_ref[0], fnorm_ref[...], scf_ref[0], shf_ref[0], x1_ref, hf_ref)

    @pl.when(t == pl.num_programs(1) - 1)
    def _():
        for h in range(n_heads):
            sf_ref[0, h] = st_scr[h].T


def _gla_layer(x, mod, s0, proj_consts, out_consts, tm):
    bsz, L, d = x.shape
    _, n_heads, dk, dv = s0.shape
    consts = proj_consts + out_consts
    row = lambda a: pl.BlockSpec((1, tm, a.shape[-1]), lambda b, t: (b, t, 0))
    sspec = pl.BlockSpec((1, n_heads, dk, dv), lambda b, t: (b, 0, 0, 0))
    return pl.pallas_call(
        functools.partial(_gla_layer_kernel, chunk=math.gcd(tm, GLA_CHUNK), q_scale=float(dk ** -0.5)),
        grid=(bsz, L // tm),
        in_specs=[row(x)] + [_mod_spec(mod, tm, d, c) for c in range(5)] + [sspec]
        + [_const_spec(a) for a in consts],
        out_specs=[row(x), row(x), sspec],
        out_shape=[jax.ShapeDtypeStruct((bsz, L, d), F32), jax.ShapeDtypeStruct((bsz, L, d), BF16),
                   jax.ShapeDtypeStruct(s0.shape, F32)],
        scratch_shapes=[pltpu.VMEM((n_heads, dv, dk), F32), pltpu.VMEM((tm, n_heads * dv), F32),
                        pltpu.VMEM((tm, n_heads * dk), F32), pltpu.VMEM((tm, n_heads * dk), F32)],
        compiler_params=_params("arbitrary", "arbitrary"),
        name="gla_layer",
    )(x, mod, mod, mod, mod, mod, s0, *consts)


def _gla_step_kernel(q_ref, k_ref, v_ref, g_ref, s_ref, o_ref, sn_ref, *, nb, dk, dv):
    eye = (lax.broadcasted_iota(jnp.int32, (dk, dk), 0) == lax.broadcasted_iota(jnp.int32, (dk, dk), 1))

    def column(row_vec):
        return jnp.sum(jnp.where(eye, row_vec, 0.0), axis=1, keepdims=True)

    for j in range(nb):
        for h in range(GLA_HEADS):
            ks = slice(h * dk, (h + 1) * dk)
            vs = slice(h * dv, (h + 1) * dv)
            decay = column(jnp.exp(g_ref[j:j + 1, ks]))
            kc = column(k_ref[j:j + 1, ks])
            qc = column(q_ref[j:j + 1, ks])
            s_new = s_ref[j, h] * decay + kc * v_ref[j:j + 1, vs].astype(F32)
            sn_ref[j, h] = s_new
            o_ref[j:j + 1, vs] = jnp.sum(qc * s_new, axis=0, keepdims=True)


def _gla_step(q, k, v, gk, s0):
    n, hk = q.shape
    hv = v.shape[-1]
    dk, dv = hk // GLA_HEADS, hv // GLA_HEADS
    nb = 8
    row = lambda w: pl.BlockSpec((nb, w), lambda i: (i, 0))
    sspec = pl.BlockSpec((nb, GLA_HEADS, dk, dv), lambda i: (i, 0, 0, 0))
    return pl.pallas_call(
        functools.partial(_gla_step_kernel, nb=nb, dk=dk, dv=dv),
        grid=(n // nb,),
        in_specs=[row(hk), row(hk), row(hv), row(hk), sspec],
        out_specs=[row(hv), sspec],
        out_shape=[jax.ShapeDtypeStruct((n, hv), F32),
                   jax.ShapeDtypeStruct((n, GLA_HEADS, dk, dv), F32)],
        compiler_params=_params("arbitrary"),
        name="gla_step",
    )(q, k, v, gk, s0)


def _residual_ffn_norm(x, m, ga, fnorm, scf, shf, x1_ref, hf_ref):
    x1 = x + ga * m
    x1_ref[0] = x1
    hf_ref[0] = (_rms(x1, fnorm) * (1.0 + scf) + shf).astype(BF16)


def _gla_out_kernel(o_ref, r_ref, x_ref, ga_ref, shf_ref, scf_ref, onorm_ref, wout_ref, fnorm_ref,
                    x1_ref, hf_ref, *, dv):
    parts = []
    for h in range(GLA_HEADS):
        vs = slice(h * dv, (h + 1) * dv)
        y = _rms(o_ref[0, :, vs], onorm_ref[...]) * _silu(r_ref[0, :, vs])
        parts.append(y.astype(BF16))
    m = _dot(jnp.concatenate(parts, axis=-1), wout_ref[...])
    _residual_ffn_norm(x_ref[0], m, ga_ref[0], fnorm_ref[...], scf_ref[0], shf_ref[0], x1_ref, hf_ref)


def _mla_out_kernel(o_ref, x_ref, ga_ref, shf_ref, scf_ref, wout_ref, fnorm_ref, x1_ref, hf_ref):
    m = _dot(o_ref[0], wout_ref[...])
    _residual_ffn_norm(x_ref[0], m, ga_ref[0], fnorm_ref[...], scf_ref[0], shf_ref[0], x1_ref, hf_ref)


def _mixer_out(kernel_fn, acts, x, mod, consts, tm, name):
    bsz, L, d = x.shape
    row = lambda a: pl.BlockSpec((1, tm, a.shape[-1]), lambda b, t: (b, t, 0))
    return pl.pallas_call(
        kernel_fn,
        grid=(bsz, L // tm),
        in_specs=[row(a) for a in acts] + [row(x), _mod_spec(mod, tm, d, 2), _mod_spec(mod, tm, d, 3),
                                           _mod_spec(mod, tm, d, 4)] + [_const_spec(a) for a in consts],
        out_specs=[row(x), row(x)],
        out_shape=[jax.ShapeDtypeStruct((bsz, L, d), F32), jax.ShapeDtypeStruct((bsz, L, d), BF16)],
        compiler_params=_params("arbitrary", "arbitrary"),
        name=name,
    )(*acts, x, mod, mod, mod, *consts)


def _moe_kernel(hf_ref, x1_ref, gf_ref, wr_ref, br_ref, wg_ref, wu_ref, wd_ref, y_ref, acc_scr, gate_scr, grp_scr):
    g = pl.program_id(2)
    hf = hf_ref[0]

    @pl.when(g == 0)
    def _():
        logits = _dot(hf, wr_ref[...]) + br_ref[...]
        lane = lax.broadcasted_iota(jnp.int32, logits.shape, 1)
        big = jnp.int32(1 << 20)
        is_grp = lane < MOE_GROUPS
        lg = jnp.where(is_grp, logits, -jnp.inf)
        mg = jnp.max(lg, axis=-1, keepdims=True)
        g_val = 1.0 / jnp.sum(jnp.exp(lg - mg), axis=-1, keepdims=True)
        g_idx = jnp.min(jnp.where(is_grp & (lg == mg), lane, big), axis=-1, keepdims=True)
        lo = MOE_GROUPS + g_idx * MOE_EXP_PER_GROUP
        in_grp = (lane >= lo) & (lane < lo + MOE_EXP_PER_GROUP)
        le = jnp.where(in_grp, logits, -jnp.inf)
        me = jnp.max(le, axis=-1, keepdims=True)
        pe = jnp.exp(le - me)
        p = pe / jnp.sum(pe, axis=-1, keepdims=True)
        p1 = jnp.max(p, axis=-1, keepdims=True)
        i1 = jnp.min(jnp.where(in_grp & (p == p1), lane, big), axis=-1, keepdims=True)
        rest = jnp.where(in_grp & (lane != i1), p, -1.0)
        p2 = jnp.max(rest, axis=-1, keepdims=True)
        i2 = jnp.min(jnp.where(rest == p2, lane, big), axis=-1, keepdims=True)
        norm = g_val / (p1 + p2)
        w1, w2 = p1 * norm, p2 * norm
        for i in range(MOE_EXP_PER_GROUP):
            col = jnp.where(i1 == lo + i, w1, 0.0) + jnp.where(i2 == lo + i, w2, 0.0)
            gate_scr[i] = jnp.broadcast_to(col, gate_scr.shape[1:])
        grp_scr[...] = jnp.broadcast_to(g_idx, grp_scr.shape)
        acc_scr[...] = jnp.zeros_like(acc_scr)

    mine = grp_scr[...] == g
    total = None
    for i in range(MOE_EXP_PER_GROUP):
        gate = jnp.where(mine, gate_scr[i], 0.0)
        hid = _silu(_dot(hf, wg_ref[i])) * _dot(hf, wu_ref[i])
        hid = hid * jnp.concatenate([gate] * (hid.shape[1] // LANES), axis=1)
        part = _dot(hid.astype(BF16), wd_ref[i])
        total = part if total is None else total + part
    acc_scr[...] += total

    @pl.when(g == pl.num_programs(2) - 1)
    def _():
        y_ref[0] = x1_ref[0] + gf_ref[0] * acc_scr[...]


def _moe(hf, x1, mod, w_route, b_route, w_gate, w_up, w_down, tm):
    bsz, L, d = x1.shape
    n_exp, _, d_ff = w_gate.shape
    row = pl.BlockSpec((1, tm, d), lambda b, t, e: (b, t, 0))
    if mod.shape[1] == 1:
        gspec = pl.BlockSpec((1, 1, d), lambda b, t, e: (b, 0, 5))
    else:
        gspec = pl.BlockSpec((1, tm, d), lambda b, t, e: (b, t, 5))
    return pl.pallas_call(
        _moe_kernel,
        grid=(bsz, L // tm, n_exp // MOE_EXP_PER_GROUP),
        in_specs=[row, row, gspec,
                  pl.BlockSpec(w_route.shape, lambda b, t, e: (0, 0)),
                  pl.BlockSpec(b_route.shape, lambda b, t, e: (0, 0)),
                  pl.BlockSpec((MOE_EXP_PER_GROUP, d, d_ff), lambda b, t, e: (e, 0, 0)),
                  pl.BlockSpec((MOE_EXP_PER_GROUP, d, d_ff), lambda b, t, e: (e, 0, 0)),
                  pl.BlockSpec((MOE_EXP_PER_GROUP, d_ff, d), lambda b, t, e: (e, 0, 0))],
        out_specs=row,
        out_shape=jax.ShapeDtypeStruct((bsz, L, d), F32),
        scratch_shapes=[pltpu.VMEM((tm, d), F32), pltpu.VMEM((MOE_EXP_PER_GROUP, tm, LANES), F32),
                        pltpu.VMEM((tm, LANES), jnp.int32)],
        compiler_params=_params("arbitrary", "arbitrary", "arbitrary"),
        name="moe",
    )(hf, x1, mod, w_route, b_route, w_gate, w_up, w_down)


def _mla_out_moe_kernel(o_ref, x_ref, ga_ref, shf_ref, scf_ref, gf_ref, wout_ref, fnorm_ref, wr_ref, br_ref,
                        wg_ref, wu_ref, wd_ref, y_ref, acc_scr, gate_scr, grp_scr, hf_scr):
    @pl.when(pl.program_id(2) == 0)
    def _():
        _mla_out_kernel(o_ref, x_ref, ga_ref, shf_ref, scf_ref, wout_ref, fnorm_ref, y_ref, hf_scr)

    _moe_kernel(hf_scr, y_ref, gf_ref, wr_ref, br_ref, wg_ref, wu_ref, wd_ref, y_ref, acc_scr, gate_scr, grp_scr)


def _mla_out_moe(o, x, mod, out_consts, w_route, b_route, w_gate, w_up, w_down, tm):
    bsz, L, d = x.shape
    n_exp, _, d_ff = w_gate.shape
    row = lambda a: pl.BlockSpec((1, tm, a.shape[-1]), lambda b, t, e: (b, t, 0))

    def mod_chunk(chunk):
        if mod.shape[1] == 1:
            return pl.BlockSpec((1, 1, d), lambda b, t, e: (b, 0, chunk))
        return pl.BlockSpec((1, tm, d), lambda b, t, e: (b, t, chunk))

    const = lambda a: pl.BlockSpec(a.shape, lambda b, t, e: (0,) * a.ndim)
    expert = lambda r, c: pl.BlockSpec((MOE_EXP_PER_GROUP, r, c), lambda b, t, e: (e, 0, 0))
    return pl.pallas_call(
        _mla_out_moe_kernel,
        grid=(bsz, L // tm, n_exp // MOE_EXP_PER_GROUP),
        in_specs=[row(o), row(x)] + [mod_chunk(c) for c in (2, 3, 4, 5)] + [const(a) for a in out_consts]
        + [const(w_route), const(b_route), expert(d, d_ff), expert(d, d_ff), expert(d_ff, d)],
        out_specs=row(x),
        out_shape=jax.ShapeDtypeStruct((bsz, L, d), F32),
        scratch_shapes=[pltpu.VMEM((tm, d), F32), pltpu.VMEM((MOE_EXP_PER_GROUP, tm, LANES), F32),
                        pltpu.VMEM((tm, LANES), jnp.int32), pltpu.VMEM((1, tm, d), BF16)],
        compiler_params=_params("arbitrary", "arbitrary", "arbitrary"),
        name="mla_out_moe",
    )(o, x, mod, mod, mod, mod, *out_consts, w_route, b_route, w_gate, w_up, w_down)


def _kv_kernel(x_ref, innorm_ref, wc_ref, wr_ref, wrot_ref, latnorm_ref, cos_ref, sin_ref, ckv_ref, kr_ref):
    hn = _rms(x_ref[0], innorm_ref[...]).astype(BF16)
    ckv_ref[0] = _rms(_dot(hn, wc_ref[...]), latnorm_ref[...])
    kr_ref[0] = _dot(hn, wr_ref[...]) * cos_ref[...] + _dot(hn, wrot_ref[...]) * sin_ref[...]


def _rope_spec(table, tm):
    if table.shape[0] == 1:
        return pl.BlockSpec((1, LANES), lambda b, t: (0, 0))
    return pl.BlockSpec((tm, LANES), lambda b, t: (t, 0))


def _shared_kv(x, consts, cos, sin, tm):
    bsz, L, d = x.shape
    rank = consts[1].shape[1]
    row = lambda n: pl.BlockSpec((1, tm, n), lambda b, t: (b, t, 0))
    return pl.pallas_call(
        _kv_kernel,
        grid=(bsz, L // tm),
        in_specs=[row(d)] + [_const_spec(a) for a in consts] + [_rope_spec(cos, tm), _rope_spec(sin, tm)],
        out_specs=[row(rank), row(LANES)],
        out_shape=[jax.ShapeDtypeStruct((bsz, L, rank), F32), jax.ShapeDtypeStruct((bsz, L, LANES), F32)],
        compiler_params=_params("arbitrary", "arbitrary"),
        name="shared_kv",
    )(x, *consts, cos, sin)


def _mla_q_kernel(x_ref, sh_ref, sc_ref, mnorm_ref, wdq_ref, qlat_ref, wuq_ref, wuqrot_ref, qnorm_ref,
                  cos_ref, sin_ref, q_ref, *, scale):
    hm = _rms(x_ref[0], mnorm_ref[...]) * (1.0 + sc_ref[0]) + sh_ref[0]
    cq = _rms(_dot(hm.astype(BF16), wdq_ref[...]), qlat_ref[...]).astype(BF16)
    qp = _dot(cq, wuq_ref[...])
    qr = _dot(cq, wuqrot_ref[...])
    for h in range(MLA_HEADS):
        hs = slice(h * LANES, (h + 1) * LANES)
        a = qp[:, hs] * cos_ref[...] + qr[:, hs] * sin_ref[...]
        inv = lax.rsqrt(jnp.sum(a * a, axis=-1, keepdims=True) * (1.0 / MLA_QK) + NORM_EPS)
        q_ref[0, :, hs] = (a * inv * (qnorm_ref[...] * scale)).astype(q_ref.dtype)


def _mla_q(x, mod, consts, cos, sin, tm, out_dtype, scale):
    bsz, L, d = x.shape
    n = MLA_HEADS * LANES
    row = lambda w: pl.BlockSpec((1, tm, w), lambda b, t: (b, t, 0))
    return pl.pallas_call(
        functools.partial(_mla_q_kernel, scale=scale),
        grid=(bsz, L // tm),
        in_specs=[row(d), _mod_spec(mod, tm, d, 0), _mod_spec(mod, tm, d, 1)] + [_const_spec(a) for a in consts]
        + [_rope_spec(cos, tm), _rope_spec(sin, tm)],
        out_specs=row(n),
        out_shape=jax.ShapeDtypeStruct((bsz, L, n), out_dtype),
        compiler_params=_params("arbitrary", "arbitrary"),
        name="mla_q",
    )(x, mod, mod, *consts, cos, sin)


def _kv_up_kernel(ckv_ref, kr_ref, wuk_ref, knorm_ref, *rest, vt_tile):
    if vt_tile:
        wuvt_ref, k_ref, vt_ref = rest
    else:
        (k_ref,) = rest
    c = ckv_ref[0].astype(BF16)
    kn = _dot(c, wuk_ref[...])
    for h in range(MLA_HEADS):
        hs = slice(h * LANES, (h + 1) * LANES)
        a = kn[:, hs] + kr_ref[0]
        inv = lax.rsqrt(jnp.sum(a * a, axis=-1, keepdims=True) * (1.0 / MLA_QK) + NORM_EPS)
        k_ref[0, :, hs] = (a * inv * knorm_ref[...]).astype(k_ref.dtype)
    if vt_tile:
        for s in range(c.shape[0] // vt_tile):
            vt = _dot_nt(wuvt_ref[...], c[s * vt_tile:(s + 1) * vt_tile])
            vt_ref[0, s] = vt.astype(vt_ref.dtype)


def _kv_up(ckv, kr_slot, consts, tm, out_dtype, vt_tile=0):
    bsz, L, rank = ckv.shape
    n = MLA_HEADS * LANES
    row = lambda w: pl.BlockSpec((1, tm, w), lambda b, t: (b, t, 0))
    out_specs = [row(n)]
    out_shape = [jax.ShapeDtypeStruct((bsz, L, n), out_dtype)]
    if vt_tile:
        nv = consts[-1].shape[0]
        out_specs.append(pl.BlockSpec((1, tm // vt_tile, nv, vt_tile), lambda b, t: (b, t, 0, 0)))
        out_shape.append(jax.ShapeDtypeStruct((bsz, L // vt_tile, nv, vt_tile), out_dtype))
    return pl.pallas_call(
        functools.partial(_kv_up_kernel, vt_tile=vt_tile),
        grid=(bsz, L // tm),
        in_specs=[row(rank), row(LANES)] + [_const_spec(a) for a in consts],
        out_specs=out_specs,
        out_shape=out_shape,
        compiler_params=_params("arbitrary", "arbitrary"),
        name="kv_up",
    )(ckv, kr_slot, *consts)


def _mla_pre_kernel(x_ref, sh_ref, sc_ref, *refs, scale, vt_tile, n_kv, n_q, n_up):
    kv_consts = refs[:n_kv]
    q_consts = refs[n_kv:n_kv + n_q]
    up_consts = refs[n_kv + n_q:n_kv + n_q + n_up]
    cos_ref, sin_ref, ckv_ref, kr_ref, q_ref, k_ref, vt_ref = refs[n_kv + n_q + n_up:]
    _kv_kernel(x_ref, *kv_consts, cos_ref, sin_ref, ckv_ref, kr_ref)
    _mla_q_kernel(x_ref, sh_ref, sc_ref, *q_consts, cos_ref, sin_ref, q_ref, scale=scale)
    _kv_up_kernel(ckv_ref, kr_ref, *up_consts, k_ref, vt_ref, vt_tile=vt_tile)


def _mla_pre(x, mod, kv_consts, q_consts, up_consts, cos, sin, tm, scale, vt_tile):
    bsz, L, d = x.shape
    rank = kv_consts[1].shape[1]
    n = MLA_HEADS * LANES
    nv = up_consts[-1].shape[0]
    consts = kv_consts + q_consts + up_consts
    row = lambda w: pl.BlockSpec((1, tm, w), lambda b, t: (b, t, 0))
    return pl.pallas_call(
        functools.partial(_mla_pre_kernel, scale=scale, vt_tile=vt_tile, n_kv=len(kv_consts), n_q=len(q_consts),
                          n_up=len(up_consts)),
        grid=(bsz, L // tm),
        in_specs=[row(d), _mod_spec(mod, tm, d, 0), _mod_spec(mod, tm, d, 1)] + [_const_spec(a) for a in consts]
        + [_rope_spec(cos, tm), _rope_spec(sin, tm)],
        out_specs=[row(rank), row(LANES), row(n), row(n),
                   pl.BlockSpec((1, tm // vt_tile, nv, vt_tile), lambda b, t: (b, t, 0, 0))],
        out_shape=[jax.ShapeDtypeStruct((bsz, L, rank), F32), jax.ShapeDtypeStruct((bsz, L, LANES), F32),
                   jax.ShapeDtypeStruct((bsz, L, n), BF16), jax.ShapeDtypeStruct((bsz, L, n), BF16),
                   jax.ShapeDtypeStruct((bsz, L // vt_tile, nv, vt_tile), BF16)],
        compiler_params=_params("arbitrary", "arbitrary"),
        name="mla_pre",
    )(x, mod, mod, *consts, cos, sin)


def _flash_kernel(q_ref, k_ref, vt_ref, o_ref, *, tq, tk):
    qi = pl.program_id(2)
    sub = tk // tq
    n_full = (qi * tq) // tk

    def update(carry, first_tile, n_tiles, diagonal):
        new = []
        off = pl.multiple_of(first_tile * tq, tq)
        scores = [_dot_nt(k_ref[0, pl.ds(off, n_tiles * tq), hh * LANES:(hh + 1) * LANES],
                          q_ref[0, :, hh * LANES:(hh + 1) * LANES]) for hh in range(len(carry))]
        for hh, (m, acc) in enumerate(carry):
            s = scores[hh]
            if diagonal:
                s = jnp.where(key_le_query, s, -jnp.inf)
            m_new = jnp.maximum(m, jnp.max(s, axis=0, keepdims=True))
            pb = jnp.exp2(s - m_new).astype(BF16)
            acc = acc * jnp.exp2(m - m_new)
            for t in range(n_tiles):
                vt = vt_ref[0, first_tile + t, hh * MLA_V:(hh + 1) * MLA_V, :]
                acc = acc + _dot(jnp.concatenate([vt, ones], axis=0), pb[t * tq:(t + 1) * tq])
            new.append((m_new, acc))
        return tuple(new)

    ones = jnp.ones((16, tq), BF16)
    key_le_query = (lax.broadcasted_iota(jnp.int32, (tq, tq), 0) <= lax.broadcasted_iota(jnp.int32, (tq, tq), 1))
    one = (jnp.full((1, tq), NEG_BIG, F32), jnp.zeros((MLA_V + 16, tq), F32))
    carry = lax.fori_loop(0, n_full, lambda c, cr: update(cr, c * sub, sub, False), (one,) * FLASH_HEADS)
    for r in range(1, sub):
        carry = lax.cond(qi - n_full * sub >= r,
                         lambda cr, r=r: update(cr, n_full * sub + (r - 1), 1, False), lambda cr: cr, carry)
    carry = update(carry, qi, 1, True)
    out_t = jnp.concatenate([acc[:MLA_V] / acc[MLA_V:MLA_V + 1] for (_, acc) in carry], axis=0)
    o_ref[0] = out_t.T.astype(o_ref.dtype)


def _flash(q, k, vt, tq):
    bsz, S, _ = q.shape
    nh = FLASH_HEADS
    qspec = pl.BlockSpec((1, tq, nh * LANES), lambda b, h, i: (b, i, h))
    kspec = pl.BlockSpec((1, S, nh * LANES), lambda b, h, i: (b, 0, h))
    vspec = pl.BlockSpec((1, S // tq, nh * MLA_V, tq), lambda b, h, i: (b, 0, h, 0))
    return pl.pallas_call(
        functools.partial(_flash_kernel, tq=tq, tk=min(FLASH_KEYS, S)),
        grid=(bsz, MLA_HEADS // nh, S // tq),
        in_specs=[qspec, kspec, vspec],
        out_specs=pl.BlockSpec((1, tq, nh * MLA_V), lambda b, h, i: (b, i, h)),
        out_shape=jax.ShapeDtypeStruct((bsz, S, MLA_HEADS * MLA_V), BF16),
        compiler_params=_params("arbitrary", "arbitrary", "arbitrary"),
        name="flash",
    )(q, k, vt)


def _paged_kernel(pt_ref, ckv_hbm, krt_hbm, q_ref, knew_ref, cnew_ref, knorm_ref, wukt_ref, wukt_slot_ref,
                  wuv_slot_ref, o_ref, m_scr, l_scr, acc_scr, lhs_scr, c_scr, cbuf, krbuf, sems, *, n_pages):
    b, j = pl.program_id(0), pl.program_id(1)
    steps = pl.num_programs(1)
    step = b * steps + j
    last_step = pl.num_programs(0) * steps - 1
    slot = step % 2
    nh = MLA_HEADS
    n_kn = nh * MLA_NOPE
    page = ckv_hbm.shape[1]
    tile = 2 * page
    head = lax.broadcasted_iota(jnp.int32, (nh, 1), 0)

    def page_copies(pid, p, to_slot):
        return (pltpu.make_async_copy(ckv_hbm.at[pid], cbuf.at[to_slot, pl.ds(p * page, page), :],
                                      sems.at[to_slot, 0]),
                pltpu.make_async_copy(krt_hbm.at[pid], krbuf.at[to_slot, :, pl.ds(p * page, page)],
                                      sems.at[to_slot, 1]))

    def start_pages(for_step, to_slot, positions):
        fb, fj = for_step // steps, for_step % steps
        for p in positions:
            for cp in page_copies(pt_ref[fb, fj * n_pages + p], p, to_slot):
                cp.start()

    def wait_pages(in_slot):
        for p in range(n_pages):
            for cp in page_copies(0, p, in_slot):
                cp.wait()

    @pl.when(step == 0)
    def _():
        start_pages(step, slot, range(n_pages))

    qg = q_ref[0] * knorm_ref[...]

    @pl.when(j == 0)
    def _():
        m_scr[...] = jnp.full_like(m_scr, NEG_BIG)
        l_scr[...] = jnp.zeros_like(l_scr)
        acc_scr[...] = jnp.zeros_like(acc_scr)
        qb = qg.astype(BF16)
        qabs = jnp.zeros((nh, lhs_scr.shape[1]), F32)
        for h in range(nh):
            qabs = qabs + jnp.where(head == h, _dot(qb, wukt_slot_ref[h]), 0.0)
        lhs_scr[:n_kn, :] = wukt_ref[...]
        lhs_scr[n_kn:, :] = jnp.concatenate([qabs, jnp.zeros_like(qabs)], axis=0).astype(BF16)

    wait_pages(slot)
    nxt = jnp.minimum(step + 1, last_step)
    q_rope = qg[:, MLA_NOPE:MLA_QK].astype(BF16)
    scores = []
    for i in range(n_pages // 2):
        start_pages(nxt, 1 - slot, (2 * i, 2 * i + 1))
        c2 = cbuf[slot, i * tile:(i + 1) * tile, :].astype(BF16)
        c_scr[i * tile:(i + 1) * tile, :] = c2
        both = _dot_nt(lhs_scr[...], c2)
        knt = both[:n_kn]
        ss = jnp.sum((knt * knt).reshape(nh, MLA_NOPE, tile), axis=1)
        kr2 = krbuf[slot, :, i * tile:(i + 1) * tile]
        ss = ss + jnp.sum(kr2 * kr2, axis=0, keepdims=True)
        num = both[n_kn:n_kn + nh] + _dot(q_rope, kr2.astype(BF16))
        scores.append(num * lax.rsqrt(ss * (1.0 / MLA_QK) + NORM_EPS))
    s = jnp.concatenate(scores, axis=1)
    m = m_scr[...]
    m_new = jnp.maximum(m, jnp.max(s, axis=-1, keepdims=True))
    pr = jnp.exp(s - m_new)
    corr = jnp.exp(m - m_new)
    l = l_scr[...] * corr + jnp.sum(pr, axis=-1, keepdims=True)
    pb = pr.astype(BF16)
    pv = [None, None]
    for i in range(n_pages // 2):
        part = _dot(pb[:, i * tile:(i + 1) * tile], c_scr[i * tile:(i + 1) * tile, :])
        pv[i % 2] = part if pv[i % 2] is None else pv[i % 2] + part
    acc = acc_scr[...] * corr + (pv[0] + pv[1])
    m_scr[...] = m_new
    l_scr[...] = l
    acc_scr[...] = acc

    @pl.when(j == steps - 1)
    def _():
        s_new = jnp.sum(q_ref[0] * knew_ref[0], axis=-1, keepdims=True)
        m_fin = jnp.maximum(m_new, s_new)
        corr_fin = jnp.exp(m_new - m_fin)
        p_new = jnp.exp(s_new - m_fin)
        l_fin = l * corr_fin + p_new
        o_lat = ((acc * corr_fin + p_new * cnew_ref[0]) / l_fin).astype(BF16)
        out = jnp.zeros((nh, LANES), F32)
        for h in range(nh):
            out = out + jnp.where(head == h, _dot(o_lat, wuv_slot_ref[h]), 0.0)
        o_ref[0] = out

    @pl.when(step == last_step)
    def _():
        wait_pages(1 - slot)


def _paged_attention(q, k_new, c_new, cache_ckv, cache_krope_t, page_table, consts, n_pages):
    nb, nh, _ = q.shape
    _, page, rank = cache_ckv.shape
    rope = cache_krope_t.shape[1]
    steps = page_table.shape[1] // n_pages
    assert n_pages % 2 == 0 and steps * n_pages == page_table.shape[1]
    per_b = lambda a: pl.BlockSpec((1,) + a.shape[1:], lambda b, j, pt: (b,) + (0,) * (a.ndim - 1))
    const = lambda a: pl.BlockSpec(a.shape, lambda b, j, pt: (0,) * a.ndim)
    hbm = pl.BlockSpec(memory_space=pl.ANY)
    grid_spec = pltpu.PrefetchScalarGridSpec(
        num_scalar_prefetch=1,
        grid=(nb, steps),
        in_specs=[hbm, hbm, per_b(q), per_b(k_new), per_b(c_new)] + [const(a) for a in consts],
        out_specs=pl.BlockSpec((1, nh, LANES), lambda b, j, pt: (b, 0, 0)),
        scratch_shapes=[pltpu.VMEM((nh, 1), F32), pltpu.VMEM((nh, 1), F32), pltpu.VMEM((nh, rank), F32),
                        pltpu.VMEM((nh * MLA_NOPE + 2 * nh, rank), BF16),
                        pltpu.VMEM((n_pages * page, rank), BF16),
                        pltpu.VMEM((2, n_pages * page, rank), F32),
                        pltpu.VMEM((2, rope, n_pages * page), F32),
                        pltpu.SemaphoreType.DMA((2, 2))],
    )
    return pl.pallas_call(
        functools.partial(_paged_kernel, n_pages=n_pages),
        grid_spec=grid_spec,
        out_shape=jax.ShapeDtypeStruct((nb, nh, LANES), F32),
        compiler_params=_params("arbitrary", "arbitrary"),
        name="paged_attn",
    )(page_table, cache_ckv, cache_krope_t, q, k_new, c_new, *consts)


def _rotate_half_cols(w):
    half = w.shape[-1] // 2
    return jnp.concatenate([-w[..., half:], w[..., :half]], axis=-1)


def _slot(nope, rope):
    ref = nope if nope is not None else rope
    lead = ref.shape[:-1]
    z = lambda n: jnp.zeros(lead + (n,), ref.dtype)
    a = nope if nope is not None else z(MLA_NOPE)
    b = rope if rope is not None else z(MLA_ROPE)
    return jnp.concatenate([a, b, z(LANES - MLA_QK)], axis=-1)


def _rope_tables(pos):
    inv = ROPE_THETA ** (-jnp.arange(0, MLA_ROPE, 2, dtype=F32) / MLA_ROPE)
    ang = pos.astype(F32)[:, None] * inv[None, :]
    ang = jnp.concatenate([ang, ang], axis=-1)
    n = pos.shape[0]
    cos = _slot(jnp.ones((n, MLA_NOPE), F32), jnp.cos(ang))
    sin = _slot(jnp.zeros((n, MLA_NOPE), F32), jnp.sin(ang))
    return cos, sin


def _prepare(w):
    d = w['mix_norm'].shape[-1]
    p = {}
    p['route_w'] = [jnp.pad(jnp.concatenate([w['moe_w_group'][l], w['moe_w_router'][l]], axis=-1),
                            ((0, 0), (0, LANES - MOE_GROUPS - MOE_EXPERTS))).astype(BF16) for l in range(2)]
    p['route_b'] = [jnp.pad(jnp.concatenate([w['moe_b_group'][l], w['moe_b_router'][l]], axis=-1),
                            (0, LANES - MOE_GROUPS - MOE_EXPERTS)).reshape(1, LANES) for l in range(2)]
    p['moe_gate'] = w['moe_w_gate'].astype(BF16)
    p['moe_up'] = w['moe_w_up'].astype(BF16)
    p['moe_down'] = w['moe_w_down'].astype(BF16)
    p['gla_out'] = [w['gla_out_norm'][0].reshape(1, -1), w['gla_w_out'][0].astype(BF16),
                    w['ffn_norm'][0].reshape(1, d)]
    rank = w['kv_lat_norm'].shape[0]
    w_r = w['kv_w_dkv'][:, rank:]
    p['kv'] = [w['kv_in_norm'].reshape(1, d), w['kv_w_dkv'][:, :rank].astype(BF16),
               _slot(None, w_r).astype(BF16), _slot(None, _rotate_half_cols(w_r)).astype(BF16),
               w['kv_lat_norm'].reshape(1, rank)]
    q_rank = w['mla_w_dq'].shape[-1]
    wuq = w['mla_w_uq'][0].reshape(q_rank, MLA_HEADS, MLA_QK)
    wuq_n, wuq_r = wuq[..., :MLA_NOPE], wuq[..., MLA_NOPE:]
    p['mla_q'] = [w['mix_norm'][1].reshape(1, d), w['mla_w_dq'][0].astype(BF16),
                  w['mla_q_lat_norm'][0].reshape(1, q_rank),
                  _slot(wuq_n, wuq_r).reshape(q_rank, -1).astype(BF16),
                  _slot(None, _rotate_half_cols(wuq_r)).reshape(q_rank, -1).astype(BF16),
                  jnp.pad(w['mla_q_norm'][0], (0, LANES - MLA_QK)).reshape(1, LANES)]
    knorm = jnp.pad(w['kv_k_norm'], (0, LANES - MLA_QK)).reshape(1, LANES)
    wuk, wuv = w['kv_w_uk'], w['kv_w_uv']
    p['kv_up'] = [_slot(wuk, None).reshape(rank, -1).astype(BF16), knorm]
    p['wuv_t'] = wuv.reshape(rank, -1).T.astype(BF16)
    wukt = wuk.transpose(1, 2, 0)
    p['paged'] = [knorm, wukt.reshape(MLA_HEADS * MLA_NOPE, rank).astype(BF16),
                  jnp.pad(wukt, ((0, 0), (0, LANES - MLA_NOPE), (0, 0))).astype(BF16),
                  jnp.pad(wuv.transpose(1, 0, 2), ((0, 0), (0, 0), (0, LANES - MLA_V))).astype(BF16)]
    wo = w['mla_w_out'][0].reshape(MLA_HEADS, MLA_V, d)
    p['mla_out_slot'] = [jnp.pad(wo, ((0, 0), (0, LANES - MLA_V), (0, 0))).reshape(MLA_HEADS * LANES, d)
                         .astype(BF16), w['ffn_norm'][1].reshape(1, d)]
    p['mla_out'] = [w['mla_w_out'][0].astype(BF16), w['ffn_norm'][1].reshape(1, d)]
    return p


def _moe_layer(l, hf, x1, mod, p):
    return _moe(hf, x1, mod, p['route_w'][l], p['route_b'][l], p['moe_gate'][l], p['moe_up'][l],
                p['moe_down'][l], min(MOE_ROWS, x1.shape[1]))


def _trunk(x, mods, pos, gla_s0, paged, w, p, tm):
    cos, sin = _rope_tables(pos)
    gla_w = (w['mix_norm'][0], w['gla_w_in'][0], w['gla_w_gate_up'][0], w['gla_b_gate'][0])
    if paged is None:
        x1, hf, s_fin = _gla_layer(x, mods[0], gla_s0, _gla_proj_consts(*gla_w), p['gla_out'], tm)
    else:
        q, k, v, r, gk = _gla_proj(x, mods[0], *gla_w, tm)
        o, s_fin = _gla_step(q[0], k[0], v[0], gk[0], gla_s0)
        x1, hf = _mixer_out(functools.partial(_gla_out_kernel, dv=v.shape[-1] // GLA_HEADS), [o[None], r], x,
                            mods[0], p['gla_out'], tm, "gla_out")
    x2 = _moe_layer(0, hf, x1, mods[0], p)
    if paged is None:
        tq = min(FLASH_T, x.shape[1])
        ckv, kr_slot, qh, kh, vt = _mla_pre(x2, mods[1], p['kv'], p['mla_q'], p['kv_up'] + [p['wuv_t']], cos, sin,
                                            tm, MLA_QK ** -0.5 * math.log2(math.e), tq)
        o = _flash(qh, kh, vt, tq)
        out_consts = p['mla_out']
    else:
        n = x.shape[1]
        ckv, kr_slot = _shared_kv(x2, p['kv'], cos, sin, tm)
        qh = _mla_q(x2, mods[1], p['mla_q'], cos, sin, tm, F32, MLA_QK ** -0.5)
        (kh,) = _kv_up(ckv, kr_slot, p['kv_up'], tm, F32)
        o = _paged_attention(qh.reshape(n, MLA_HEADS, LANES), kh.reshape(n, MLA_HEADS, LANES),
                             ckv.reshape(n, 1, -1), paged[0], jnp.swapaxes(paged[1], 1, 2), paged[2], p['paged'],
                             min(PAGES_PER_STEP, paged[2].shape[1]))
        o = o.reshape(1, n, MLA_HEADS * LANES).astype(BF16)
        out_consts = p['mla_out_slot']
    y = _mla_out_moe(o, x2, mods[1], out_consts, p['route_w'][1], p['route_b'][1], p['moe_gate'][1],
                     p['moe_up'][1], p['moe_down'][1], min(MOE_ROWS, x2.shape[1]))
    return y, s_fin, ckv, kr_slot[..., MLA_NOPE:MLA_QK]


def _forward(x_prompt, x_sample, c_prompt, c_sample, state_gla, cache_ckv, cache_krope, page_table, w, past_len):
    bsz, seq, d = x_prompt.shape
    nd, dl, _ = x_sample.shape
    assert dl == 1, "decode path handles one new token per sequence"
    p = _prepare(w)
    mod = _ada_mod(jnp.concatenate([c_prompt, c_sample], axis=0), w['ada_w'], w['ada_b'])
    mods_p = [mod[l, :bsz].reshape(bsz, 1, -1) for l in range(2)]
    mods_s = [mod[l, bsz:].reshape(1, nd, -1) for l in range(2)]
    s0_p = jnp.zeros((bsz,) + state_gla.shape[2:], state_gla.dtype)
    y_p, s_p, ckv_p, kr_p = _trunk(x_prompt, mods_p, jnp.arange(seq), s0_p, None, w, p, min(512, seq))
    y_s, s_s, ckv_s, kr_s = _trunk(x_sample.reshape(1, nd, d), mods_s, jnp.full((1,), past_len), state_gla[0],
                                   (cache_ckv, cache_krope, page_table), w, p, nd)
    return (y_p, y_s.reshape(nd, 1, d), s_p[None], s_s[None], ckv_p, kr_p,
            ckv_s.reshape(nd, 1, -1), kr_s.reshape(nd, 1, -1))


def kernel(x_prompt, x_sample, c_prompt, c_sample, state_gla, cache_ckv, cache_krope, page_table, ada_w, ada_b, mix_norm, ffn_norm, gla_w_in, gla_w_gate_up, gla_b_gate, gla_out_norm, gla_w_out, mla_w_dq, mla_q_lat_norm, mla_w_uq, mla_q_norm, mla_w_out, kv_in_norm, kv_w_dkv, kv_lat_norm, kv_w_uk, kv_w_uv, kv_k_norm, moe_w_group, moe_b_group, moe_w_router, moe_b_router, moe_w_gate, moe_w_up, moe_w_down):
    w = dict(ada_w=ada_w, ada_b=ada_b, mix_norm=mix_norm, ffn_norm=ffn_norm,
             gla_w_in=gla_w_in, gla_w_gate_up=gla_w_gate_up, gla_b_gate=gla_b_gate,
             gla_out_norm=gla_out_norm, gla_w_out=gla_w_out,
             mla_w_dq=mla_w_dq, mla_q_lat_norm=mla_q_lat_norm, mla_w_uq=mla_w_uq,
             mla_q_norm=mla_q_norm, mla_w_out=mla_w_out,
             kv_in_norm=kv_in_norm, kv_w_dkv=kv_w_dkv, kv_lat_norm=kv_lat_norm,
             kv_w_uk=kv_w_uk, kv_w_uv=kv_w_uv, kv_k_norm=kv_k_norm,
             moe_w_group=moe_w_group, moe_b_group=moe_b_group, moe_w_router=moe_w_router,
             moe_b_router=moe_b_router, moe_w_gate=moe_w_gate, moe_w_up=moe_w_up, moe_w_down=moe_w_down)
    past_len = page_table.shape[1] * cache_ckv.shape[1]
    return _forward(x_prompt, x_sample, c_prompt, c_sample, state_gla, cache_ckv, cache_krope, page_table, w,
                    past_len)
```

```python
import functools
import math

import jax
import jax.numpy as jnp
from jax import lax
from jax.experimental import pallas as pl
from jax.experimental.pallas import tpu as pltpu

F32 = jnp.float32
BF16 = jnp.bfloat16

NORM_EPS = 1e-6
GLA_HEADS = 4
GLA_GATE_NORM = 16.0
GLA_CHUNK = 256
GLA_SAFE_EXPONENT = 80.0
MLA_HEADS = 8
MLA_NOPE = 64
MLA_ROPE = 32
MLA_QK = MLA_NOPE + MLA_ROPE
MLA_V = 64
ROPE_THETA = 10000.0
MOE_GROUPS = 4
MOE_EXP_PER_GROUP = 4
MOE_EXPERTS = MOE_GROUPS * MOE_EXP_PER_GROUP
LANES = 128
SUBLANES = 8
NEG_BIG = -1e30
FLASH_T = 512
FLASH_KEYS = 1024
MOE_ROWS = 1024
PAGES_PER_STEP = 64
FLASH_HEADS = 4
VMEM_LIMIT = 52 * 1024 * 1024


def _params(*sem):
    return pltpu.CompilerParams(dimension_semantics=sem, vmem_limit_bytes=VMEM_LIMIT)


def _dot(a, b):
    return jnp.dot(a, b, preferred_element_type=F32)


def _dot_nt(a, b):
    return lax.dot_general(a, b, (((1,), (1,)), ((), ())), preferred_element_type=F32)


def _dot_tn(a, b):
    return lax.dot_general(a, b, (((0,), (0,)), ((), ())), preferred_element_type=F32)


def _rms(x, g):
    return x * lax.rsqrt(jnp.mean(x * x, axis=-1, keepdims=True) + NORM_EPS) * g


def _silu(x):
    return x / (1.0 + jnp.exp(-x))


def _split_bf16(x):
    hi = x.astype(BF16)
    lo = (x - hi.astype(F32)).astype(BF16)
    return hi, lo


def _ada_kernel(c_ref, w_ref, b_ref, o_ref):
    a = _silu(c_ref[...]).astype(BF16)
    o_ref[0] = _dot(a, w_ref[0].astype(BF16)) + b_ref[0]


def _ada_mod(c, ada_w, ada_b):
    depth, d, n = ada_w.shape
    r = c.shape[0]
    tn = 1536
    return pl.pallas_call(
        _ada_kernel,
        grid=(depth, n // tn),
        in_specs=[
            pl.BlockSpec((r, d), lambda l, j: (0, 0)),
            pl.BlockSpec((1, d, tn), lambda l, j: (l, 0, j)),
            pl.BlockSpec((1, 1, tn), lambda l, j: (l, 0, j)),
        ],
        out_specs=pl.BlockSpec((1, r, tn), lambda l, j: (l, 0, j)),
        out_shape=jax.ShapeDtypeStruct((depth, r, n), F32),
        compiler_params=_params("arbitrary", "arbitrary"),
        name="ada_mod",
    )(c, ada_w, ada_b.reshape(depth, 1, n))


def _mod_spec(mod, tm, d, chunk):
    if mod.shape[1] == 1:
        return pl.BlockSpec((1, 1, d), lambda b, t: (b, 0, chunk))
    return pl.BlockSpec((1, tm, d), lambda b, t: (b, t, chunk))


def _const_spec(arr):
    nd = arr.ndim
    return pl.BlockSpec(arr.shape, lambda b, t: (0,) * nd)


def _gla_proj_kernel(x_ref, sh_ref, sc_ref, g_ref, wq_ref, wk_ref, wv_ref, wr_ref, wg_ref, wgu_ref, bg_ref,
                     q_ref, k_ref, v_ref, r_ref, gk_ref, *, q_scale):
    hm = _rms(x_ref[0], g_ref[...]) * (1.0 + sc_ref[0]) + sh_ref[0]
    hb = hm.astype(BF16)
    q_ref[0] = _dot(hb, wq_ref[...]) * q_scale
    k_ref[0] = _dot(hb, wk_ref[...])
    v_ref[0] = _dot(hb, wv_ref[...]).astype(BF16)
    r_ref[0] = _dot(hb, wr_ref[...])
    g_low = _dot(hb, wg_ref[...])
    z = _dot(g_low.astype(BF16), wgu_ref[...]) + bg_ref[...]
    log_sig = jnp.minimum(z, 0.0) - jnp.log(1.0 + jnp.exp(-jnp.abs(z)))
    gk_ref[0] = log_sig * (1.0 / GLA_GATE_NORM)


def _gla_proj_consts(norm_g, w_in, w_gate_up, b_gate):
    hk = w_gate_up.shape[1]
    hv = (w_in.shape[1] - 2 * hk - w_gate_up.shape[0]) // 2
    cuts = [0, hk, 2 * hk, 2 * hk + hv, 2 * hk + 2 * hv, w_in.shape[1]]
    pieces = [w_in[:, a:b].astype(BF16) for a, b in zip(cuts[:-1], cuts[1:])]
    return [norm_g.reshape(1, -1)] + pieces + [w_gate_up.astype(BF16), b_gate.reshape(1, hk)]


def _gla_proj(x, mod, norm_g, w_in, w_gate_up, b_gate, tm):
    bsz, L, d = x.shape
    consts = _gla_proj_consts(norm_g, w_in, w_gate_up, b_gate)
    hk, hv = consts[1].shape[1], consts[3].shape[1]
    row = lambda n: pl.BlockSpec((1, tm, n), lambda b, t: (b, t, 0))
    return pl.pallas_call(
        functools.partial(_gla_proj_kernel, q_scale=float((hk // GLA_HEADS) ** -0.5)),
        grid=(bsz, L // tm),
        in_specs=[row(d), _mod_spec(mod, tm, d, 0), _mod_spec(mod, tm, d, 1)] + [_const_spec(a) for a in consts],
        out_specs=[row(hk), row(hk), row(hv), row(hv), row(hk)],
        out_shape=[
            jax.ShapeDtypeStruct((bsz, L, hk), F32),
            jax.ShapeDtypeStruct((bsz, L, hk), F32),
            jax.ShapeDtypeStruct((bsz, L, hv), BF16),
            jax.ShapeDtypeStruct((bsz, L, hv), F32),
            jax.ShapeDtypeStruct((bsz, L, hk), F32),
        ],
        compiler_params=_params("arbitrary", "arbitrary"),
        name="gla_proj",
    )(x, mod, mod, *consts)


def _gla_layer_kernel(x_ref, sh_ref, sc_ref, ga_ref, shf_ref, scf_ref, s0_ref, mnorm_ref, wq_ref, wk_ref, wv_ref,
                      wr_ref, wg_ref, wgu_ref, bg_ref, onorm_ref, wout_ref, fnorm_ref,
                      x1_ref, hf_ref, sf_ref, st_scr, o_scr, b_scr, k_scr, *, chunk, q_scale):
    t = pl.program_id(1)
    n_heads, dv, dk = st_scr.shape

    @pl.when(t == 0)
    def _():
        for h in range(n_heads):
            st_scr[h] = s0_ref[0, h].T

    x = x_ref[0]
    tm = x.shape[0]
    n_chunks = tm // chunk
    hb = (_rms(x, mnorm_ref[...]) * (1.0 + sc_ref[0]) + sh_ref[0]).astype(BF16)
    q = _dot(hb, wq_ref[...]) * q_scale
    k = _dot(hb, wk_ref[...])
    v = _dot(hb, wv_ref[...]).astype(BF16)
    r = _dot(hb, wr_ref[...])
    z = _dot(_dot(hb, wg_ref[...]).astype(BF16), wgu_ref[...]) + bg_ref[...]
    gk = (jnp.minimum(z, 0.0) - jnp.log(1.0 + jnp.exp(-jnp.abs(z)))) * (1.0 / GLA_GATE_NORM)
    tri = jnp.where(lax.broadcasted_iota(jnp.int32, (tm, tm), 0) >= lax.broadcasted_iota(jnp.int32, (tm, tm), 1),
                    1.0, 0.0).astype(BF16)
    g_hi, g_lo = _split_bf16(gk)
    cum = _dot(tri, g_hi) + _dot(tri, g_lo)
    causal = (lax.broadcasted_iota(jnp.int32, (chunk, chunk), 0)
              >= lax.broadcasted_iota(jnp.int32, (chunk, chunk), 1))

    def local_gate(c, cols):
        b = cum[c * chunk:(c + 1) * chunk, cols]
        return b - cum[c * chunk - 1:c * chunk, cols] if c else b

    total_decay = jnp.concatenate([local_gate(c, slice(None))[chunk - 1:chunk] for c in range(n_chunks)], axis=0)
    factored_ok = jnp.max(-total_decay) <= GLA_SAFE_EXPONENT

    @pl.when(factored_ok)
    def _():
        for h in range(n_heads):
            ks = slice(h * dk, (h + 1) * dk)
            vs = slice(h * dv, (h + 1) * dv)
            st = st_scr[h]
            for c in range(n_chunks):
                rows = slice(c * chunk, (c + 1) * chunk)
                b = local_gate(c, ks)
                qt = (q[rows, ks] * jnp.exp(b)).astype(BF16)
                kt = (k[rows, ks] * jnp.exp(-b)).astype(BF16)
                vc = v[rows, vs]
                attn = jnp.where(causal, _dot_nt(qt, kt), 0.0).astype(BF16)
                o_scr[rows, vs] = _dot(attn, vc) + _dot_nt(qt, st.astype(BF16))
                st = (st + _dot_tn(vc, kt)) * jnp.exp(b[chunk - 1:chunk, :])
            st_scr[h] = st

    @pl.when(jnp.logical_not(factored_ok))
    def _():
        k_scr[...] = k
        lane = lax.broadcasted_iota(jnp.int32, (chunk, chunk), 1)
        for h in range(n_heads):
            ks = slice(h * dk, (h + 1) * dk)
            vs = slice(h * dv, (h + 1) * dv)
            st = st_scr[h]
            for c in range(n_chunks):
                rows = slice(c * chunk, (c + 1) * chunk)
                b = local_gate(c, ks)
                b_scr[rows, ks] = b
                qc = q[rows, ks]

                def key_rows(grp, attn, c=c, ks=ks, b=b, qc=qc):
                    first = pl.multiple_of(c * chunk + grp * SUBLANES, SUBLANES)
                    b_grp = b_scr[pl.ds(first, SUBLANES), ks]
                    k_grp = k_scr[pl.ds(first, SUBLANES), ks]
                    for i in range(SUBLANES):
                        decay = jnp.exp(jnp.minimum(b - b_grp[i:i + 1], 0.0))
                        col = jnp.sum(qc * k_grp[i:i + 1] * decay, axis=-1, keepdims=True)
                        attn = jnp.where(lane == grp * SUBLANES + i, col, attn)
                    return attn

                attn = lax.fori_loop(0, chunk // SUBLANES, key_rows, jnp.zeros((chunk, chunk), F32))
                attn = jnp.where(causal, attn, 0.0).astype(BF16)
                vc = v[rows, vs]
                b_end = b[chunk - 1:chunk, :]
                qt = (qc * jnp.exp(b)).astype(BF16)
                o_scr[rows, vs] = _dot(attn, vc) + _dot_nt(qt, st.astype(BF16))
                kd = (k[rows, ks] * jnp.exp(b_end - b)).astype(BF16)
                st = st * jnp.exp(b_end) + _dot_tn(vc, kd)
            st_scr[h] = st

    parts = []
    for h in range(n_heads):
        vs = slice(h * dv, (h + 1) * dv)
        parts.append((_rms(o_scr[:, vs], onorm_ref[...]) * _silu(r[:, vs])).astype(BF16))
    m = _dot(jnp.concatenate(parts, axis=-1), wout_ref[...])
    _residual_ffn_norm(x, m, ga_ref[0], fnorm_ref[...], scf_ref[0], shf_ref[0], x1_ref, hf_ref)

    @pl.when(t == pl.num_programs(1) - 1)
    def _():
        for h in range(n_heads):
            sf_ref[0, h] = st_scr[h].T


def _gla_layer(x, mod, s0, proj_consts, out_consts, tm):
    bsz, L, d = x.shape
    _, n_heads, dk, dv = s0.shape
    consts = proj_consts + out_consts
    row = lambda a: pl.BlockSpec((1, tm, a.shape[-1]), lambda b, t: (b, t, 0))
    sspec = pl.BlockSpec((1, n_heads, dk, dv), lambda b, t: (b, 0, 0, 0))
    return pl.pallas_call(
        functools.partial(_gla_layer_kernel, chunk=math.gcd(tm, GLA_CHUNK), q_scale=float(dk ** -0.5)),
        grid=(bsz, L // tm),
        in_specs=[row(x)] + [_mod_spec(mod, tm, d, c) for c in range(5)] + [sspec]
        + [_const_spec(a) for a in consts],
        out_specs=[row(x), row(x), sspec],
        out_shape=[jax.ShapeDtypeStruct((bsz, L, d), F32), jax.ShapeDtypeStruct((bsz, L, d), BF16),
                   jax.ShapeDtypeStruct(s0.shape, F32)],
        scratch_shapes=[pltpu.VMEM((n_heads, dv, dk), F32), pltpu.VMEM((tm, n_heads * dv), F32),
                        pltpu.VMEM((tm, n_heads * dk), F32), pltpu.VMEM((tm, n_heads * dk), F32)],
        compiler_params=_params("arbitrary", "arbitrary"),
        name="gla_layer",
    )(x, mod, mod, mod, mod, mod, s0, *consts)


def _gla_step_kernel(q_ref, k_ref, v_ref, g_ref, s_ref, o_ref, sn_ref, *, nb, dk, dv):
    eye = (lax.broadcasted_iota(jnp.int32, (dk, dk), 0) == lax.broadcasted_iota(jnp.int32, (dk, dk), 1))

    def column(row_vec):
        return jnp.sum(jnp.where(eye, row_vec, 0.0), axis=1, keepdims=True)

    for j in range(nb):
        for h in range(GLA_HEADS):
            ks = slice(h * dk, (h + 1) * dk)
            vs = slice(h * dv, (h + 1) * dv)
            decay = column(jnp.exp(g_ref[j:j + 1, ks]))
            kc = column(k_ref[j:j + 1, ks])
            qc = column(q_ref[j:j + 1, ks])
            s_new = s_ref[j, h] * decay + kc * v_ref[j:j + 1, vs].astype(F32)
            sn_ref[j, h] = s_new
            o_ref[j:j + 1, vs] = jnp.sum(qc * s_new, axis=0, keepdims=True)


def _gla_step(q, k, v, gk, s0):
    n, hk = q.shape
    hv = v.shape[-1]
    dk, dv = hk // GLA_HEADS, hv // GLA_HEADS
    nb = 8
    row = lambda w: pl.BlockSpec((nb, w), lambda i: (i, 0))
    sspec = pl.BlockSpec((nb, GLA_HEADS, dk, dv), lambda i: (i, 0, 0, 0))
    return pl.pallas_call(
        functools.partial(_gla_step_kernel, nb=nb, dk=dk, dv=dv),
        grid=(n // nb,),
        in_specs=[row(hk), row(hk), row(hv), row(hk), sspec],
        out_specs=[row(hv), sspec],
        out_shape=[jax.ShapeDtypeStruct((n, hv), F32),
                   jax.ShapeDtypeStruct((n, GLA_HEADS, dk, dv), F32)],
        compiler_params=_params("arbitrary"),
        name="gla_step",
    )(q, k, v, gk, s0)


def _residual_ffn_norm(x, m, ga, fnorm, scf, shf, x1_ref, hf_ref):
    x1 = x + ga * m
    x1_ref[0] = x1
    hf_ref[0] = (_rms(x1, fnorm) * (1.0 + scf) + shf).astype(BF16)


def _gla_out_kernel(o_ref, r_ref, x_ref, ga_ref, shf_ref, scf_ref, onorm_ref, wout_ref, fnorm_ref,
                    x1_ref, hf_ref, *, dv):
    parts = []
    for h in range(GLA_HEADS):
        vs = slice(h * dv, (h + 1) * dv)
        y = _rms(o_ref[0, :, vs], onorm_ref[...]) * _silu(r_ref[0, :, vs])
        parts.append(y.astype(BF16))
    m = _dot(jnp.concatenate(parts, axis=-1), wout_ref[...])
    _residual_ffn_norm(x_ref[0], m, ga_ref[0], fnorm_ref[...], scf_ref[0], shf_ref[0], x1_ref, hf_ref)


def _mla_out_kernel(o_ref, x_ref, ga_ref, shf_ref, scf_ref, wout_ref, fnorm_ref, x1_ref, hf_ref):
    m = _dot(o_ref[0], wout_ref[...])
    _residual_ffn_norm(x_ref[0], m, ga_ref[0], fnorm_ref[...], scf_ref[0], shf_ref[0], x1_ref, hf_ref)


def _mixer_out(kernel_fn, acts, x, mod, consts, tm, name):
    bsz, L, d = x.shape
    row = lambda a: pl.BlockSpec((1, tm, a.shape[-1]), lambda b, t: (b, t, 0))
    return pl.pallas_call(
        kernel_fn,
        grid=(bsz, L // tm),
        in_specs=[row(a) for a in acts] + [row(x), _mod_spec(mod, tm, d, 2), _mod_spec(mod, tm, d, 3),
                                           _mod_spec(mod, tm, d, 4)] + [_const_spec(a) for a in consts],
        out_specs=[row(x), row(x)],
        out_shape=[jax.ShapeDtypeStruct((bsz, L, d), F32), jax.ShapeDtypeStruct((bsz, L, d), BF16)],
        compiler_params=_params("arbitrary", "arbitrary"),
        name=name,
    )(*acts, x, mod, mod, mod, *consts)


def _moe_kernel(hf_ref, x1_ref, gf_ref, wr_ref, br_ref, wg_ref, wu_ref, wd_ref, y_ref, acc_scr, gate_scr, grp_scr):
    g = pl.program_id(2)
    hf = hf_ref[0]

    @pl.when(g == 0)
    def _():
        rows = 4 * SUBLANES
        eye = (lax.broadcasted_iota(jnp.int32, (LANES, LANES), 0)
               == lax.broadcasted_iota(jnp.int32, (LANES, LANES), 1))
        bias_col = jnp.sum(jnp.where(eye, br_ref[...], 0.0), axis=1, keepdims=True)[:rows]
        logits = lax.dot_general(wr_ref[...], hf, (((0,), (1,)), ((), ())),
                                 preferred_element_type=F32)[:rows] + bias_col
        r = lax.broadcasted_iota(jnp.int32, logits.shape, 0)
        big = jnp.int32(1 << 20)
        is_grp = r < MOE_GROUPS
        lg = jnp.where(is_grp, logits, -jnp.inf)
        mg = jnp.max(lg, axis=0, keepdims=True)
        g_val = 1.0 / jnp.sum(jnp.exp(lg - mg), axis=0, keepdims=True)
        g_idx = jnp.min(jnp.where(is_grp & (lg == mg), r, big), axis=0, keepdims=True)
        lo = MOE_GROUPS + g_idx * MOE_EXP_PER_GROUP
        in_grp = (r >= lo) & (r < lo + MOE_EXP_PER_GROUP)
        le = jnp.where(in_grp, logits, -jnp.inf)
        me = jnp.max(le, axis=0, keepdims=True)
        pe = jnp.exp(le - me)
        p = pe / jnp.sum(pe, axis=0, keepdims=True)
        p1 = jnp.max(p, axis=0, keepdims=True)
        i1 = jnp.min(jnp.where(in_grp & (p == p1), r, big), axis=0, keepdims=True)
        rest = jnp.where(in_grp & (r != i1), p, -1.0)
        p2 = jnp.max(rest, axis=0, keepdims=True)
        i2 = jnp.min(jnp.where(rest == p2, r, big), axis=0, keepdims=True)
        norm = g_val / (p1 + p2)
        w1, w2 = p1 * norm, p2 * norm
        tm = logits.shape[1]
        packed = jnp.zeros((LANES, tm), F32)
        rr = lax.broadcasted_iota(jnp.int32, (LANES, tm), 0)
        for i in range(MOE_EXP_PER_GROUP):
            row_i = jnp.where(i1 == lo + i, w1, 0.0) + jnp.where(i2 == lo + i, w2, 0.0)
            packed = jnp.where(rr == i, row_i, packed)
        packed = jnp.where(rr == MOE_EXP_PER_GROUP, g_idx.astype(F32), packed)
        cols = packed.T
        for i in range(MOE_EXP_PER_GROUP):
            gate_scr[i] = jnp.broadcast_to(cols[:, i:i + 1], gate_scr.shape[1:])
        grp_scr[...] = jnp.broadcast_to(cols[:, MOE_EXP_PER_GROUP:MOE_EXP_PER_GROUP + 1],
                                        grp_scr.shape).astype(jnp.int32)
        acc_scr[...] = jnp.zeros_like(acc_scr)

    mine = grp_scr[...] == g
    total = None
    for i in range(MOE_EXP_PER_GROUP):
        gate = jnp.where(mine, gate_scr[i], 0.0)
        hid = _silu(_dot(hf, wg_ref[i])) * _dot(hf, wu_ref[i])
        hid = hid * jnp.concatenate([gate] * (hid.shape[1] // LANES), axis=1)
        part = _dot(hid.astype(BF16), wd_ref[i])
        total = part if total is None else total + part
    acc_scr[...] += total

    @pl.when(g == pl.num_programs(2) - 1)
    def _():
        y_ref[0] = x1_ref[0] + gf_ref[0] * acc_scr[...]


def _moe(hf, x1, mod, w_route, b_route, w_gate, w_up, w_down, tm):
    bsz, L, d = x1.shape
    n_exp, _, d_ff = w_gate.shape
    row = pl.BlockSpec((1, tm, d), lambda b, t, e: (b, t, 0))
    if mod.shape[1] == 1:
        gspec = pl.BlockSpec((1, 1, d), lambda b, t, e: (b, 0, 5))
    else:
        gspec = pl.BlockSpec((1, tm, d), lambda b, t, e: (b, t, 5))
    return pl.pallas_call(
        _moe_kernel,
        grid=(bsz, L // tm, n_exp // MOE_EXP_PER_GROUP),
        in_specs=[row, row, gspec,
                  pl.BlockSpec(w_route.shape, lambda b, t, e: (0, 0)),
                  pl.BlockSpec(b_route.shape, lambda b, t, e: (0, 0)),
                  pl.BlockSpec((MOE_EXP_PER_GROUP, d, d_ff), lambda b, t, e: (e, 0, 0)),
                  pl.BlockSpec((MOE_EXP_PER_GROUP, d, d_ff), lambda b, t, e: (e, 0, 0)),
                  pl.BlockSpec((MOE_EXP_PER_GROUP, d_ff, d), lambda b, t, e: (e, 0, 0))],
        out_specs=row,
        out_shape=jax.ShapeDtypeStruct((bsz, L, d), F32),
        scratch_shapes=[pltpu.VMEM((tm, d), F32), pltpu.VMEM((MOE_EXP_PER_GROUP, tm, LANES), F32),
                        pltpu.VMEM((tm, LANES), jnp.int32)],
        compiler_params=_params("arbitrary", "arbitrary", "arbitrary"),
        name="moe",
    )(hf, x1, mod, w_route, b_route, w_gate, w_up, w_down)


def _mla_out_moe_kernel(o_ref, x_ref, ga_ref, shf_ref, scf_ref, gf_ref, wout_ref, fnorm_ref, wr_ref, br_ref,
                        wg_ref, wu_ref, wd_ref, y_ref, acc_scr, gate_scr, grp_scr, hf_scr):
    @pl.when(pl.program_id(2) == 0)
    def _():
        _mla_out_kernel(o_ref, x_ref, ga_ref, shf_ref, scf_ref, wout_ref, fnorm_ref, y_ref, hf_scr)

    _moe_kernel(hf_scr, y_ref, gf_ref, wr_ref, br_ref, wg_ref, wu_ref, wd_ref, y_ref, acc_scr, gate_scr, grp_scr)


def _mla_out_moe(o, x, mod, out_consts, w_route, b_route, w_gate, w_up, w_down, tm):
    bsz, L, d = x.shape
    n_exp, _, d_ff = w_gate.shape
    row = lambda a: pl.BlockSpec((1, tm, a.shape[-1]), lambda b, t, e: (b, t, 0))

    def mod_chunk(chunk):
        if mod.shape[1] == 1:
            return pl.BlockSpec((1, 1, d), lambda b, t, e: (b, 0, chunk))
        return pl.BlockSpec((1, tm, d), lambda b, t, e: (b, t, chunk))

    const = lambda a: pl.BlockSpec(a.shape, lambda b, t, e: (0,) * a.ndim)
    expert = lambda r, c: pl.BlockSpec((MOE_EXP_PER_GROUP, r, c), lambda b, t, e: (e, 0, 0))
    return pl.pallas_call(
        _mla_out_moe_kernel,
        grid=(bsz, L // tm, n_exp // MOE_EXP_PER_GROUP),
        in_specs=[row(o), row(x)] + [mod_chunk(c) for c in (2, 3, 4, 5)] + [const(a) for a in out_consts]
        + [const(w_route), const(b_route), expert(d, d_ff), expert(d, d_ff), expert(d_ff, d)],
        out_specs=row(x),
        out_shape=jax.ShapeDtypeStruct((bsz, L, d), F32),
        scratch_shapes=[pltpu.VMEM((tm, d), F32), pltpu.VMEM((MOE_EXP_PER_GROUP, tm, LANES), F32),
                        pltpu.VMEM((tm, LANES), jnp.int32), pltpu.VMEM((1, tm, d), BF16)],
        compiler_params=_params("arbitrary", "arbitrary", "arbitrary"),
        name="mla_out_moe",
    )(o, x, mod, mod, mod, mod, *out_consts, w_route, b_route, w_gate, w_up, w_down)


def _kv_kernel(x_ref, innorm_ref, wc_ref, wr_ref, wrot_ref, latnorm_ref, cos_ref, sin_ref, ckv_ref, kr_ref):
    hn = _rms(x_ref[0], innorm_ref[...]).astype(BF16)
    ckv_ref[0] = _rms(_dot(hn, wc_ref[...]), latnorm_ref[...])
    kr_ref[0] = _dot(hn, wr_ref[...]) * cos_ref[...] + _dot(hn, wrot_ref[...]) * sin_ref[...]


def _rope_spec(table, tm):
    if table.shape[0] == 1:
        return pl.BlockSpec((1, LANES), lambda b, t: (0, 0))
    return pl.BlockSpec((tm, LANES), lambda b, t: (t, 0))


def _shared_kv(x, consts, cos, sin, tm):
    bsz, L, d = x.shape
    rank = consts[1].shape[1]
    row = lambda n: pl.BlockSpec((1, tm, n), lambda b, t: (b, t, 0))
    return pl.pallas_call(
        _kv_kernel,
        grid=(bsz, L // tm),
        in_specs=[row(d)] + [_const_spec(a) for a in consts] + [_rope_spec(cos, tm), _rope_spec(sin, tm)],
        out_specs=[row(rank), row(LANES)],
        out_shape=[jax.ShapeDtypeStruct((bsz, L, rank), F32), jax.ShapeDtypeStruct((bsz, L, LANES), F32)],
        compiler_params=_params("arbitrary", "arbitrary"),
        name="shared_kv",
    )(x, *consts, cos, sin)


def _mla_q_kernel(x_ref, sh_ref, sc_ref, mnorm_ref, wdq_ref, qlat_ref, wuq_ref, wuqrot_ref, qnorm_ref,
                  cos_ref, sin_ref, q_ref, *, scale):
    hm = _rms(x_ref[0], mnorm_ref[...]) * (1.0 + sc_ref[0]) + sh_ref[0]
    cq = _rms(_dot(hm.astype(BF16), wdq_ref[...]), qlat_ref[...]).astype(BF16)
    qp = _dot(cq, wuq_ref[...])
    qr = _dot(cq, wuqrot_ref[...])
    for h in range(MLA_HEADS):
        hs = slice(h * LANES, (h + 1) * LANES)
        a = qp[:, hs] * cos_ref[...] + qr[:, hs] * sin_ref[...]
        inv = lax.rsqrt(jnp.sum(a * a, axis=-1, keepdims=True) * (1.0 / MLA_QK) + NORM_EPS)
        q_ref[0, :, hs] = (a * inv * (qnorm_ref[...] * scale)).astype(q_ref.dtype)


def _mla_q(x, mod, consts, cos, sin, tm, out_dtype, scale):
    bsz, L, d = x.shape
    n = MLA_HEADS * LANES
    row = lambda w: pl.BlockSpec((1, tm, w), lambda b, t: (b, t, 0))
    return pl.pallas_call(
        functools.partial(_mla_q_kernel, scale=scale),
        grid=(bsz, L // tm),
        in_specs=[row(d), _mod_spec(mod, tm, d, 0), _mod_spec(mod, tm, d, 1)] + [_const_spec(a) for a in consts]
        + [_rope_spec(cos, tm), _rope_spec(sin, tm)],
        out_specs=row(n),
        out_shape=jax.ShapeDtypeStruct((bsz, L, n), out_dtype),
        compiler_params=_params("arbitrary", "arbitrary"),
        name="mla_q",
    )(x, mod, mod, *consts, cos, sin)


def _kv_up_kernel(ckv_ref, kr_ref, wuk_ref, knorm_ref, *rest, vt_tile):
    if vt_tile:
        wuvt_ref, k_ref, vt_ref = rest
    else:
        (k_ref,) = rest
    c = ckv_ref[0].astype(BF16)
    kn = _dot(c, wuk_ref[...])
    for h in range(MLA_HEADS):
        hs = slice(h * LANES, (h + 1) * LANES)
        a = kn[:, hs] + kr_ref[0]
        inv = lax.rsqrt(jnp.sum(a * a, axis=-1, keepdims=True) * (1.0 / MLA_QK) + NORM_EPS)
        k_ref[0, :, hs] = (a * inv * knorm_ref[...]).astype(k_ref.dtype)
    if vt_tile:
        for s in range(c.shape[0] // vt_tile):
            vt = _dot_nt(wuvt_ref[...], c[s * vt_tile:(s + 1) * vt_tile])
            vt_ref[0, s] = vt.astype(vt_ref.dtype)


def _kv_up(ckv, kr_slot, consts, tm, out_dtype, vt_tile=0):
    bsz, L, rank = ckv.shape
    n = MLA_HEADS * LANES
    row = lambda w: pl.BlockSpec((1, tm, w), lambda b, t: (b, t, 0))
    out_specs = [row(n)]
    out_shape = [jax.ShapeDtypeStruct((bsz, L, n), out_dtype)]
    if vt_tile:
        nv = consts[-1].shape[0]
        out_specs.append(pl.BlockSpec((1, tm // vt_tile, nv, vt_tile), lambda b, t: (b, t, 0, 0)))
        out_shape.append(jax.ShapeDtypeStruct((bsz, L // vt_tile, nv, vt_tile), out_dtype))
    return pl.pallas_call(
        functools.partial(_kv_up_kernel, vt_tile=vt_tile),
        grid=(bsz, L // tm),
        in_specs=[row(rank), row(LANES)] + [_const_spec(a) for a in consts],
        out_specs=out_specs,
        out_shape=out_shape,
        compiler_params=_params("arbitrary", "arbitrary"),
        name="kv_up",
    )(ckv, kr_slot, *consts)


def _mla_pre_kernel(x_ref, sh_ref, sc_ref, *refs, scale, vt_tile, n_kv, n_q, n_up):
    kv_consts = refs[:n_kv]
    q_consts = refs[n_kv:n_kv + n_q]
    up_consts = refs[n_kv + n_q:n_kv + n_q + n_up]
    cos_ref, sin_ref, ckv_ref, kr_ref, q_ref, k_ref, vt_ref = refs[n_kv + n_q + n_up:]
    _kv_kernel(x_ref, *kv_consts, cos_ref, sin_ref, ckv_ref, kr_ref)
    _mla_q_kernel(x_ref, sh_ref, sc_ref, *q_consts, cos_ref, sin_ref, q_ref, scale=scale)
    _kv_up_kernel(ckv_ref, kr_ref, *up_consts, k_ref, vt_ref, vt_tile=vt_tile)


def _mla_pre(x, mod, kv_consts, q_consts, up_consts, cos, sin, tm, scale, vt_tile):
    bsz, L, d = x.shape
    rank = kv_consts[1].shape[1]
    n = MLA_HEADS * LANES
    nv = up_consts[-1].shape[0]
    consts = kv_consts + q_consts + up_consts
    row = lambda w: pl.BlockSpec((1, tm, w), lambda b, t: (b, t, 0))
    return pl.pallas_call(
        functools.partial(_mla_pre_kernel, scale=scale, vt_tile=vt_tile, n_kv=len(kv_consts), n_q=len(q_consts),
                          n_up=len(up_consts)),
        grid=(bsz, L // tm),
        in_specs=[row(d), _mod_spec(mod, tm, d, 0), _mod_spec(mod, tm, d, 1)] + [_const_spec(a) for a in consts]
        + [_rope_spec(cos, tm), _rope_spec(sin, tm)],
        out_specs=[row(rank), row(LANES), row(n), row(n),
                   pl.BlockSpec((1, tm // vt_tile, nv, vt_tile), lambda b, t: (b, t, 0, 0))],
        out_shape=[jax.ShapeDtypeStruct((bsz, L, rank), F32), jax.ShapeDtypeStruct((bsz, L, LANES), F32),
                   jax.ShapeDtypeStruct((bsz, L, n), BF16), jax.ShapeDtypeStruct((bsz, L, n), BF16),
                   jax.ShapeDtypeStruct((bsz, L // vt_tile, nv, vt_tile), BF16)],
        compiler_params=_params("arbitrary", "arbitrary"),
        name="mla_pre",
    )(x, mod, mod, *consts, cos, sin)


def _flash_kernel(q_ref, k_ref, vt_ref, o_ref, *, tq, tk):
    qi = pl.program_id(2)
    sub = tk // tq
    n_full = (qi * tq) // tk

    def update(carry, first_tile, n_tiles, diagonal):
        new = []
        off = pl.multiple_of(first_tile * tq, tq)
        scores = [_dot_nt(k_ref[0, pl.ds(off, n_tiles * tq), hh * LANES:(hh + 1) * LANES],
                          q_ref[0, :, hh * LANES:(hh + 1) * LANES]) for hh in range(len(carry))]
        for hh, (m, acc) in enumerate(carry):
            s = scores[hh]
            if diagonal:
                s = jnp.where(key_le_query, s, -jnp.inf)
            m_new = jnp.maximum(m, jnp.max(s, axis=0, keepdims=True))
            pb = jnp.exp2(s - m_new).astype(BF16)
            acc = acc * jnp.exp2(m - m_new)
            for t in range(n_tiles):
                vt = vt_ref[0, first_tile + t, hh * MLA_V:(hh + 1) * MLA_V, :]
                acc = acc + _dot(jnp.concatenate([vt, ones], axis=0), pb[t * tq:(t + 1) * tq])
            new.append((m_new, acc))
        return tuple(new)

    ones = jnp.ones((16, tq), BF16)
    key_le_query = (lax.broadcasted_iota(jnp.int32, (tq, tq), 0) <= lax.broadcasted_iota(jnp.int32, (tq, tq), 1))
    one = (jnp.full((1, tq), NEG_BIG, F32), jnp.zeros((MLA_V + 16, tq), F32))
    carry = lax.fori_loop(0, n_full, lambda c, cr: update(cr, c * sub, sub, False), (one,) * FLASH_HEADS)
    for r in range(1, sub):
        carry = lax.cond(qi - n_full * sub >= r,
                         lambda cr, r=r: update(cr, n_full * sub + (r - 1), 1, False), lambda cr: cr, carry)
    carry = update(carry, qi, 1, True)
    out_t = jnp.concatenate([acc[:MLA_V] / acc[MLA_V:MLA_V + 1] for (_, acc) in carry], axis=0)
    o_ref[0] = out_t.T.astype(o_ref.dtype)


def _flash(q, k, vt, tq):
    bsz, S, _ = q.shape
    nh = FLASH_HEADS
    qspec = pl.BlockSpec((1, tq, nh * LANES), lambda b, h, i: (b, i, h))
    kspec = pl.BlockSpec((1, S, nh * LANES), lambda b, h, i: (b, 0, h))
    vspec = pl.BlockSpec((1, S // tq, nh * MLA_V, tq), lambda b, h, i: (b, 0, h, 0))
    return pl.pallas_call(
        functools.partial(_flash_kernel, tq=tq, tk=min(FLASH_KEYS, S)),
        grid=(bsz, MLA_HEADS // nh, S // tq),
        in_specs=[qspec, kspec, vspec],
        out_specs=pl.BlockSpec((1, tq, nh * MLA_V), lambda b, h, i: (b, i, h)),
        out_shape=jax.ShapeDtypeStruct((bsz, S, MLA_HEADS * MLA_V), BF16),
        compiler_params=_params("arbitrary", "arbitrary", "arbitrary"),
        name="flash",
    )(q, k, vt)


def _paged_kernel(pt_ref, ckv_hbm, krt_hbm, q_ref, knew_ref, cnew_ref, knorm_ref, wukt_ref, wukt_slot_ref,
                  wuv_slot_ref, o_ref, m_scr, l_scr, acc_scr, lhs_scr, c_scr, cbuf, krbuf, sems, *, n_pages):
    b, j = pl.program_id(0), pl.program_id(1)
    steps = pl.num_programs(1)
    step = b * steps + j
    last_step = pl.num_programs(0) * steps - 1
    slot = step % 2
    nh = MLA_HEADS
    n_kn = nh * MLA_NOPE
    page = ckv_hbm.shape[1]
    tile = 2 * page
    head = lax.broadcasted_iota(jnp.int32, (nh, 1), 0)

    def page_copies(pid, p, to_slot):
        return (pltpu.make_async_copy(ckv_hbm.at[pid], cbuf.at[to_slot, pl.ds(p * page, page), :],
                                      sems.at[to_slot, 0]),
                pltpu.make_async_copy(krt_hbm.at[pid], krbuf.at[to_slot, :, pl.ds(p * page, page)],
                                      sems.at[to_slot, 1]))

    def start_pages(for_step, to_slot, positions):
        fb, fj = for_step // steps, for_step % steps
        for p in positions:
            for cp in page_copies(pt_ref[fb, fj * n_pages + p], p, to_slot):
                cp.start()

    def wait_pages(in_slot):
        for p in range(n_pages):
            for cp in page_copies(0, p, in_slot):
                cp.wait()

    @pl.when(step == 0)
    def _():
        start_pages(step, slot, range(n_pages))

    qg = q_ref[0] * knorm_ref[...]

    @pl.when(j == 0)
    def _():
        m_scr[...] = jnp.full_like(m_scr, NEG_BIG)
        l_scr[...] = jnp.zeros_like(l_scr)
        acc_scr[...] = jnp.zeros_like(acc_scr)
        qb = qg.astype(BF16)
        qabs = jnp.zeros((nh, lhs_scr.shape[1]), F32)
        for h in range(nh):
            qabs = qabs + jnp.where(head == h, _dot(qb, wukt_slot_ref[h]), 0.0)
        lhs_scr[:n_kn, :] = wukt_ref[...]
        lhs_scr[n_kn:, :] = jnp.concatenate([qabs, jnp.zeros_like(qabs)], axis=0).astype(BF16)

    wait_pages(slot)
    nxt = jnp.minimum(step + 1, last_step)
    q_rope = qg[:, MLA_NOPE:MLA_QK].astype(BF16)
    scores = []
    for i in range(n_pages // 2):
        start_pages(nxt, 1 - slot, (2 * i, 2 * i + 1))
        c2 = cbuf[slot, i * tile:(i + 1) * tile, :].astype(BF16)
        c_scr[i * tile:(i + 1) * tile, :] = c2
        both = _dot_nt(lhs_scr[...], c2)
        knt = both[:n_kn]
        ss = jnp.sum((knt * knt).reshape(nh, MLA_NOPE, tile), axis=1)
        kr2 = krbuf[slot, :, i * tile:(i + 1) * tile]
        ss = ss + jnp.sum(kr2 * kr2, axis=0, keepdims=True)
        num = both[n_kn:n_kn + nh] + _dot(q_rope, kr2.astype(BF16))
        scores.append(num * lax.rsqrt(ss * (1.0 / MLA_QK) + NORM_EPS))
    s = jnp.concatenate(scores, axis=1)
    m = m_scr[...]
    m_new = jnp.maximum(m, jnp.max(s, axis=-1, keepdims=True))
    pr = jnp.exp(s - m_new)
    corr = jnp.exp(m - m_new)
    l = l_scr[...] * corr + jnp.sum(pr, axis=-1, keepdims=True)
    pb = pr.astype(BF16)
    pv = [None, None]
    for i in range(n_pages // 2):
        part = _dot(pb[:, i * tile:(i + 1) * tile], c_scr[i * tile:(i + 1) * tile, :])
        pv[i % 2] = part if pv[i % 2] is None else pv[i % 2] + part
    acc = acc_scr[...] * corr + (pv[0] + pv[1])
    m_scr[...] = m_new
    l_scr[...] = l
    acc_scr[...] = acc

    @pl.when(j == steps - 1)
    def _():
        s_new = jnp.sum(q_ref[0] * knew_ref[0], axis=-1, keepdims=True)
        m_fin = jnp.maximum(m_new, s_new)
        corr_fin = jnp.exp(m_new - m_fin)
        p_new = jnp.exp(s_new - m_fin)
        l_fin = l * corr_fin + p_new
        o_lat = ((acc * corr_fin + p_new * cnew_ref[0]) / l_fin).astype(BF16)
        out = jnp.zeros((nh, LANES), F32)
        for h in range(nh):
            out = out + jnp.where(head == h, _dot(o_lat, wuv_slot_ref[h]), 0.0)
        o_ref[0] = out

    @pl.when(step == last_step)
    def _():
        wait_pages(1 - slot)


def _paged_attention(q, k_new, c_new, cache_ckv, cache_krope_t, page_table, consts, n_pages):
    nb, nh, _ = q.shape
    _, page, rank = cache_ckv.shape
    rope = cache_krope_t.shape[1]
    steps = page_table.shape[1] // n_pages
    assert n_pages % 2 == 0 and steps * n_pages == page_table.shape[1]
    per_b = lambda a: pl.BlockSpec((1,) + a.shape[1:], lambda b, j, pt: (b,) + (0,) * (a.ndim - 1))
    const = lambda a: pl.BlockSpec(a.shape, lambda b, j, pt: (0,) * a.ndim)
    hbm = pl.BlockSpec(memory_space=pl.ANY)
    grid_spec = pltpu.PrefetchScalarGridSpec(
        num_scalar_prefetch=1,
        grid=(nb, steps),
        in_specs=[hbm, hbm, per_b(q), per_b(k_new), per_b(c_new)] + [const(a) for a in consts],
        out_specs=pl.BlockSpec((1, nh, LANES), lambda b, j, pt: (b, 0, 0)),
        scratch_shapes=[pltpu.VMEM((nh, 1), F32), pltpu.VMEM((nh, 1), F32), pltpu.VMEM((nh, rank), F32),
                        pltpu.VMEM((nh * MLA_NOPE + 2 * nh, rank), BF16),
                        pltpu.VMEM((n_pages * page, rank), BF16),
                        pltpu.VMEM((2, n_pages * page, rank), F32),
                        pltpu.VMEM((2, rope, n_pages * page), F32),
                        pltpu.SemaphoreType.DMA((2, 2))],
    )
    return pl.pallas_call(
        functools.partial(_paged_kernel, n_pages=n_pages),
        grid_spec=grid_spec,
        out_shape=jax.ShapeDtypeStruct((nb, nh, LANES), F32),
        compiler_params=_params("arbitrary", "arbitrary"),
        name="paged_attn",
    )(page_table, cache_ckv, cache_krope_t, q, k_new, c_new, *consts)


def _rotate_half_cols(w):
    half = w.shape[-1] // 2
    return jnp.concatenate([-w[..., half:], w[..., :half]], axis=-1)


def _slot(nope, rope):
    ref = nope if nope is not None else rope
    lead = ref.shape[:-1]
    z = lambda n: jnp.zeros(lead + (n,), ref.dtype)
    a = nope if nope is not None else z(MLA_NOPE)
    b = rope if rope is not None else z(MLA_ROPE)
    return jnp.concatenate([a, b, z(LANES - MLA_QK)], axis=-1)


def _rope_tables(pos):
    inv = ROPE_THETA ** (-jnp.arange(0, MLA_ROPE, 2, dtype=F32) / MLA_ROPE)
    ang = pos.astype(F32)[:, None] * inv[None, :]
    ang = jnp.concatenate([ang, ang], axis=-1)
    n = pos.shape[0]
    cos = _slot(jnp.ones((n, MLA_NOPE), F32), jnp.cos(ang))
    sin = _slot(jnp.zeros((n, MLA_NOPE), F32), jnp.sin(ang))
    return cos, sin


def _prepare(w):
    d = w['mix_norm'].shape[-1]
    p = {}
    p['route_w'] = [jnp.pad(jnp.concatenate([w['moe_w_group'][l], w['moe_w_router'][l]], axis=-1),
                            ((0, 0), (0, LANES - MOE_GROUPS - MOE_EXPERTS))).astype(BF16) for l in range(2)]
    p['route_b'] = [jnp.pad(jnp.concatenate([w['moe_b_group'][l], w['moe_b_router'][l]], axis=-1),
                            (0, LANES - MOE_GROUPS - MOE_EXPERTS)).reshape(1, LANES) for l in range(2)]
    p['moe_gate'] = w['moe_w_gate'].astype(BF16)
    p['moe_up'] = w['moe_w_up'].astype(BF16)
    p['moe_down'] = w['moe_w_down'].astype(BF16)
    p['gla_out'] = [w['gla_out_norm'][0].reshape(1, -1), w['gla_w_out'][0].astype(BF16),
                    w['ffn_norm'][0].reshape(1, d)]
    rank = w['kv_lat_norm'].shape[0]
    w_r = w['kv_w_dkv'][:, rank:]
    p['kv'] = [w['kv_in_norm'].reshape(1, d), w['kv_w_dkv'][:, :rank].astype(BF16),
               _slot(None, w_r).astype(BF16), _slot(None, _rotate_half_cols(w_r)).astype(BF16),
               w['kv_lat_norm'].reshape(1, rank)]
    q_rank = w['mla_w_dq'].shape[-1]
    wuq = w['mla_w_uq'][0].reshape(q_rank, MLA_HEADS, MLA_QK)
    wuq_n, wuq_r = wuq[..., :MLA_NOPE], wuq[..., MLA_NOPE:]
    p['mla_q'] = [w['mix_norm'][1].reshape(1, d), w['mla_w_dq'][0].astype(BF16),
                  w['mla_q_lat_norm'][0].reshape(1, q_rank),
                  _slot(wuq_n, wuq_r).reshape(q_rank, -1).astype(BF16),
                  _slot(None, _rotate_half_cols(wuq_r)).reshape(q_rank, -1).astype(BF16),
                  jnp.pad(w['mla_q_norm'][0], (0, LANES - MLA_QK)).reshape(1, LANES)]
    knorm = jnp.pad(w['kv_k_norm'], (0, LANES - MLA_QK)).reshape(1, LANES)
    wuk, wuv = w['kv_w_uk'], w['kv_w_uv']
    p['kv_up'] = [_slot(wuk, None).reshape(rank, -1).astype(BF16), knorm]
    p['wuv_t'] = wuv.reshape(rank, -1).T.astype(BF16)
    wukt = wuk.transpose(1, 2, 0)
    p['paged'] = [knorm, wukt.reshape(MLA_HEADS * MLA_NOPE, rank).astype(BF16),
                  jnp.pad(wukt, ((0, 0), (0, LANES - MLA_NOPE), (0, 0))).astype(BF16),
                  jnp.pad(wuv.transpose(1, 0, 2), ((0, 0), (0, 0), (0, LANES - MLA_V))).astype(BF16)]
    wo = w['mla_w_out'][0].reshape(MLA_HEADS, MLA_V, d)
    p['mla_out_slot'] = [jnp.pad(wo, ((0, 0), (0, LANES - MLA_V), (0, 0))).reshape(MLA_HEADS * LANES, d)
                         .astype(BF16), w['ffn_norm'][1].reshape(1, d)]
    p['mla_out'] = [w['mla_w_out'][0].astype(BF16), w['ffn_norm'][1].reshape(1, d)]
    return p


def _moe_layer(l, hf, x1, mod, p):
    return _moe(hf, x1, mod, p['route_w'][l], p['route_b'][l], p['moe_gate'][l], p['moe_up'][l],
                p['moe_down'][l], min(MOE_ROWS, x1.shape[1]))


def _trunk(x, mods, pos, gla_s0, paged, w, p, tm):
    cos, sin = _rope_tables(pos)
    gla_w = (w['mix_norm'][0], w['gla_w_in'][0], w['gla_w_gate_up'][0], w['gla_b_gate'][0])
    if paged is None:
        x1, hf, s_fin = _gla_layer(x, mods[0], gla_s0, _gla_proj_consts(*gla_w), p['gla_out'], tm)
    else:
        q, k, v, r, gk = _gla_proj(x, mods[0], *gla_w, tm)
        o, s_fin = _gla_step(q[0], k[0], v[0], gk[0], gla_s0)
        x1, hf = _mixer_out(functools.partial(_gla_out_kernel, dv=v.shape[-1] // GLA_HEADS), [o[None], r], x,
                            mods[0], p['gla_out'], tm, "gla_out")
    x2 = _moe_layer(0, hf, x1, mods[0], p)
    if paged is None:
        tq = min(FLASH_T, x.shape[1])
        ckv, kr_slot, qh, kh, vt = _mla_pre(x2, mods[1], p['kv'], p['mla_q'], p['kv_up'] + [p['wuv_t']], cos, sin,
                                            tm, MLA_QK ** -0.5 * math.log2(math.e), tq)
        o = _flash(qh, kh, vt, tq)
        out_consts = p['mla_out']
    else:
        n = x.shape[1]
        ckv, kr_slot = _shared_kv(x2, p['kv'], cos, sin, tm)
        qh = _mla_q(x2, mods[1], p['mla_q'], cos, sin, tm, F32, MLA_QK ** -0.5)
        (kh,) = _kv_up(ckv, kr_slot, p['kv_up'], tm, F32)
        o = _paged_attention(qh.reshape(n, MLA_HEADS, LANES), kh.reshape(n, MLA_HEADS, LANES),
                             ckv.reshape(n, 1, -1), paged[0], jnp.swapaxes(paged[1], 1, 2), paged[2], p['paged'],
                             min(PAGES_PER_STEP, paged[2].shape[1]))
        o = o.reshape(1, n, MLA_HEADS * LANES).astype(BF16)
        out_consts = p['mla_out_slot']
    y = _mla_out_moe(o, x2, mods[1], out_consts, p['route_w'][1], p['route_b'][1], p['moe_gate'][1],
                     p['moe_up'][1], p['moe_down'][1], min(MOE_ROWS, x2.shape[1]))
    return y, s_fin, ckv, kr_slot[..., MLA_NOPE:MLA_QK]


def _forward(x_prompt, x_sample, c_prompt, c_sample, state_gla, cache_ckv, cache_krope, page_table, w, past_len):
    bsz, seq, d = x_prompt.shape
    nd, dl, _ = x_sample.shape
    assert dl == 1, "decode path handles one new token per sequence"
    p = _prepare(w)
    mod = _ada_mod(jnp.concatenate([c_prompt, c_sample], axis=0), w['ada_w'], w['ada_b'])
    mods_p = [mod[l, :bsz].reshape(bsz, 1, -1) for l in range(2)]
    mods_s = [mod[l, bsz:].reshape(1, nd, -1) for l in range(2)]
    s0_p = jnp.zeros((bsz,) + state_gla.shape[2:], state_gla.dtype)
    y_p, s_p, ckv_p, kr_p = _trunk(x_prompt, mods_p, jnp.arange(seq), s0_p, None, w, p, min(512, seq))
    y_s, s_s, ckv_s, kr_s = _trunk(x_sample.reshape(1, nd, d), mods_s, jnp.full((1,), past_len), state_gla[0],
                                   (cache_ckv, cache_krope, page_table), w, p, nd)
    return (y_p, y_s.reshape(nd, 1, d), s_p[None], s_s[None], ckv_p, kr_p,
            ckv_s.reshape(nd, 1, -1), kr_s.reshape(nd, 1, -1))


def kernel(x_prompt, x_sample, c_prompt, c_sample, state_gla, cache_ckv, cache_krope, page_table, ada_w, ada_b, mix_norm, ffn_norm, gla_w_in, gla_w_gate_up, gla_b_gate, gla_out_norm, gla_w_out, mla_w_dq, mla_q_lat_norm, mla_w_uq, mla_q_norm, mla_w_out, kv_in_norm, kv_w_dkv, kv_lat_norm, kv_w_uk, kv_w_uv, kv_k_norm, moe_w_group, moe_b_group, moe_w_router, moe_b_router, moe_w_gate, moe_w_up, moe_w_down):
    w = dict(ada_w=ada_w, ada_b=ada_b, mix_norm=mix_norm, ffn_norm=ffn_norm,
             gla_w_in=gla_w_in, gla_w_gate_up=gla_w_gate_up, gla_b_gate=gla_b_gate,
             gla_out_norm=gla_out_norm, gla_w_out=gla_w_out,
             mla_w_dq=mla_w_dq, mla_q_lat_norm=mla_q_lat_norm, mla_w_uq=mla_w_uq,
             mla_q_norm=mla_q_norm, mla_w_out=mla_w_out,
             kv_in_norm=kv_in_norm, kv_w_dkv=kv_w_dkv, kv_lat_norm=kv_lat_norm,
             kv_w_uk=kv_w_uk, kv_w_uv=kv_w_uv, kv_k_norm=kv_k_norm,
             moe_w_group=moe_w_group, moe_b_group=moe_b_group, moe_w_router=moe_w_router,
             moe_b_router=moe_b_router, moe_w_gate=moe_w_gate, moe_w_up=moe_w_up, moe_w_down=moe_w_down)
    past_len = page_table.shape[1] * cache_ckv.shape[1]
    return _forward(x_prompt, x_sample, c_prompt, c_sample, state_gla, cache_ckv, cache_krope, page_table, w,
                    past_len)
```
